```python
import jax
import jax.numpy as jnp
from jax import lax
import numpy as np

D_MODEL = 1024
BATCH = 32
SEQ = 2048
DEPTH = 2

GRID_W = 64
CTX_LEN = 256
CHUNK = 128
EPS = 1e-6
N_MOD = 6

A_GROUPS = 4
A_DIM = 512
A_GDIM = A_DIM // A_GROUPS
B_HEADS = 4
B_HEAD_DIM = 128
B_DIM = B_HEADS * B_HEAD_DIM
N_GATES = 4
MIX_DIM = A_DIM + B_DIM
K_OFF = 0
V_OFF = K_OFF + B_DIM
G_OFF = V_OFF + B_DIM
Q_OFF = G_OFF + N_GATES * B_HEADS
O_OFF = Q_OFF + B_DIM
U_OFF = O_OFF + B_DIM
VA_OFF = U_OFF + A_DIM
AB_PROJ = VA_OFF + A_DIM
C_HEADS = 4
C_QK_DIM = 256
C_V_DIM = 512
RK_OFF = 0
RV_OFF = RK_OFF + C_HEADS * C_QK_DIM
RQ_OFF = RV_OFF + C_HEADS * C_V_DIM
RG_OFF = RQ_OFF + C_HEADS * C_QK_DIM
C_PROJ = RG_OFF + C_HEADS * C_V_DIM
D_FF = 2816
N_EVEN = (DEPTH + 1) // 2
N_ODD = DEPTH // 2

kernel_name = "hybrid_sgu_mlstm_retention_dit"


def _rms(x, g):
    xf = x.astype(jnp.float32)
    y = xf * lax.rsqrt(jnp.mean(xf * xf, axis=-1, keepdims=True) + EPS)
    return (y * g).astype(x.dtype)


def _ln(x):
    xf = x.astype(jnp.float32)
    xc = xf - jnp.mean(xf, axis=-1, keepdims=True)
    return (xc * lax.rsqrt(jnp.mean(xc * xc, axis=-1, keepdims=True) + EPS)).astype(x.dtype)


def _heads(t, nh):
    b, l, w = t.shape
    return t.reshape(b, l, nh, w // nh).transpose(0, 2, 1, 3)


def _head_norm(y, g):
    b, nh, l, d = y.shape
    yn = _ln(y.astype(jnp.float32))
    return yn.transpose(0, 2, 1, 3).reshape(b, l, nh * d) * g


def _to_chunks(t):
    b, h, l = t.shape[:3]
    return jnp.moveaxis(t.reshape(b, h, l // CHUNK, CHUNK, *t.shape[3:]), 2, 0)


def _from_chunks(t):
    t = jnp.moveaxis(t, 0, 2)
    return t.reshape(t.shape[0], t.shape[1], -1, *t.shape[4:])


def _dwconv1d(x, w):
    xp = jnp.pad(x, ((0, 0), (1, 1), (0, 0)))
    return xp[:, :-2] * w[0] + xp[:, 1:-1] * w[1] + xp[:, 2:] * w[2]


def _dwconv_grid(x, w):
    b, s, ch = x.shape
    rows = s // GRID_W
    img = x.reshape(b, rows, GRID_W, ch)
    y = lax.conv_general_dilated(img, w[:, :, None, :], window_strides=(1, 1), padding="SAME",
                                 dimension_numbers=("NHWC", "HWIO", "NHWC"), feature_group_count=ch)
    return y.reshape(b, s, ch)


def _conv_glu(h, w_up, w_conv, w_down, grid):
    a, g = jnp.split(h @ w_up, 2, axis=-1)
    g = _dwconv_grid(g, w_conv) if grid else _dwconv1d(g, w_conv[1])
    return (jax.nn.gelu(g) * a) @ w_down


def _bidir_scan(scan_fn, init, q_l, in_l, q_c, in_c):
    outs_l, outs_c = [], []
    for d in range(2):
        rev = (lambda t: t) if d == 0 else (lambda t: None if t is None else jnp.flip(t, 2))
        state, o_c = scan_fn(d, [rev(t) for t in in_c], init, rev(q_c))
        _, o_l = scan_fn(d, [rev(t) for t in in_l], state, rev(q_l))
        outs_l.append(rev(o_l))
        outs_c.append(rev(o_c))
    y_c = None if q_c is None else outs_c[0] + outs_c[1]
    return outs_l[0] + outs_l[1], y_c


def _mlstm_scan(direction, inputs, state, q):
    k, v, gates = inputs
    with_q = q is not None
    ig = gates[..., 2 * direction]
    lf = jax.nn.log_sigmoid(gates[..., 2 * direction + 1])
    lower = jnp.tril(jnp.ones((CHUNK, CHUNK), dtype=bool))
    xs = (_to_chunks(k), _to_chunks(v), _to_chunks(ig), _to_chunks(lf))
    if with_q:
        xs = xs + (_to_chunks(q),)

    def step(carry, inp):
        c_mem, n_mem, m = carry
        kc, vc, ic, fc = inp[0], inp[1], inp[2], inp[3]
        bcum = jnp.cumsum(fc, axis=-1)
        btot = bcum[..., -1]
        src = btot[..., None] - bcum + ic
        m_new = jnp.maximum(btot + m, jnp.max(src, axis=-1))
        kw = kc * jnp.exp(src - m_new[..., None])[..., None]
        decay_prev = jnp.exp(btot + m - m_new)
        c_new = decay_prev[..., None, None] * c_mem + jnp.einsum("bhsk,bhsv->bhkv", kw, vc)
        n_new = decay_prev[..., None] * n_mem + jnp.sum(kw, axis=2)
        if not with_q:
            return (c_new, n_new, m_new), None
        qc = inp[4]
        log_d = jnp.where(lower, bcum[..., :, None] - bcum[..., None, :] + ic[..., None, :], -jnp.inf)
        log_prev = bcum + m[..., None]
        m_t = jnp.maximum(log_prev, jnp.max(log_d, axis=-1))
        scores = jnp.einsum("bhtk,bhsk->bhts", qc, kc) * jnp.exp(log_d - m_t[..., None])
        w_prev = jnp.exp(log_prev - m_t)
        num = jnp.einsum("bhts,bhsv->bhtv", scores, vc) + w_prev[..., None] * jnp.einsum("bhtk,bhkv->bhtv", qc, c_mem)
        den = jnp.sum(scores, axis=-1) + w_prev * jnp.einsum("bhtk,bhk->bht", qc, n_mem)
        out = num / jnp.maximum(jnp.abs(den), jnp.exp(-m_t))[..., None]
        return (c_new, n_new, m_new), out

    state, ys = lax.scan(step, state, xs)
    return state, (_from_chunks(ys) if with_q else None)


def _ret_scan(log_g, inputs, state, q):
    k, v = inputs
    with_q = q is not None
    lg = log_g[:, None]
    pos = jnp.arange(CHUNK, dtype=jnp.float32)
    rel = pos[:, None] - pos[None, :]
    d_intra = jnp.exp(jnp.where(rel >= 0, lg[..., None] * rel, -jnp.inf))
    zeta = jnp.exp(lg * (CHUNK - 1 - pos))
    xi = jnp.exp(lg * (pos + 1))
    g_chunk = jnp.exp(log_g * CHUNK)
    xs = (_to_chunks(k), _to_chunks(v))
    if with_q:
        xs = xs + (_to_chunks(q),)

    def step(r_mem, inp):
        kc, vc = inp[0], inp[1]
        r_new = g_chunk[:, None, None] * r_mem + jnp.einsum("bhsk,bhsv->bhkv", kc * zeta[:, :, None], vc)
        if not with_q:
            return r_new, None
        qc = inp[2]
        scores = jnp.einsum("bhtk,bhsk->bhts", qc, kc) * d_intra
        out = jnp.einsum("bhts,bhsv->bhtv", scores, vc) + xi[:, :, None] * jnp.einsum("bhtk,bhkv->bhtv", qc, r_mem)
        return r_new, out

    state, ys = lax.scan(step, state, xs)
    return state, (_from_chunks(ys) if with_q else None)


def _chunk_sgu(u, v, w_s, b_s):
    b, l, _ = v.shape
    vc = _ln(v).reshape(b, l // CHUNK, CHUNK, A_GROUPS, A_GDIM)
    mixed = jnp.einsum("gpq,bnqgd->bnpgd", w_s, vc) + b_s.T[:, :, None]
    return u * mixed.reshape(b, l, A_DIM)


def _mlstm_in(p, qk_conv, gate_b, with_q):
    b, l, _ = p.shape
    k = _heads(jax.nn.silu(_dwconv1d(p[..., K_OFF:V_OFF], qk_conv[:, B_DIM:])), B_HEADS) * B_HEAD_DIM ** -0.5
    v = _heads(p[..., V_OFF:G_OFF], B_HEADS)
    gates = (p[..., G_OFF:Q_OFF].reshape(b, l, N_GATES, B_HEADS).astype(jnp.float32) + gate_b).transpose(0, 3, 1, 2)
    q = _heads(jax.nn.silu(_dwconv1d(p[..., Q_OFF:O_OFF], qk_conv[:, :B_DIM])), B_HEADS) if with_q else None
    return q, [k, v, gates]


def _ab_mixer(h_l, h_c, w_in, qk_conv, gate_b, sgu_w, sgu_b, head_g, w_out, ctx_out):
    p_l = h_l @ w_in
    p_c = h_c @ (w_in if ctx_out else w_in[:, :Q_OFF])
    q_l, in_l = _mlstm_in(p_l, qk_conv, gate_b, True)
    q_c, in_c = _mlstm_in(p_c, qk_conv, gate_b, ctx_out)
    b = h_l.shape[0]
    init = (jnp.zeros((b, B_HEADS, B_HEAD_DIM, B_HEAD_DIM), jnp.float32),
            jnp.zeros((b, B_HEADS, B_HEAD_DIM), jnp.float32),
            jnp.zeros((b, B_HEADS), jnp.float32))
    r_l, r_c = _bidir_scan(_mlstm_scan, init, q_l, in_l, q_c, in_c)

    def merge(p, r):
        u = jax.nn.gelu(p[..., U_OFF:VA_OFF])
        va = jax.nn.gelu(p[..., VA_OFF:])
        o = _heads(jax.nn.sigmoid(p[..., O_OFF:U_OFF]), B_HEADS)
        mem = _head_norm(o * r, head_g).astype(u.dtype)
        return jnp.concatenate([_chunk_sgu(u, va, sgu_w, sgu_b), mem], axis=-1) @ w_out

    return merge(p_l, r_l), (merge(p_c, r_c) if ctx_out else None)


def _ret_in(p, with_q):
    k = _heads(p[..., RK_OFF:RV_OFF], C_HEADS) * C_QK_DIM ** -0.5
    v = _heads(p[..., RV_OFF:RQ_OFF], C_HEADS)
    q = _heads(p[..., RQ_OFF:RG_OFF], C_HEADS) if with_q else None
    return q, [k, v]


def _ret_mixer(h_l, h_c, w_in, decay_logit, head_g, w_out, ctx_out):
    p_l = h_l @ w_in
    p_c = h_c @ (w_in if ctx_out else w_in[:, :RQ_OFF])
    q_l, in_l = _ret_in(p_l, True)
    q_c, in_c = _ret_in(p_c, ctx_out)
    log_g = jax.nn.log_sigmoid(decay_logit.astype(jnp.float32))
    init = jnp.zeros((h_l.shape[0], C_HEADS, C_QK_DIM, C_V_DIM), jnp.float32)
    r_l, r_c = _bidir_scan(lambda d, inp, st, q: _ret_scan(log_g[d], inp, st, q), init, q_l, in_l, q_c, in_c)

    def merge(p, r):
        return (jax.nn.silu(p[..., RG_OFF:]) * _head_norm(r, head_g)) @ w_out

    return merge(p_l, r_l), (merge(p_c, r_c) if ctx_out else None)


def setup_inputs(seed: int = 0) -> dict:
    key = jax.random.key(seed)
    ks = iter(jax.random.split(key, 32))

    def nrm(shape, scale):
        return jax.random.normal(next(ks), shape, jnp.float32) * scale

    gamma0 = 1.0 - 2.0 ** (-5.0 - jnp.arange(C_HEADS, dtype=jnp.float32))
    decay_logit0 = jnp.log(gamma0) - jnp.log1p(-gamma0)
    gate_base = jnp.array([0.0, 4.0, 0.0, 4.0], jnp.float32)[:, None]
    return {
        "x": nrm((BATCH, SEQ, D_MODEL), 1.0),
        "c": nrm((BATCH, D_MODEL), 1.0),
        "ctx": nrm((BATCH, CTX_LEN, D_MODEL), 1.0),
        "c_ctx": nrm((D_MODEL,), 1.0),
        "ada_w": nrm((DEPTH, D_MODEL, N_MOD * D_MODEL), 0.5 * D_MODEL ** -0.5),
        "ada_b": nrm((DEPTH, N_MOD * D_MODEL), 0.02),
        "pre_g": 1.0 + nrm((DEPTH, 2, D_MODEL), 0.05),
        "post_g": 1.0 + nrm((DEPTH, 2, D_MODEL), 0.05),
        "ffn_up": nrm((DEPTH, D_MODEL, 2 * D_FF), D_MODEL ** -0.5),
        "ffn_conv": nrm((DEPTH, 3, 3, D_FF), 1.0 / 3.0),
        "ffn_down": nrm((DEPTH, D_FF, D_MODEL), D_FF ** -0.5),
        "ab_w_in": nrm((N_EVEN, D_MODEL, AB_PROJ), D_MODEL ** -0.5),
        "ab_qk_conv": nrm((N_EVEN, 3, 2 * B_DIM), 3.0 ** -0.5),
        "ab_gate_b": gate_base + nrm((N_EVEN, N_GATES, B_HEADS), 0.1),
        "ab_sgu_w": nrm((N_EVEN, A_GROUPS, CHUNK, CHUNK), CHUNK ** -0.5),
        "ab_sgu_b": nrm((N_EVEN, A_GROUPS, CHUNK), 0.02),
        "ab_head_g": 1.0 + nrm((N_EVEN, B_DIM), 0.05),
        "ab_w_out": nrm((N_EVEN, MIX_DIM, D_MODEL), MIX_DIM ** -0.5),
        "ret_w_in": nrm((N_ODD, D_MODEL, C_PROJ), D_MODEL ** -0.5),
        "ret_decay": decay_logit0 + nrm((N_ODD, 2, C_HEADS), 0.1),
        "ret_head_g": 1.0 + nrm((N_ODD, C_HEADS * C_V_DIM), 0.05),
        "ret_w_out": nrm((N_ODD, C_HEADS * C_V_DIM, D_MODEL), (C_HEADS * C_V_DIM) ** -0.5),
    }


def reference(x, c, ctx, c_ctx, ada_w, ada_b, pre_g, post_g, ffn_up, ffn_conv, ffn_down,
              ab_w_in, ab_qk_conv, ab_gate_b, ab_sgu_w, ab_sgu_b, ab_head_g, ab_w_out,
              ret_w_in, ret_decay, ret_head_g, ret_w_out):
    s_lat = jax.nn.silu(c)
    s_ctx = jax.nn.silu(c_ctx)
    for layer in range(DEPTH):
        last = layer == DEPTH - 1
        j = layer // 2
        m_l = [t[:, None, :] for t in jnp.split(s_lat @ ada_w[layer] + ada_b[layer], N_MOD, axis=-1)]
        m_c = jnp.split(s_ctx @ ada_w[layer] + ada_b[layer], N_MOD, axis=-1)
        h_l = _rms(x, pre_g[layer, 0]) * (1.0 + m_l[1]) + m_l[0]
        h_c = _rms(ctx, pre_g[layer, 0]) * (1.0 + m_c[1]) + m_c[0]
        if layer % 2 == 0:
            y_l, y_c = _ab_mixer(h_l, h_c, ab_w_in[j], ab_qk_conv[j], ab_gate_b[j], ab_sgu_w[j], ab_sgu_b[j],
                                 ab_head_g[j], ab_w_out[j], not last)
        else:
            y_l, y_c = _ret_mixer(h_l, h_c, ret_w_in[j], ret_decay[j], ret_head_g[j], ret_w_out[j], not last)
        x = x + (m_l[2] * _rms(y_l, post_g[layer, 0])).astype(x.dtype)
        f_l = _conv_glu(_rms(x, pre_g[layer, 1]) * (1.0 + m_l[4]) + m_l[3],
                        ffn_up[layer], ffn_conv[layer], ffn_down[layer], True)
        x = x + (m_l[5] * _rms(f_l, post_g[layer, 1])).astype(x.dtype)
        if not last:
            ctx = ctx + (m_c[2] * _rms(y_c, post_g[layer, 0])).astype(ctx.dtype)
            f_c = _conv_glu(_rms(ctx, pre_g[layer, 1]) * (1.0 + m_c[4]) + m_c[3],
                            ffn_up[layer], ffn_conv[layer], ffn_down[layer], False)
            ctx = ctx + (m_c[5] * _rms(f_c, post_g[layer, 1])).astype(ctx.dtype)
    return x
```

```python
import functools

import jax
import jax.numpy as jnp
from jax import lax
from jax.experimental import pallas as pl
from jax.experimental.pallas import tpu as pltpu

F32 = jnp.float32
BF16 = jnp.bfloat16
HIGHEST = lax.Precision.HIGHEST

D_MODEL = 1024
GRID_W = 64
CHUNK = 128
EPS = 1e-6
N_MOD = 6
A_GROUPS = 4
A_DIM = 512
B_HEADS = 4
B_HEAD_DIM = 128
B_DIM = 512
N_GATES = 4
C_HEADS = 4
C_QK_DIM = 256
C_V_DIM = 512
D_FF = 2816

GATE_LANES = 128
V7X_VMEM_BYTES = 64 * 1024 * 1024
MIB = 1024 * 1024


def _cparams(semantics, vmem_mib):
    assert vmem_mib * MIB < V7X_VMEM_BYTES
    return pltpu.CompilerParams(dimension_semantics=semantics, vmem_limit_bytes=vmem_mib * MIB)


def _gelu(x):
    return x * (0.5 * (1.0 + jnp.tanh(0.7978845608028654 * (x + 0.044715 * (x * x * x)))))


def _silu(x):
    return x * jax.nn.sigmoid(x)


def _rms(x, g):
    return x * lax.rsqrt(jnp.mean(x * x, axis=-1, keepdims=True) + EPS) * g


def _ln(x):
    xc = x - jnp.mean(x, axis=-1, keepdims=True)
    return xc * lax.rsqrt(jnp.mean(xc * xc, axis=-1, keepdims=True) + EPS)


def _dot(a, b):
    return jnp.dot(a, b, preferred_element_type=F32)


def _dot_nt(a, b):
    return lax.dot_general(a, b, (((1,), (1,)), ((), ())), preferred_element_type=F32)


def _dot_tn(a, b):
    return lax.dot_general(a, b, (((0,), (0,)), ((), ())), preferred_element_type=F32)


def _adaln_kernel(c_ref, w_ref, b_ref, o_ref):
    s = _silu(c_ref[...])
    o_ref[0] = _dot(s.astype(BF16), w_ref[0].astype(BF16)) + b_ref[0]


def _adaln(cs, ada_w, ada_b):
    depth, d, n = ada_w.shape
    mp = cs.shape[0]
    tn = 1024
    return pl.pallas_call(
        _adaln_kernel,
        grid=(depth, n // tn),
        in_specs=[pl.BlockSpec((mp, d), lambda l, j: (0, 0)),
                  pl.BlockSpec((1, d, tn), lambda l, j: (l, 0, j)),
                  pl.BlockSpec((1, 1, tn), lambda l, j: (l, 0, j))],
        out_specs=pl.BlockSpec((1, mp, tn), lambda l, j: (l, 0, j)),
        out_shape=jax.ShapeDtypeStruct((depth, mp, n), F32),
        compiler_params=_cparams(("arbitrary", "arbitrary"), 32),
        name="adaln",
    )(cs, ada_w, ada_b.reshape(depth, 1, n))


def _proj_kernel(x_ref, g_ref, sh_ref, sc_ref, w_ref, *rest, rows, sub, with_gates):
    if with_gates:
        wg_ref, p_ref, gate_ref, h_ref = rest
    else:
        p_ref, h_ref = rest

    @pl.when(pl.program_id(1) == 0)
    def _():
        for r0 in range(0, rows, sub):
            h = _rms(x_ref[0, r0:r0 + sub, :], g_ref[...]) * (1.0 + sc_ref[0]) + sh_ref[0]
            hb = h.astype(BF16)
            h_ref[r0:r0 + sub, :] = hb
            if with_gates:
                gate_ref[0, r0:r0 + sub, :] = _dot(hb, wg_ref[...])

    for r0 in range(0, rows, sub):
        p_ref[0, r0:r0 + sub, :] = _dot(h_ref[r0:r0 + sub, :], w_ref[...]).astype(BF16)


def _proj(x3, pre_g, shift, scale, w, wg=None):
    nb, rows, d = x3.shape
    n = w.shape[1]
    tn = 1024
    per_block = shift.shape[0] != 1
    mod_map = (lambda i, j: (i, 0, 0)) if per_block else (lambda i, j: (0, 0, 0))
    in_specs = [pl.BlockSpec((1, rows, d), lambda i, j: (i, 0, 0)),
                pl.BlockSpec((1, d), lambda i, j: (0, 0)),
                pl.BlockSpec((1, 1, d), mod_map),
                pl.BlockSpec((1, 1, d), mod_map),
                pl.BlockSpec((d, tn), lambda i, j: (0, j))]
    out_specs = [pl.BlockSpec((1, rows, tn), lambda i, j: (i, 0, j))]
    out_shape = [jax.ShapeDtypeStruct((nb, rows, n), BF16)]
    args = [x3, pre_g.reshape(1, d), shift, scale, w]
    if wg is not None:
        in_specs.append(pl.BlockSpec((d, GATE_LANES), lambda i, j: (0, 0)))
        out_specs.append(pl.BlockSpec((1, rows, GATE_LANES), lambda i, j: (i, 0, 0)))
        out_shape.append(jax.ShapeDtypeStruct((nb, rows, GATE_LANES), F32))
        args.append(wg)
    outs = pl.pallas_call(
        functools.partial(_proj_kernel, rows=rows, sub=min(512, rows), with_gates=wg is not None),
        grid=(nb, n // tn),
        in_specs=in_specs, out_specs=out_specs, out_shape=out_shape,
        scratch_shapes=[pltpu.VMEM((rows, d), BF16)],
        compiler_params=_cparams(("arbitrary", "arbitrary"), 48),
        name="proj",
    )(*args)
    return outs if wg is not None else outs[0]


def _mlstm_kernel(k_ref, v_ref, q_ref, o_ref, gates_ref, wq_ref, wk_ref, bias_ref, hg_ref,
                  c0_ref, n0_ref, m0_ref,
                  mem_ref, cf_ref, nf_ref, mf_ref,
                  gcol_ref, cumf_ref, cumb_ref, grow_ref, cumfr_ref, cumbr_ref, kc_ref, qc_ref, r_ref,
                  *, seq):
    T = CHUNK
    nc = seq // T
    row = lax.broadcasted_iota(jnp.int32, (T, T), 0)
    col = lax.broadcasted_iota(jnp.int32, (T, T), 1)
    lower = row >= col
    upper = col >= row

    def gate_body(c, carry):
        sl = pl.ds(pl.multiple_of(c * T, T), T)
        g = gates_ref[0, sl, :] + bias_ref[...]
        lf = jnp.minimum(g, 0.0) - jnp.log1p(jnp.exp(-jnp.abs(g)))
        cf = jnp.dot(lower.astype(F32), lf, precision=HIGHEST, preferred_element_type=F32)
        cb = jnp.dot(upper.astype(F32), lf, precision=HIGHEST, preferred_element_type=F32)
        gcol_ref[sl, :] = g
        cumf_ref[sl, :] = cf
        cumb_ref[sl, :] = cb
        grow_ref[c] = g.T[:16]
        cumfr_ref[c] = cf.T[:16]
        cumbr_ref[c] = cb.T[:16]
        return carry

    lax.fori_loop(0, nc, gate_body, 0)

    ridx = lax.broadcasted_iota(jnp.int32, (T, B_HEAD_DIM), 0)

    for h in range(B_HEADS):
        hs = slice(h * B_HEAD_DIM, (h + 1) * B_HEAD_DIM)

        def conv_body(c, carry, hs=hs):
            start = pl.multiple_of(c * T, T)
            sl = pl.ds(start, T)
            prev_start = pl.multiple_of(jnp.maximum(start - 16, 0), 16)
            next_start = pl.multiple_of(jnp.minimum(start + T, seq - 16), 16)
            has_prev = (c > 0).astype(F32)
            has_next = (c < nc - 1).astype(F32)
            for src_ref, w_ref, dst_ref, scale in ((k_ref, wk_ref, kc_ref, B_HEAD_DIM ** -0.5),
                                                   (q_ref, wq_ref, qc_ref, 1.0)):
                cur = src_ref[0, sl, hs].astype(F32)
                prev_row = src_ref[0, pl.ds(prev_start, 16), hs].astype(F32)[15:16] * has_prev
                next_row = src_ref[0, pl.ds(next_start, 16), hs].astype(F32)[0:1] * has_next
                w = w_ref[:, hs]
                xm = jnp.where(ridx == 0, prev_row, pltpu.roll(cur, 1, 0))
                xp = jnp.where(ridx == T - 1, next_row, pltpu.roll(cur, T - 1, 0))
                y = xm * w[0:1] + cur * w[1:2] + xp * w[2:3]
                dst_ref[sl, :] = (_silu(y) * scale).astype(BF16)
            return carry

        lax.fori_loop(0, nc, conv_body, 0)

        def chunk(c, carry, direction, hs=hs, h=h):
            c_mem, n_mem, m = carry
            sl = pl.ds(pl.multiple_of(c * T, T), T)
            ig_col_idx = 2 * direction * B_HEADS + h
            lf_col_idx = (2 * direction + 1) * B_HEADS + h
            cum_ref, cumr_ref, mask = (cumf_ref, cumfr_ref, lower) if direction == 0 else (cumb_ref, cumbr_ref, upper)
            q = qc_ref[sl, :]
            k = kc_ref[sl, :]
            v = v_ref[0, sl, hs]
            bc = cum_ref[sl, :][:, lf_col_idx:lf_col_idx + 1]
            br = cumr_ref[c, lf_col_idx:lf_col_idx + 1, :]
            ig_r = grow_ref[c, ig_col_idx:ig_col_idx + 1, :]
            ig_c = gcol_ref[sl, :][:, ig_col_idx:ig_col_idx + 1]
            btot = br[:, T - 1:T] if direction == 0 else br[:, 0:1]
            log_d = jnp.where(mask, bc - br + ig_r, -jnp.inf)
            log_prev = bc + m
            m_t = jnp.maximum(log_prev, jnp.max(log_d, axis=-1, keepdims=True))
            scores = _dot_nt(q, k) * jnp.exp(log_d - m_t)
            w_prev = jnp.exp(log_prev - m_t)
            num = _dot(scores.astype(BF16), v) + w_prev * _dot(q, c_mem.astype(BF16))
            den = (jnp.sum(scores, axis=-1, keepdims=True)
                   + w_prev * jnp.sum(q.astype(F32) * n_mem, axis=-1, keepdims=True))
            out = num / jnp.maximum(jnp.abs(den), jnp.exp(-m_t))
            src = btot - bc + ig_c
            m_new = jnp.maximum(btot + m, jnp.max(src, axis=0, keepdims=True))
            kw = k.astype(F32) * jnp.exp(src - m_new)
            decay = jnp.exp(btot + m - m_new)
            c_new = decay * c_mem + _dot_tn(kw.astype(BF16), v)
            n_new = decay * n_mem + jnp.sum(kw, axis=0, keepdims=True)
            return out, sl, (c_new, n_new, m_new)

        def fwd_body(i, carry):
            out, sl, carry = chunk(i, carry, 0)
            r_ref[sl, :] = out
            return carry

        def bwd_body(i, carry, hs=hs):
            out, sl, carry = chunk(nc - 1 - i, carry, 1)
            y = jax.nn.sigmoid(o_ref[0, sl, hs].astype(F32)) * (r_ref[sl, :] + out)
            mem_ref[0, sl, hs] = (_ln(y) * hg_ref[:, hs]).astype(BF16)
            return carry

        for direction, body in ((0, fwd_body), (1, bwd_body)):
            init = (c0_ref[0, h, direction], n0_ref[0, h, direction], m0_ref[0, h, direction][:, 0:1])
            c_fin, n_fin, m_fin = lax.fori_loop(0, nc, body, init)
            cf_ref[0, h, direction] = c_fin
            nf_ref[0, h, direction] = n_fin
            mf_ref[0, h, direction] = jnp.broadcast_to(m_fin, (1, B_HEAD_DIM))


def _mlstm(p, gates, wq, wk, bias_row, head_g, state):
    b, seq, _ = p.shape
    c0, n0, m0 = state
    hd = B_HEAD_DIM
    nc = seq // CHUNK
    sec = lambda idx: pl.BlockSpec((1, seq, B_DIM), lambda i, idx=idx: (i, 0, idx))
    full = lambda shape: pl.BlockSpec(shape, lambda i: (0,) * len(shape))
    st_c = pl.BlockSpec((1, B_HEADS, 2, hd, hd), lambda i: (i, 0, 0, 0, 0))
    st_v = pl.BlockSpec((1, B_HEADS, 2, 1, hd), lambda i: (i, 0, 0, 0, 0))
    return pl.pallas_call(
        functools.partial(_mlstm_kernel, seq=seq),
        grid=(b,),
        in_specs=[sec(0), sec(1), sec(2), sec(3),
                  pl.BlockSpec((1, seq, GATE_LANES), lambda i: (i, 0, 0)),
                  full((3, B_DIM)), full((3, B_DIM)), full((1, GATE_LANES)), full((1, B_DIM)),
                  st_c, st_v, st_v],
        out_specs=[pl.BlockSpec((1, seq, B_DIM), lambda i: (i, 0, 0)), st_c, st_v, st_v],
        out_shape=[jax.ShapeDtypeStruct((b, seq, B_DIM), BF16),
                   jax.ShapeDtypeStruct(c0.shape, F32),
                   jax.ShapeDtypeStruct(n0.shape, F32),
                   jax.ShapeDtypeStruct(m0.shape, F32)],
        scratch_shapes=[pltpu.VMEM((seq, GATE_LANES), F32), pltpu.VMEM((seq, GATE_LANES), F32),
                        pltpu.VMEM((seq, GATE_LANES), F32),
                        pltpu.VMEM((nc, 16, CHUNK), F32), pltpu.VMEM((nc, 16, CHUNK), F32),
                        pltpu.VMEM((nc, 16, CHUNK), F32),
                        pltpu.VMEM((seq, hd), BF16), pltpu.VMEM((seq, hd), BF16), pltpu.VMEM((seq, hd), F32)],
        compiler_params=_cparams(("arbitrary",), 48),
        name="mlstm",
    )(p, p, p, p, gates, wq, wk, bias_row, head_g, c0, n0, m0)


def _merge0_kernel(u_ref, va_ref, mem_ref, x_ref, sw_ref, sb_ref, wo_ref, pg_ref, gate_ref, out_ref, cat_ref, *, tm):
    T = CHUNK
    gd = A_DIM // A_GROUPS
    for c in range(tm // T):
        sl = slice(c * T, (c + 1) * T)
        vn = _ln(_gelu(va_ref[0, sl, :].astype(F32))).astype(BF16)
        u = _gelu(u_ref[0, sl, :].astype(F32))
        for g in range(A_GROUPS):
            gs = slice(g * gd, (g + 1) * gd)
            mixed = _dot(sw_ref[g], vn[:, gs]) + sb_ref[g]
            cat_ref[sl, gs] = (u[:, gs] * mixed).astype(BF16)
    cat_ref[:, A_DIM:] = mem_ref[0]
    y = _dot(cat_ref[...], wo_ref[...])
    out_ref[0] = x_ref[0] + gate_ref[0] * _rms(y, pg_ref[...])


def _merge0(p, mem, x, sgu_w, sgu_bb, w_out, post_g, gate, tm):
    nb = x.shape[0]
    d = D_MODEL
    per_block = gate.shape[0] != 1
    blocks_per_gate = nb // gate.shape[0]
    gate_map = (lambda i: (i // blocks_per_gate, 0, 0)) if per_block else (lambda i: (0, 0, 0))
    return pl.pallas_call(
        functools.partial(_merge0_kernel, tm=tm),
        grid=(nb,),
        in_specs=[pl.BlockSpec((1, tm, A_DIM), lambda i: (i, 0, 4)),
                  pl.BlockSpec((1, tm, A_DIM), lambda i: (i, 0, 5)),
                  pl.BlockSpec((1, tm, B_DIM), lambda i: (i, 0, 0)),
                  pl.BlockSpec((1, tm, d), lambda i: (i, 0, 0)),
                  pl.BlockSpec((A_GROUPS, CHUNK, CHUNK), lambda i: (0, 0, 0)),
                  pl.BlockSpec((A_GROUPS, CHUNK, CHUNK), lambda i: (0, 0, 0)),
                  pl.BlockSpec((A_DIM + B_DIM, d), lambda i: (0, 0)),
                  pl.BlockSpec((1, d), lambda i: (0, 0)),
                  pl.BlockSpec((1, 1, d), gate_map)],
        out_specs=pl.BlockSpec((1, tm, d), lambda i: (i, 0, 0)),
        out_shape=jax.ShapeDtypeStruct((nb, tm, d), F32),
        scratch_shapes=[pltpu.VMEM((tm, A_DIM + B_DIM), BF16)],
        compiler_params=_cparams(("arbitrary",), 32),
        name="merge0",
    )(p, p, mem, x, sgu_w, sgu_bb, w_out, post_g.reshape(1, d), gate)


FFN_TK = 256
FFN_RS = 256
FFN_HALO = 64


def _ffn_kernel(x_ref, g_ref, sh_ref, sc_ref, wa_ref, wg_ref, cw_ref, wd_ref, pg_ref, gate_ref,
                o_ref, h_ref, gs_ref, *, rows, width, grid2d, sub):
    k = pl.program_id(1)
    tk = FFN_TK
    rs = FFN_RS

    @pl.when(k == 0)
    def _():
        for r0 in range(0, rows, sub):
            h = _rms(x_ref[0, r0:r0 + sub, :], g_ref[...]) * (1.0 + sc_ref[0]) + sh_ref[0]
            h_ref[r0:r0 + sub, :] = h.astype(BF16)
            o_ref[0, r0:r0 + sub, :] = jnp.zeros((sub, D_MODEL), F32)
        gs_ref[0:FFN_HALO, :] = jnp.zeros((FFN_HALO, tk), F32)
        gs_ref[rows + FFN_HALO:rows + 2 * FFN_HALO, :] = jnp.zeros((FFN_HALO, tk), F32)

    for r0 in range(0, rows, sub):
        gs_ref[FFN_HALO + r0:FFN_HALO + r0 + sub, :] = _dot(h_ref[r0:r0 + sub, :], wg_ref[...])

    cw = cw_ref[...]
    colidx = lax.broadcasted_iota(jnp.int32, (rs, tk), 0) & (width - 1)

    def tile(i, carry):
        r0 = pl.multiple_of(i * rs, rs)
        gm = gs_ref[pl.ds(r0 + FFN_HALO, rs), :]
        if grid2d:
            gu = gs_ref[pl.ds(r0, rs), :]
            gd = gs_ref[pl.ds(r0 + 2 * FFN_HALO, rs), :]
            y0 = cw[0:1] * gu + cw[3:4] * gm + cw[6:7] * gd
            y1 = cw[1:2] * gu + cw[4:5] * gm + cw[7:8] * gd
            y2 = cw[2:3] * gu + cw[5:6] * gm + cw[8:9] * gd
        else:
            y0 = cw[3:4] * gm
            y1 = cw[4:5] * gm
            y2 = cw[5:6] * gm
        conv = (y1 + jnp.where(colidx == 0, 0.0, pltpu.roll(y0, 1, 0))
                + jnp.where(colidx == width - 1, 0.0, pltpu.roll(y2, rs - 1, 0)))
        a = _dot(h_ref[pl.ds(r0, rs), :], wa_ref[...])
        act = (_gelu(conv) * a).astype(BF16)
        o_ref[0, pl.ds(r0, rs), :] += _dot(act, wd_ref[...])
        return carry

    lax.fori_loop(0, rows // rs, tile, 0)

    @pl.when(k == pl.num_programs(1) - 1)
    def _():
        for r0 in range(0, rows, sub):
            f = o_ref[0, r0:r0 + sub, :]
            o_ref[0, r0:r0 + sub, :] = x_ref[0, r0:r0 + sub, :] + gate_ref[0] * _rms(f, pg_ref[...])


def _ffn(x3, pre_g, shift, scale, w_up, conv_w, w_down, post_g, gate, *, width, grid2d):
    nb, rows, d = x3.shape
    tk = FFN_TK
    ks = D_FF // tk
    assert rows % FFN_RS == 0 and FFN_RS % width == 0 and (not grid2d or width == FFN_HALO)
    per_block = shift.shape[0] != 1
    mod_map = (lambda i, k: (i, 0, 0)) if per_block else (lambda i, k: (0, 0, 0))
    vec = lambda: pl.BlockSpec((1, d), lambda i, k: (0, 0))
    mod = lambda: pl.BlockSpec((1, 1, d), mod_map)
    return pl.pallas_call(
        functools.partial(_ffn_kernel, rows=rows, width=width, grid2d=grid2d, sub=min(512, rows)),
        grid=(nb, ks),
        in_specs=[pl.BlockSpec((1, rows, d), lambda i, k: (i, 0, 0)),
                  vec(), mod(), mod(),
                  pl.BlockSpec((d, tk), lambda i, k: (0, k)),
                  pl.BlockSpec((d, tk), lambda i, k: (0, ks + k)),
                  pl.BlockSpec((9, tk), lambda i, k: (0, k)),
                  pl.BlockSpec((tk, d), lambda i, k: (k, 0)),
                  vec(), mod()],
        out_specs=pl.BlockSpec((1, rows, d), lambda i, k: (i, 0, 0)),
        out_shape=jax.ShapeDtypeStruct((nb, rows, d), F32),
        scratch_shapes=[pltpu.VMEM((rows, d), BF16), pltpu.VMEM((rows + 2 * FFN_HALO, tk), F32)],
        compiler_params=_cparams(("arbitrary", "arbitrary"), 56),
        name="ffn",
    )(x3, pre_g.reshape(1, d), shift, scale, w_up, w_up, conv_w, w_down, post_g.reshape(1, d), gate)


def _ret_decay_terms(dec_ref):
    T = CHUNK
    t_col = lax.broadcasted_iota(jnp.int32, (T, 1), 0).astype(F32)
    lg = []
    for direction in range(2):
        z = dec_ref[0, direction][0:1, 0:1]
        lg.append(jnp.minimum(z, 0.0) - jnp.log1p(jnp.exp(-jnp.abs(z))))
    scale = C_QK_DIM ** -0.5
    zeta_f = jnp.exp(lg[0] * (T - 1 - t_col)) * scale
    zeta_b = jnp.exp(lg[1] * t_col) * scale
    xi_f = jnp.exp(lg[0] * (t_col + 1.0))
    xi_b = jnp.exp(lg[1] * (T - t_col))
    g_f = jnp.exp(lg[0] * T)
    g_b = jnp.exp(lg[1] * T)
    return lg, scale, (zeta_f, zeta_b), (xi_f, xi_b), (g_f, g_b)


def _ret_state_kernel(k_ref, v_ref, dec_ref, r0_ref, rf_ref, *, seq):
    T = CHUNK
    nc = seq // T
    _, _, zeta, _, gch = _ret_decay_terms(dec_ref)
    for direction in range(2):
        rf_ref[0, 0, direction] = r0_ref[0, 0, direction]

        def body(i, carry, direction=direction):
            c = i if direction == 0 else nc - 1 - i
            sl = pl.ds(pl.multiple_of(c * T, T), T)
            kz = (k_ref[0, sl, :].astype(F32) * zeta[direction]).astype(BF16)
            rf_ref[0, 0, direction] = gch[direction] * rf_ref[0, 0, direction] + _dot_tn(kz, v_ref[0, sl, :])
            return carry

        lax.fori_loop(0, nc, body, 0)


def _ret_state(p, dec_b, r0):
    b, seq, _ = p.shape
    st = pl.BlockSpec((1, 1, 2, C_QK_DIM, C_V_DIM), lambda i, h: (i, h, 0, 0, 0))
    return pl.pallas_call(
        functools.partial(_ret_state_kernel, seq=seq),
        grid=(b, C_HEADS),
        in_specs=[pl.BlockSpec((1, seq, C_QK_DIM), lambda i, h: (i, 0, h)),
                  pl.BlockSpec((1, seq, C_V_DIM), lambda i, h: (i, 0, 2 + h)),
                  pl.BlockSpec((1, 2, 8, 128), lambda i, h: (h, 0, 0, 0)),
                  st],
        out_specs=st,
        out_shape=jax.ShapeDtypeStruct(r0.shape, F32),
        compiler_params=_cparams(("arbitrary", "arbitrary"), 32),
        name="ret_state",
    )(p, p, dec_b, r0)


def _ret_kernel(k_ref, v_ref, q_ref, g_ref, dec_ref, r0_ref, hg_ref, wo_ref, y_ref, r_ref, rf_ref, rb_ref, *, seq):
    T = CHUNK
    nc = seq // T
    h = pl.program_id(1)
    lg, scale, zeta, xi, gch = _ret_decay_terms(dec_ref)
    row = lax.broadcasted_iota(jnp.int32, (T, T), 0)
    col = lax.broadcasted_iota(jnp.int32, (T, T), 1)
    rel = (row - col).astype(F32)
    d_intra = (jnp.where(row >= col, jnp.exp(lg[0] * rel), 0.0)
               + jnp.where(col >= row, jnp.exp(-lg[1] * rel), 0.0)) * scale

    rf_ref[...] = r0_ref[0, 0, 0]
    rb_ref[...] = r0_ref[0, 0, 1]

    def fwd_body(c, carry):
        sl = pl.ds(pl.multiple_of(c * T, T), T)
        q = q_ref[0, sl, :]
        k = k_ref[0, sl, :]
        v = v_ref[0, sl, :]
        scores = _dot_nt(q, k) * d_intra
        r_ref[sl, :] = _dot(scores.astype(BF16), v) + xi[0] * _dot(q, rf_ref[...].astype(BF16))
        kz = (k.astype(F32) * zeta[0]).astype(BF16)
        rf_ref[...] = gch[0] * rf_ref[...] + _dot_tn(kz, v)
        return carry

    lax.fori_loop(0, nc, fwd_body, 0)

    def bwd_body(i, carry):
        c = nc - 1 - i
        sl = pl.ds(pl.multiple_of(c * T, T), T)
        q = q_ref[0, sl, :]
        k = k_ref[0, sl, :]
        v = v_ref[0, sl, :]
        r = r_ref[sl, :] + xi[1] * _dot(q, rb_ref[...].astype(BF16))
        kz = (k.astype(F32) * zeta[1]).astype(BF16)
        rb_ref[...] = gch[1] * rb_ref[...] + _dot_tn(kz, v)
        z = (_silu(g_ref[0, sl, :].astype(F32)) * (_ln(r) * hg_ref[...])).astype(BF16)
        contrib = _dot(z, wo_ref[...])

        @pl.when(h == 0)
        def _():
            y_ref[0, sl, :] = contrib

        @pl.when(h > 0)
        def _():
            y_ref[0, sl, :] += contrib

        return carry

    lax.fori_loop(0, nc, bwd_body, 0)


def _ret(p, dec_b, r0, head_g, w_out):
    b, seq, _ = p.shape
    d = D_MODEL
    return pl.pallas_call(
        functools.partial(_ret_kernel, seq=seq),
        grid=(b, C_HEADS),
        in_specs=[pl.BlockSpec((1, seq, C_QK_DIM), lambda i, h: (i, 0, h)),
                  pl.BlockSpec((1, seq, C_V_DIM), lambda i, h: (i, 0, 2 + h)),
                  pl.BlockSpec((1, seq, C_QK_DIM), lambda i, h: (i, 0, 12 + h)),
                  pl.BlockSpec((1, seq, C_V_DIM), lambda i, h: (i, 0, 8 + h)),
                  pl.BlockSpec((1, 2, 8, 128), lambda i, h: (h, 0, 0, 0)),
                  pl.BlockSpec((1, 1, 2, C_QK_DIM, C_V_DIM), lambda i, h: (i, h, 0, 0, 0)),
                  pl.BlockSpec((1, C_V_DIM), lambda i, h: (0, h)),
                  pl.BlockSpec((C_V_DIM, d), lambda i, h: (h, 0))],
        out_specs=pl.BlockSpec((1, seq, d), lambda i, h: (i, 0, 0)),
        out_shape=jax.ShapeDtypeStruct((b, seq, d), F32),
        scratch_shapes=[pltpu.VMEM((seq, C_V_DIM), F32),
                        pltpu.VMEM((C_QK_DIM, C_V_DIM), F32), pltpu.VMEM((C_QK_DIM, C_V_DIM), F32)],
        compiler_params=_cparams(("arbitrary", "arbitrary"), 48),
        name="ret",
    )(p, p, p, p, dec_b, r0, head_g, w_out)


def _postnorm_kernel(x_ref, y_ref, pg_ref, gate_ref, o_ref):
    o_ref[0] = x_ref[0] + gate_ref[0] * _rms(y_ref[0], pg_ref[...])


def _postnorm(x, y, post_g, gate, tm):
    nb, _, d = x.shape
    blocks_per_gate = nb // gate.shape[0]
    row = pl.BlockSpec((1, tm, d), lambda i: (i, 0, 0))
    return pl.pallas_call(
        _postnorm_kernel,
        grid=(nb,),
        in_specs=[row, row, pl.BlockSpec((1, d), lambda i: (0, 0)),
                  pl.BlockSpec((1, 1, d), lambda i: (i // blocks_per_gate, 0, 0))],
        out_specs=row,
        out_shape=jax.ShapeDtypeStruct(x.shape, F32),
        compiler_params=_cparams(("arbitrary",), 32),
        name="postnorm",
    )(x, y, post_g.reshape(1, d), gate)


def kernel(x, c, ctx, c_ctx, ada_w, ada_b, pre_g, post_g, ffn_up, ffn_conv, ffn_down,
           ab_w_in, ab_qk_conv, ab_gate_b, ab_sgu_w, ab_sgu_b, ab_head_g, ab_w_out,
           ret_w_in, ret_decay, ret_head_g, ret_w_out):
    b, seq, d = x.shape
    ctx_len = ctx.shape[1]
    depth = ada_w.shape[0]
    tm = 512
    ctx_rows = min(2048, b * ctx_len)

    n_cond = b + 1
    n_pad = -n_cond % 8
    cs = jnp.concatenate([c, c_ctx[None, :], jnp.zeros((n_pad, d), F32)], axis=0)
    mods = _adaln(cs, ada_w, ada_b)

    def mod_l(layer, i):
        return mods[layer, :b, i * d:(i + 1) * d].reshape(b, 1, d)

    def mod_c(layer, i):
        return mods[layer, b:b + 1, i * d:(i + 1) * d].reshape(1, 1, d)

    ffn_up_b = ffn_up.astype(BF16)
    ffn_down_b = ffn_down.astype(BF16)
    ffn_conv_r = ffn_conv.reshape(depth, 9, D_FF)

    def ffn_l(layer, xl):
        return _ffn(xl, pre_g[layer, 1], mod_l(layer, 3), mod_l(layer, 4), ffn_up_b[layer], ffn_conv_r[layer],
                    ffn_down_b[layer], post_g[layer, 1], mod_l(layer, 5), width=GRID_W, grid2d=True)

    def ffn_c(layer, xc):
        out = _ffn(xc.reshape(-1, ctx_rows, d), pre_g[layer, 1], mod_c(layer, 3), mod_c(layer, 4), ffn_up_b[layer],
                   ffn_conv_r[layer], ffn_down_b[layer], post_g[layer, 1], mod_c(layer, 5),
                   width=ctx_len, grid2d=False)
        return out.reshape(xc.shape)

    w_in = ab_w_in[0]
    g_lo = 2 * B_DIM
    g_hi = g_lo + N_GATES * B_HEADS
    w_main = jnp.concatenate([w_in[:, :g_lo], w_in[:, g_hi:]], axis=1).astype(BF16)
    w_gate = jnp.pad(w_in[:, g_lo:g_hi], ((0, 0), (0, GATE_LANES - (g_hi - g_lo)))).astype(BF16)
    bias_row = jnp.pad(ab_gate_b[0].reshape(1, -1), ((0, 0), (0, GATE_LANES - N_GATES * B_HEADS)))
    wq = ab_qk_conv[0][:, :B_DIM]
    wk = ab_qk_conv[0][:, B_DIM:]
    head_g0 = ab_head_g[0].reshape(1, B_DIM)
    sgu_w = ab_sgu_w[0].astype(BF16)
    sgu_bb = jnp.broadcast_to(ab_sgu_b[0][:, :, None], (A_GROUPS, CHUNK, CHUNK))
    w_out0 = ab_w_out[0].astype(BF16)

    p_c, gates_c = _proj(ctx.reshape(-1, ctx_rows, d), pre_g[0, 0], mod_c(0, 0), mod_c(0, 1), w_main, w_gate)
    p_l, gates_l = _proj(x, pre_g[0, 0], mod_l(0, 0), mod_l(0, 1), w_main, w_gate)
    p_c = p_c.reshape(b, ctx_len, -1)
    gates_c = gates_c.reshape(b, ctx_len, GATE_LANES)
    zero_state = (jnp.zeros((b, B_HEADS, 2, B_HEAD_DIM, B_HEAD_DIM), F32),
                  jnp.zeros((b, B_HEADS, 2, 1, B_HEAD_DIM), F32),
                  jnp.zeros((b, B_HEADS, 2, 1, B_HEAD_DIM), F32))
    mem_c, *ctx_state = _mlstm(p_c, gates_c, wq, wk, bias_row, head_g0, zero_state)
    mem_l, *_ = _mlstm(p_l, gates_l, wq, wk, bias_row, head_g0, tuple(ctx_state))

    rt = lambda t: t.reshape(-1, tm, t.shape[-1])
    x = _merge0(rt(p_l), rt(mem_l), rt(x), sgu_w, sgu_bb, w_out0, post_g[0, 0], mod_l(0, 2), tm).reshape(b, seq, d)
    ctx = _merge0(rt(p_c), rt(mem_c), rt(ctx), sgu_w, sgu_bb, w_out0, post_g[0, 0], mod_c(0, 2), tm).reshape(b, ctx_len, d)
    x = ffn_l(0, x)
    ctx = ffn_c(0, ctx)

    w_ret = ret_w_in[0].astype(BF16)
    kv_cols = C_HEADS * (C_QK_DIM + C_V_DIM)
    dec_b = jnp.broadcast_to(ret_decay[0].T[:, :, None, None], (C_HEADS, 2, 8, 128))
    p_c = _proj(ctx.reshape(-1, ctx_rows, d), pre_g[1, 0], mod_c(1, 0), mod_c(1, 1), w_ret[:, :kv_cols])
    p_l = _proj(x, pre_g[1, 0], mod_l(1, 0), mod_l(1, 1), w_ret)
    r_zero = jnp.zeros((b, C_HEADS, 2, C_QK_DIM, C_V_DIM), F32)
    r_ctx = _ret_state(p_c.reshape(b, ctx_len, -1), dec_b, r_zero)
    y = _ret(p_l, dec_b, r_ctx, ret_head_g[0].reshape(1, -1), ret_w_out[0].astype(BF16))
    x = _postnorm(rt(x), rt(y), post_g[1, 0], mod_l(1, 2), tm).reshape(b, seq, d)
    x = ffn_l(1, x)
    return x
```

```python
import functools

import jax
import jax.numpy as jnp
from jax import lax
from jax.experimental import pallas as pl
from jax.experimental.pallas import tpu as pltpu

F32 = jnp.float32
BF16 = jnp.bfloat16
HIGHEST = lax.Precision.HIGHEST

D_MODEL = 1024
GRID_W = 64
CHUNK = 128
EPS = 1e-6
N_MOD = 6
A_GROUPS = 4
A_DIM = 512
B_HEADS = 4
B_HEAD_DIM = 128
B_DIM = 512
N_GATES = 4
C_HEADS = 4
C_QK_DIM = 256
C_V_DIM = 512
D_FF = 2816

LANES = 128
GATE_LANES = 2 * LANES
N_CHAINS = 2 * B_HEADS
V7X_VMEM_BYTES = 64 * 1024 * 1024
MIB = 1024 * 1024


def _cparams(semantics, vmem_mib):
    assert vmem_mib * MIB < V7X_VMEM_BYTES
    return pltpu.CompilerParams(dimension_semantics=semantics, vmem_limit_bytes=vmem_mib * MIB)


def _gelu(x):
    return x * (0.5 * (1.0 + jnp.tanh(0.7978845608028654 * (x + 0.044715 * (x * x * x)))))


def _silu(x):
    return x * jax.nn.sigmoid(x)


def _log_sigmoid(x):
    return jnp.minimum(x, 0.0) - jnp.log1p(jnp.exp(-jnp.abs(x)))


def _rms(x, g):
    return x * lax.rsqrt(jnp.mean(x * x, axis=-1, keepdims=True) + EPS) * g


def _ln(x):
    xc = x - jnp.mean(x, axis=-1, keepdims=True)
    return xc * lax.rsqrt(jnp.mean(xc * xc, axis=-1, keepdims=True) + EPS)


def _dot(a, b):
    return jnp.dot(a, b, preferred_element_type=F32)


def _dot_f32(a, b):
    return jnp.dot(a, b, precision=HIGHEST, preferred_element_type=F32)


def _dot_nt(a, b):
    return lax.dot_general(a, b, (((1,), (1,)), ((), ())), preferred_element_type=F32)


def _dot_tn(a, b):
    return lax.dot_general(a, b, (((0,), (0,)), ((), ())), preferred_element_type=F32)


def _adaln_kernel(c_ref, w_ref, b_ref, o_ref):
    s = _silu(c_ref[...])
    o_ref[0] = _dot(s.astype(BF16), w_ref[0].astype(BF16)) + b_ref[0]


def _adaln(cs, ada_w, ada_b):
    depth, d, n = ada_w.shape
    mp = cs.shape[0]
    tn = 1024
    return pl.pallas_call(
        _adaln_kernel,
        grid=(depth, n // tn),
        in_specs=[pl.BlockSpec((mp, d), lambda l, j: (0, 0)),
                  pl.BlockSpec((1, d, tn), lambda l, j: (l, 0, j)),
                  pl.BlockSpec((1, 1, tn), lambda l, j: (l, 0, j))],
        out_specs=pl.BlockSpec((1, mp, tn), lambda l, j: (l, 0, j)),
        out_shape=jax.ShapeDtypeStruct((depth, mp, n), F32),
        compiler_params=_cparams(("arbitrary", "arbitrary"), 32),
        name="adaln",
    )(cs, ada_w, ada_b.reshape(depth, 1, n))


def _proj_kernel(x_ref, g_ref, sh_ref, sc_ref, w_ref, *rest, rows, sub, with_gates):
    if with_gates:
        wg_ref, p_ref, gate_ref, h_ref = rest
    else:
        p_ref, h_ref = rest

    @pl.when(pl.program_id(1) == 0)
    def _():
        for r0 in range(0, rows, sub):
            h = _rms(x_ref[0, r0:r0 + sub, :], g_ref[...]) * (1.0 + sc_ref[0]) + sh_ref[0]
            hb = h.astype(BF16)
            h_ref[r0:r0 + sub, :] = hb
            if with_gates:
                gate_ref[0, r0:r0 + sub, :] = _dot(hb, wg_ref[...])

    for r0 in range(0, rows, sub):
        p_ref[0, r0:r0 + sub, :] = _dot(h_ref[r0:r0 + sub, :], w_ref[...]).astype(BF16)


def _proj(x3, pre_g, shift, scale, w, wg=None):
    nb, rows, d = x3.shape
    n = w.shape[1]
    tn = 1024
    per_block = shift.shape[0] != 1
    mod_map = (lambda i, j: (i, 0, 0)) if per_block else (lambda i, j: (0, 0, 0))
    in_specs = [pl.BlockSpec((1, rows, d), lambda i, j: (i, 0, 0)),
                pl.BlockSpec((1, d), lambda i, j: (0, 0)),
                pl.BlockSpec((1, 1, d), mod_map),
                pl.BlockSpec((1, 1, d), mod_map),
                pl.BlockSpec((d, tn), lambda i, j: (0, j))]
    out_specs = [pl.BlockSpec((1, rows, tn), lambda i, j: (i, 0, j))]
    out_shape = [jax.ShapeDtypeStruct((nb, rows, n), BF16)]
    args = [x3, pre_g.reshape(1, d), shift, scale, w]
    if wg is not None:
        in_specs.append(pl.BlockSpec((d, GATE_LANES), lambda i, j: (0, 0)))
        out_specs.append(pl.BlockSpec((1, rows, GATE_LANES), lambda i, j: (i, 0, 0)))
        out_shape.append(jax.ShapeDtypeStruct((nb, rows, GATE_LANES), F32))
        args.append(wg)
    outs = pl.pallas_call(
        functools.partial(_proj_kernel, rows=rows, sub=min(512, rows), with_gates=wg is not None),
        grid=(nb, n // tn),
        in_specs=in_specs, out_specs=out_specs, out_shape=out_shape,
        scratch_shapes=[pltpu.VMEM((rows, d), BF16)],
        compiler_params=_cparams(("arbitrary", "arbitrary"), 48),
        name="proj",
    )(*args)
    return outs if wg is not None else outs[0]


def _mlstm_kernel(k_ref, v_ref, q_ref, o_ref, gates_ref, wq_ref, wk_ref, bias_ref, hg_ref, c0_ref, m0_ref,
                  mem_ref, cf_ref, mf_ref,
                  pm_ref, cum_ref, brow_ref, kt_ref, qc_ref, rf_ref, rb_ref, st_ref, *, seq):
    T = CHUNK
    hd = B_HEAD_DIM
    nc = seq // T
    row = lax.broadcasted_iota(jnp.int32, (T, T), 0)
    col = lax.broadcasted_iota(jnp.int32, (T, T), 1)
    lower = row >= col
    upper = col >= row
    fwd_lane = lax.broadcasted_iota(jnp.int32, (T, LANES), 1) < B_HEADS
    trow = lax.broadcasted_iota(jnp.int32, (T, LANES), 0)

    def gate_body(c, carry):
        sl = pl.ds(pl.multiple_of(c * T, T), T)
        g = gates_ref[0, sl, :] + bias_ref[...]
        lf = _log_sigmoid(g[:, LANES:])
        cum = jnp.where(fwd_lane, _dot_f32(lower.astype(F32), lf), _dot_f32(upper.astype(F32), lf))
        b = g[:, :LANES] - cum
        pf = b
        ps = b
        for sh in (1, 2, 4, 8, 16, 32, 64):
            pf = jnp.where(trow >= sh, jnp.maximum(pf, pltpu.roll(pf, sh, 0)), pf)
            ps = jnp.where(trow < T - sh, jnp.maximum(ps, pltpu.roll(ps, T - sh, 0)), ps)
        pm_ref[sl, :] = jnp.where(fwd_lane, pf, ps)
        cum_ref[sl, :] = cum
        brow_ref[c] = b.T[:N_CHAINS]
        return carry

    lax.fori_loop(0, nc, gate_body, 0)

    ridx = lax.broadcasted_iota(jnp.int32, (T, hd), 0)

    def conv_body(c, carry):
        start = pl.multiple_of(c * T, T)
        sl = pl.ds(start, T)
        prev_start = pl.multiple_of(jnp.maximum(start - 16, 0), 16)
        next_start = pl.multiple_of(jnp.minimum(start + T, seq - 16), 16)
        has_prev = (c > 0).astype(F32)
        has_next = (c < nc - 1).astype(F32)
        for h in range(B_HEADS):
            hs = slice(h * hd, (h + 1) * hd)
            for src_ref, w_ref, is_k in ((k_ref, wk_ref, True), (q_ref, wq_ref, False)):
                cur = src_ref[0, sl, hs].astype(F32)
                prev_row = src_ref[0, pl.ds(prev_start, 16), hs].astype(F32)[15:16] * has_prev
                next_row = src_ref[0, pl.ds(next_start, 16), hs].astype(F32)[0:1] * has_next
                w = w_ref[:, hs]
                xm = jnp.where(ridx == 0, prev_row, pltpu.roll(cur, 1, 0))
                xp = jnp.where(ridx == T - 1, next_row, pltpu.roll(cur, T - 1, 0))
                y = _silu(xm * w[0:1] + cur * w[1:2] + xp * w[2:3])
                if is_k:
                    kt_ref[h, c] = (y * hd ** -0.5).T.astype(BF16)
                else:
                    qc_ref[sl, hs] = y.astype(BF16)
        return carry

    lax.fori_loop(0, nc, conv_body, 0)

    ones_tile = (lax.broadcasted_iota(jnp.int32, (T, hd), 1) == 0).astype(BF16)
    for j in range(N_CHAINS):
        st_ref[j] = c0_ref[0, j]

    def scan_body(i, ms):
        new_ms = []
        for direction in range(2):
            c = i if direction == 0 else nc - 1 - i
            sl = pl.ds(pl.multiple_of(c * T, T), T)
            pm_c = pm_ref[sl, :]
            cum_c = cum_ref[sl, :]
            edge = T - 1 if direction == 0 else 0
            mask = lower if direction == 0 else upper
            out_ref = rf_ref if direction == 0 else rb_ref
            for h in range(B_HEADS):
                j = direction * B_HEADS + h
                hs = slice(h * hd, (h + 1) * hd)
                m = ms[j]
                pm_col = pm_c[:, j:j + 1]
                f_col = cum_c[:, j:j + 1]
                btot = cum_c[edge:edge + 1, j:j + 1]
                pm_last = pm_c[edge:edge + 1, j:j + 1]
                b_row = brow_ref[c, j:j + 1, :]
                q = qc_ref[sl, hs]
                kt = kt_ref[h, c]
                v_aug = jnp.concatenate([v_ref[0, sl, hs], ones_tile], axis=1)
                m_col = jnp.maximum(m, pm_col)
                p = (_dot(q, kt) * jnp.where(mask, jnp.exp(b_row - m_col), 0.0)).astype(BF16)
                st = st_ref[j]
                acc = _dot(p, v_aug) + jnp.exp(m - m_col) * _dot(q, st.astype(BF16))
                den = acc[:, hd:hd + 1]
                out_ref[sl, hs] = acc[:, :hd] / jnp.maximum(jnp.abs(den), jnp.exp(-(f_col + m_col)))
                m_far = jnp.maximum(m, pm_last)
                m_new = btot + m_far
                kw_t = (kt.astype(F32) * jnp.exp(btot + b_row - m_new)).astype(BF16)
                st_ref[j] = jnp.exp(m - m_far) * st + _dot(kw_t, v_aug)
                new_ms.append(m_new)
        return tuple(new_ms)

    ms = lax.fori_loop(0, nc, scan_body, tuple(m0_ref[0, j][:, 0:1] for j in range(N_CHAINS)))
    for j in range(N_CHAINS):
        cf_ref[0, j] = st_ref[j]
        mf_ref[0, j] = jnp.broadcast_to(ms[j], (1, hd))

    def out_body(c, carry):
        sl = pl.ds(pl.multiple_of(c * T, T), T)
        for h in range(B_HEADS):
            hs = slice(h * hd, (h + 1) * hd)
            y = jax.nn.sigmoid(o_ref[0, sl, hs].astype(F32)) * (rf_ref[sl, hs] + rb_ref[sl, hs])
            mem_ref[0, sl, hs] = (_ln(y) * hg_ref[:, hs]).astype(BF16)
        return carry

    lax.fori_loop(0, nc, out_body, 0)


def _mlstm(p, gates, wq, wk, bias_row, head_g, state):
    b, seq, _ = p.shape
    c0, m0 = state
    hd = B_HEAD_DIM
    nc = seq // CHUNK
    sec = lambda idx: pl.BlockSpec((1, seq, B_DIM), lambda i, idx=idx: (i, 0, idx))
    full = lambda shape: pl.BlockSpec(shape, lambda i: (0,) * len(shape))
    st_c = pl.BlockSpec((1, N_CHAINS, hd, 2 * hd), lambda i: (i, 0, 0, 0))
    st_m = pl.BlockSpec((1, N_CHAINS, 1, hd), lambda i: (i, 0, 0, 0))
    return pl.pallas_call(
        functools.partial(_mlstm_kernel, seq=seq),
        grid=(b,),
        in_specs=[sec(0), sec(1), sec(2), sec(3),
                  pl.BlockSpec((1, seq, GATE_LANES), lambda i: (i, 0, 0)),
                  full((3, B_DIM)), full((3, B_DIM)), full((1, GATE_LANES)), full((1, B_DIM)),
                  st_c, st_m],
        out_specs=[pl.BlockSpec((1, seq, B_DIM), lambda i: (i, 0, 0)), st_c, st_m],
        out_shape=[jax.ShapeDtypeStruct((b, seq, B_DIM), BF16),
                   jax.ShapeDtypeStruct(c0.shape, F32),
                   jax.ShapeDtypeStruct(m0.shape, F32)],
        scratch_shapes=[pltpu.VMEM((seq, LANES), F32), pltpu.VMEM((seq, LANES), F32),
                        pltpu.VMEM((nc, N_CHAINS, CHUNK), F32),
                        pltpu.VMEM((B_HEADS, nc, hd, CHUNK), BF16), pltpu.VMEM((seq, B_DIM), BF16),
                        pltpu.VMEM((seq, B_DIM), F32), pltpu.VMEM((seq, B_DIM), F32),
                        pltpu.VMEM((N_CHAINS, hd, 2 * hd), F32)],
        compiler_params=_cparams(("arbitrary",), 56),
        name="mlstm",
    )(p, p, p, p, gates, wq, wk, bias_row, head_g, c0, m0)


def _merge0_kernel(u_ref, va_ref, mem_ref, x_ref, sw_ref, sb_ref, wo_ref, pg_ref, gate_ref, out_ref, cat_ref, *, tm):
    T = CHUNK
    gd = A_DIM // A_GROUPS
    for c in range(tm // T):
        sl = slice(c * T, (c + 1) * T)
        vn = _ln(_gelu(va_ref[0, sl, :].astype(F32))).astype(BF16)
        u = _gelu(u_ref[0, sl, :].astype(F32))
        for g in range(A_GROUPS):
            gs = slice(g * gd, (g + 1) * gd)
            mixed = _dot(sw_ref[g], vn[:, gs]) + sb_ref[g]
            cat_ref[sl, gs] = (u[:, gs] * mixed).astype(BF16)
    cat_ref[:, A_DIM:] = mem_ref[0]
    y = _dot(cat_ref[...], wo_ref[...])
    out_ref[0] = x_ref[0] + gate_ref[0] * _rms(y, pg_ref[...])


def _merge0(p, mem, x, sgu_w, sgu_bb, w_out, post_g, gate, tm):
    nb = x.shape[0]
    d = D_MODEL
    per_block = gate.shape[0] != 1
    blocks_per_gate = nb // gate.shape[0]
    gate_map = (lambda i: (i // blocks_per_gate, 0, 0)) if per_block else (lambda i: (0, 0, 0))
    return pl.pallas_call(
        functools.partial(_merge0_kernel, tm=tm),
        grid=(nb,),
        in_specs=[pl.BlockSpec((1, tm, A_DIM), lambda i: (i, 0, 4)),
                  pl.BlockSpec((1, tm, A_DIM), lambda i: (i, 0, 5)),
                  pl.BlockSpec((1, tm, B_DIM), lambda i: (i, 0, 0)),
                  pl.BlockSpec((1, tm, d), lambda i: (i, 0, 0)),
                  pl.BlockSpec((A_GROUPS, CHUNK, CHUNK), lambda i: (0, 0, 0)),
                  pl.BlockSpec((A_GROUPS, CHUNK, CHUNK), lambda i: (0, 0, 0)),
                  pl.BlockSpec((A_DIM + B_DIM, d), lambda i: (0, 0)),
                  pl.BlockSpec((1, d), lambda i: (0, 0)),
                  pl.BlockSpec((1, 1, d), gate_map)],
        out_specs=pl.BlockSpec((1, tm, d), lambda i: (i, 0, 0)),
        out_shape=jax.ShapeDtypeStruct((nb, tm, d), F32),
        scratch_shapes=[pltpu.VMEM((tm, A_DIM + B_DIM), BF16)],
        compiler_params=_cparams(("arbitrary",), 32),
        name="merge0",
    )(p, p, mem, x, sgu_w, sgu_bb, w_out, post_g.reshape(1, d), gate)


FFN_TK = 256
FFN_HALO = 64
FFN_PAD = 8


def _ffn_kernel(*refs, rows, tile_rows, width, grid2d, n_tiles):
    if grid2d:
        (x_ref, xu_ref, xd_ref, g_ref, sh_ref, sc_ref, wup_ref, cw_ref, wd_ref, pg_ref, gate_ref,
         o_ref, h_ref, gs0_ref, gs1_ref, act_ref) = refs
    else:
        (x_ref, g_ref, sh_ref, sc_ref, wup_ref, cw_ref, wd_ref, pg_ref, gate_ref,
         o_ref, h_ref, gs0_ref, gs1_ref, act_ref) = refs
    halo = FFN_HALO if grid2d else 0
    tk = FFN_TK
    ks = D_FF // tk
    hrows = rows + 2 * halo
    sub = 512

    def normmod(xv):
        return _rms(xv, g_ref[...]) * (1.0 + sc_ref[0]) + sh_ref[0]

    for r0 in range(0, rows, sub):
        h_ref[halo + r0:halo + r0 + sub, :] = normmod(x_ref[0, r0:r0 + sub, :]).astype(BF16)
    if grid2d:
        t = pl.program_id(0) % n_tiles
        h_ref[0:halo, :] = (normmod(xu_ref[0]) * (t > 0).astype(F32)).astype(BF16)
        h_ref[halo + rows:hrows, :] = (normmod(xd_ref[0]) * (t < n_tiles - 1).astype(F32)).astype(BF16)

    for gs_ref in (gs0_ref, gs1_ref):
        gs_ref[0:FFN_PAD, :] = jnp.zeros((FFN_PAD, tk), F32)
        gs_ref[FFN_PAD + hrows:2 * FFN_PAD + hrows, :] = jnp.zeros((FFN_PAD, tk), F32)

    def gate_branch(slab, gs_ref):
        for r0 in range(0, hrows, sub):
            r1 = min(r0 + sub, hrows)
            gs_ref[FFN_PAD + r0:FFN_PAD + r1, :] = _dot(h_ref[r0:r1, :], wup_ref[ks + slab])

    iota8 = lax.broadcasted_iota(jnp.int32, (8, tk), 0)

    def cut(vals, first):
        pieces = []
        for g0 in range(0, tile_rows, width):
            if first:
                pieces += [jnp.where(iota8 == 0, 0.0, vals[g0:g0 + 8]), vals[g0 + 8:g0 + width]]
            else:
                pieces += [vals[g0:g0 + width - 8], jnp.where(iota8 == 7, 0.0, vals[g0 + width - 8:g0 + width])]
        return jnp.concatenate(pieces, axis=0)

    def tiles(slab, gs_ref):
        cw = cw_ref[slab]
        for base in range(0, rows, tile_rows):
            def tap(dy, dx):
                start = FFN_PAD + halo + base + (dy - 1) * halo + (dx - 1)
                return cw[3 * dy + dx:3 * dy + dx + 1] * gs_ref[start:start + tile_rows, :]

            dys = (0, 1, 2) if grid2d else (1,)
            left, mid, right = (functools.reduce(lambda s, t: s + t, [tap(dy, dx) for dy in dys]) for dx in range(3))
            conv = mid + cut(left, True) + cut(right, False)
            a = _dot(h_ref[halo + base:halo + base + tile_rows, :], wup_ref[slab])
            t = jnp.tanh(conv * (0.7978845608028654 + (0.7978845608028654 * 0.044715) * (conv * conv)))
            p = a * conv
            act_ref[slab, base:base + tile_rows, :] = (p + p * t).astype(BF16)

    assert ks % 2 == 1
    gate_branch(0, gs0_ref)

    def body(j, carry):
        k0 = 2 * j
        gate_branch(k0 + 1, gs1_ref)
        tiles(k0, gs0_ref)
        gate_branch(k0 + 2, gs0_ref)
        tiles(k0 + 1, gs1_ref)
        return carry

    lax.fori_loop(0, ks // 2, body, 0)
    tiles(ks - 1, gs0_ref)

    out_rows = 256
    for r0 in range(0, rows, out_rows):
        f = _dot(act_ref[0, r0:r0 + out_rows, :], wd_ref[0])
        for k in range(1, ks):
            f = f + _dot(act_ref[k, r0:r0 + out_rows, :], wd_ref[k])
        o_ref[0, r0:r0 + out_rows, :] = x_ref[0, r0:r0 + out_rows, :] + gate_ref[0] * _rms(f, pg_ref[...])


def _ffn(x3, pre_g, shift, scale, w_up3, conv3, w_down3, post_g, gate, *, rows, width, grid2d):
    nb, seq, d = x3.shape
    tk = FFN_TK
    ks = D_FF // tk
    n_tiles = seq // rows
    tile_rows = max(width, 128)
    halo = FFN_HALO if grid2d else 0
    assert seq % rows == 0 and rows % 512 == 0 and tile_rows % width == 0 and (not grid2d or width == FFN_HALO)
    per_block = shift.shape[0] != 1
    mod_map = (lambda i: (i // n_tiles, 0, 0)) if per_block else (lambda i: (0, 0, 0))
    vec = lambda: pl.BlockSpec((1, d), lambda i: (0, 0))
    mod = lambda: pl.BlockSpec((1, 1, d), mod_map)
    resident = lambda shape: pl.BlockSpec(shape, lambda i: (0,) * len(shape), pipeline_mode=pl.Buffered(1))
    in_specs = [pl.BlockSpec((1, rows, d), lambda i: (i // n_tiles, i % n_tiles, 0))]
    args = [x3]
    if grid2d:
        per = rows // halo
        last = seq // halo - 1
        in_specs += [pl.BlockSpec((1, halo, d), lambda i: (i // n_tiles, jnp.maximum((i % n_tiles) * per - 1, 0), 0)),
                     pl.BlockSpec((1, halo, d), lambda i: (i // n_tiles, jnp.minimum((i % n_tiles + 1) * per, last), 0))]
        args += [x3, x3]
    in_specs += [vec(), mod(), mod(),
                 resident((2 * ks, d, tk)), resident((ks, 9, tk)), resident((ks, tk, d)),
                 vec(), mod()]
    args += [pre_g.reshape(1, d), shift, scale, w_up3, conv3, w_down3, post_g.reshape(1, d), gate]
    return pl.pallas_call(
        functools.partial(_ffn_kernel, rows=rows, tile_rows=tile_rows, width=width, grid2d=grid2d, n_tiles=n_tiles),
        grid=(nb * n_tiles,),
        in_specs=in_specs,
        out_specs=pl.BlockSpec((1, rows, d), lambda i: (i // n_tiles, i % n_tiles, 0)),
        out_shape=jax.ShapeDtypeStruct((nb, seq, d), F32),
        scratch_shapes=[pltpu.VMEM((rows + 2 * halo, d), BF16),
                        pltpu.VMEM((rows + 2 * halo + 2 * FFN_PAD, tk), F32),
                        pltpu.VMEM((rows + 2 * halo + 2 * FFN_PAD, tk), F32),
                        pltpu.VMEM((ks, rows, tk), BF16)],
        compiler_params=_cparams(("arbitrary",), 56),
        name="ffn",
    )(*args)


RET_CHUNK = 256


def _ret_kernel(k_ref, v_ref, q_ref, kc_ref, vc_ref, dec_ref, r_ref, rf_ref, rb_ref, *, seq, ctx_len):
    T = RET_CHUNK
    nc = seq // T
    t_col = lax.broadcasted_iota(jnp.int32, (T, 1), 0).astype(F32)
    lg = [_log_sigmoid(dec_ref[0, direction][0:1, 0:1]) for direction in range(2)]
    scale = C_QK_DIM ** -0.5
    zeta = (jnp.exp(lg[0] * (T - 1 - t_col)) * scale, jnp.exp(lg[1] * t_col) * scale)
    xi = (jnp.exp(lg[0] * (t_col + 1.0)), jnp.exp(lg[1] * (T - t_col)))
    gch = (jnp.exp(lg[0] * T), jnp.exp(lg[1] * T))
    row = lax.broadcasted_iota(jnp.int32, (T, T), 0)
    col = lax.broadcasted_iota(jnp.int32, (T, T), 1)
    rel = (row - col).astype(F32)
    d_intra = (jnp.where(row >= col, jnp.exp(lg[0] * rel), 0.0)
               + jnp.where(col >= row, jnp.exp(-lg[1] * rel), 0.0)) * scale

    def absorb(state_ref, direction, k, v, first):
        upd = _dot_tn((k.astype(F32) * zeta[direction]).astype(BF16), v)
        state_ref[...] = upd if first else gch[direction] * state_ref[...] + upd

    ncc = ctx_len // T
    for direction, state_ref in ((0, rf_ref), (1, rb_ref)):
        order = list(range(ncc)) if direction == 0 else list(range(ncc - 1, -1, -1))
        for n, c in enumerate(order):
            absorb(state_ref, direction, kc_ref[0, c * T:(c + 1) * T, :], vc_ref[0, c * T:(c + 1) * T, :], n == 0)

    def scan(i, first_visit):
        slf = pl.ds(pl.multiple_of(i * T, T), T)
        slb = pl.ds(pl.multiple_of((nc - 1 - i) * T, T), T)
        q, k, v = q_ref[0, slf, :], k_ref[0, slf, :], v_ref[0, slf, :]
        scores = _dot_nt(q, k) * d_intra
        fwd = _dot(scores.astype(BF16), v) + xi[0] * _dot(q, rf_ref[...].astype(BF16))
        absorb(rf_ref, 0, k, v, False)
        qb, kb, vb = q_ref[0, slb, :], k_ref[0, slb, :], v_ref[0, slb, :]
        bwd = xi[1] * _dot(qb, rb_ref[...].astype(BF16))
        absorb(rb_ref, 1, kb, vb, False)
        if first_visit:
            r_ref[0, slf, :] = fwd
            r_ref[0, slb, :] = bwd
        else:
            r_ref[0, slf, :] += fwd
            r_ref[0, slb, :] += bwd

    assert nc % 2 == 0

    def first_half(i, carry):
        scan(i, True)
        return carry

    def second_half(i, carry):
        scan(i, False)
        return carry

    lax.fori_loop(0, nc // 2, first_half, 0)
    lax.fori_loop(nc // 2, nc, second_half, 0)


def _ret(p, p_ctx, dec_b):
    b, seq, _ = p.shape
    ctx_len = p_ctx.shape[1]
    assert seq % RET_CHUNK == 0 and ctx_len % RET_CHUNK == 0
    return pl.pallas_call(
        functools.partial(_ret_kernel, seq=seq, ctx_len=ctx_len),
        grid=(b, C_HEADS),
        in_specs=[pl.BlockSpec((1, seq, C_QK_DIM), lambda i, h: (i, 0, h)),
                  pl.BlockSpec((1, seq, C_V_DIM), lambda i, h: (i, 0, 2 + h)),
                  pl.BlockSpec((1, seq, C_QK_DIM), lambda i, h: (i, 0, 12 + h)),
                  pl.BlockSpec((1, ctx_len, C_QK_DIM), lambda i, h: (i, 0, h)),
                  pl.BlockSpec((1, ctx_len, C_V_DIM), lambda i, h: (i, 0, 2 + h)),
                  pl.BlockSpec((1, 2, 8, LANES), lambda i, h: (h, 0, 0, 0))],
        out_specs=pl.BlockSpec((1, seq, C_V_DIM), lambda i, h: (i, 0, h)),
        out_shape=jax.ShapeDtypeStruct((b, seq, C_HEADS * C_V_DIM), F32),
        scratch_shapes=[pltpu.VMEM((C_QK_DIM, C_V_DIM), F32), pltpu.VMEM((C_QK_DIM, C_V_DIM), F32)],
        compiler_params=_cparams(("arbitrary", "arbitrary"), 48),
        name="ret",
    )(p, p, p, p_ctx, p_ctx, dec_b)


def _merge1_kernel(r_ref, g_ref, x_ref, hg_ref, wo_ref, pg_ref, gate_ref, out_ref, z_ref):
    for h in range(C_HEADS):
        hs = slice(h * C_V_DIM, (h + 1) * C_V_DIM)
        z = _silu(g_ref[0, :, hs].astype(F32)) * (_ln(r_ref[0, :, hs]) * hg_ref[:, hs])
        z_ref[:, hs] = z.astype(BF16)
    y = _dot(z_ref[...], wo_ref[...])
    out_ref[0] = x_ref[0] + gate_ref[0] * _rms(y, pg_ref[...])


def _merge1(r, p, x, head_g, w_out, post_g, gate, tm):
    nb = x.shape[0]
    d = D_MODEL
    hv = C_HEADS * C_V_DIM
    blocks_per_gate = nb // gate.shape[0]
    return pl.pallas_call(
        _merge1_kernel,
        grid=(nb,),
        in_specs=[pl.BlockSpec((1, tm, hv), lambda i: (i, 0, 0)),
                  pl.BlockSpec((1, tm, hv), lambda i: (i, 0, 2)),
                  pl.BlockSpec((1, tm, d), lambda i: (i, 0, 0)),
                  pl.BlockSpec((1, hv), lambda i: (0, 0)),
                  pl.BlockSpec((hv, d), lambda i: (0, 0)),
                  pl.BlockSpec((1, d), lambda i: (0, 0)),
                  pl.BlockSpec((1, 1, d), lambda i: (i // blocks_per_gate, 0, 0))],
        out_specs=pl.BlockSpec((1, tm, d), lambda i: (i, 0, 0)),
        out_shape=jax.ShapeDtypeStruct((nb, tm, d), F32),
        scratch_shapes=[pltpu.VMEM((tm, hv), BF16)],
        compiler_params=_cparams(("arbitrary",), 48),
        name="merge1",
    )(r, p, x, head_g, w_out, post_g.reshape(1, d), gate)


def kernel(x, c, ctx, c_ctx, ada_w, ada_b, pre_g, post_g, ffn_up, ffn_conv, ffn_down,
           ab_w_in, ab_qk_conv, ab_gate_b, ab_sgu_w, ab_sgu_b, ab_head_g, ab_w_out,
           ret_w_in, ret_decay, ret_head_g, ret_w_out):
    b, seq, d = x.shape
    ctx_len = ctx.shape[1]
    depth = ada_w.shape[0]
    tm = 512
    ctx_rows = min(2048, b * ctx_len)

    n_cond = b + 1
    n_pad = -n_cond % 8
    cs = jnp.concatenate([c, c_ctx[None, :], jnp.zeros((n_pad, d), F32)], axis=0)
    mods = _adaln(cs, ada_w, ada_b)

    def mod_l(layer, i):
        return mods[layer, :b, i * d:(i + 1) * d].reshape(b, 1, d)

    def mod_c(layer, i):
        return mods[layer, b:b + 1, i * d:(i + 1) * d].reshape(1, 1, d)

    n_slab = D_FF // FFN_TK
    ffn_up3 = ffn_up.astype(BF16).reshape(depth, d, 2 * n_slab, FFN_TK).transpose(0, 2, 1, 3)
    ffn_conv3 = ffn_conv.reshape(depth, 9, n_slab, FFN_TK).transpose(0, 2, 1, 3)
    ffn_down3 = (0.5 * ffn_down).astype(BF16).reshape(depth, n_slab, FFN_TK, d)

    def ffn_l(layer, xl):
        return _ffn(xl, pre_g[layer, 1], mod_l(layer, 3), mod_l(layer, 4), ffn_up3[layer], ffn_conv3[layer],
                    ffn_down3[layer], post_g[layer, 1], mod_l(layer, 5), rows=1024, width=GRID_W, grid2d=True)

    def ffn_c(layer, xc):
        out = _ffn(xc.reshape(1, b * ctx_len, d), pre_g[layer, 1], mod_c(layer, 3), mod_c(layer, 4), ffn_up3[layer],
                   ffn_conv3[layer], ffn_down3[layer], post_g[layer, 1], mod_c(layer, 5),
                   rows=min(1024, b * ctx_len), width=ctx_len, grid2d=False)
        return out.reshape(xc.shape)

    w_in = ab_w_in[0]
    g_lo = 2 * B_DIM
    g_hi = g_lo + N_GATES * B_HEADS
    w_main = jnp.concatenate([w_in[:, :g_lo], w_in[:, g_hi:]], axis=1).astype(BF16)
    w_gates = w_in[:, g_lo:g_hi].reshape(d, N_GATES, B_HEADS)
    lane_pad = ((0, 0), (0, LANES - N_CHAINS))
    w_gate = jnp.concatenate([jnp.pad(w_gates[:, 0::2].reshape(d, N_CHAINS), lane_pad),
                              jnp.pad(w_gates[:, 1::2].reshape(d, N_CHAINS), lane_pad)], axis=1).astype(BF16)
    bias_row = jnp.concatenate([jnp.pad(ab_gate_b[0][0::2].reshape(1, N_CHAINS), lane_pad),
                                jnp.pad(ab_gate_b[0][1::2].reshape(1, N_CHAINS), lane_pad)], axis=1)
    wq = ab_qk_conv[0][:, :B_DIM]
    wk = ab_qk_conv[0][:, B_DIM:]
    head_g0 = ab_head_g[0].reshape(1, B_DIM)
    sgu_w = ab_sgu_w[0].astype(BF16)
    sgu_bb = jnp.broadcast_to(ab_sgu_b[0][:, :, None], (A_GROUPS, CHUNK, CHUNK))
    w_out0 = ab_w_out[0].astype(BF16)

    p_c, gates_c = _proj(ctx.reshape(-1, ctx_rows, d), pre_g[0, 0], mod_c(0, 0), mod_c(0, 1), w_main, w_gate)
    p_l, gates_l = _proj(x, pre_g[0, 0], mod_l(0, 0), mod_l(0, 1), w_main, w_gate)
    p_c = p_c.reshape(b, ctx_len, -1)
    gates_c = gates_c.reshape(b, ctx_len, GATE_LANES)
    zero_state = (jnp.zeros((b, N_CHAINS, B_HEAD_DIM, 2 * B_HEAD_DIM), F32),
                  jnp.zeros((b, N_CHAINS, 1, B_HEAD_DIM), F32))
    mem_c, *ctx_state = _mlstm(p_c, gates_c, wq, wk, bias_row, head_g0, zero_state)
    mem_l, *_ = _mlstm(p_l, gates_l, wq, wk, bias_row, head_g0, tuple(ctx_state))

    rt = lambda t: t.reshape(-1, tm, t.shape[-1])
    x = _merge0(rt(p_l), rt(mem_l), rt(x), sgu_w, sgu_bb, w_out0, post_g[0, 0], mod_l(0, 2), tm).reshape(b, seq, d)
    ctx = _merge0(rt(p_c), rt(mem_c), rt(ctx), sgu_w, sgu_bb, w_out0, post_g[0, 0], mod_c(0, 2), tm).reshape(b, ctx_len, d)
    x = ffn_l(0, x)
    ctx = ffn_c(0, ctx)

    w_ret = ret_w_in[0].astype(BF16)
    kv_cols = C_HEADS * (C_QK_DIM + C_V_DIM)
    dec_b = jnp.broadcast_to(ret_decay[0].T[:, :, None, None], (C_HEADS, 2, 8, LANES))
    p_c = _proj(ctx.reshape(-1, ctx_rows, d), pre_g[1, 0], mod_c(1, 0), mod_c(1, 1), w_ret[:, :kv_cols])
    p_l = _proj(x, pre_g[1, 0], mod_l(1, 0), mod_l(1, 1), w_ret)
    r = _ret(p_l, p_c.reshape(b, ctx_len, -1), dec_b)
    x = _merge1(rt(r), rt(p_l), rt(x), ret_head_g[0].reshape(1, -1), ret_w_out[0].astype(BF16),
                post_g[1, 0], mod_l(1, 2), tm).reshape(b, seq, d)
    x = ffn_l(1, x)
    return x
```

```python
import functools

import jax
import jax.numpy as jnp
from jax import lax
from jax.experimental import pallas as pl
from jax.experimental.pallas import tpu as pltpu

F32 = jnp.float32
BF16 = jnp.bfloat16
HIGHEST = lax.Precision.HIGHEST

D_MODEL = 1024
GRID_W = 64
CHUNK = 128
EPS = 1e-6
N_MOD = 6
A_GROUPS = 4
A_DIM = 512
B_HEADS = 4
B_HEAD_DIM = 128
B_DIM = 512
N_GATES = 4
C_HEADS = 4
C_QK_DIM = 256
C_V_DIM = 512
D_FF = 2816

LANES = 128
GATE_LANES = 2 * LANES
N_CHAINS = 2 * B_HEADS
V7X_VMEM_BYTES = 64 * 1024 * 1024
MIB = 1024 * 1024


def _cparams(semantics, vmem_mib):
    assert vmem_mib * MIB < V7X_VMEM_BYTES
    return pltpu.CompilerParams(dimension_semantics=semantics, vmem_limit_bytes=vmem_mib * MIB)


def _gelu(x):
    return x * (0.5 * (1.0 + jnp.tanh(0.7978845608028654 * (x + 0.044715 * (x * x * x)))))


def _silu(x):
    return x * jax.nn.sigmoid(x)


def _log_sigmoid(x):
    return jnp.minimum(x, 0.0) - jnp.log1p(jnp.exp(-jnp.abs(x)))


def _rms(x, g):
    return x * lax.rsqrt(jnp.mean(x * x, axis=-1, keepdims=True) + EPS) * g


def _ln(x):
    xc = x - jnp.mean(x, axis=-1, keepdims=True)
    return xc * lax.rsqrt(jnp.mean(xc * xc, axis=-1, keepdims=True) + EPS)


def _dot(a, b):
    return jnp.dot(a, b, preferred_element_type=F32)


def _dot_f32(a, b):
    return jnp.dot(a, b, precision=HIGHEST, preferred_element_type=F32)


def _dot_nt(a, b):
    return lax.dot_general(a, b, (((1,), (1,)), ((), ())), preferred_element_type=F32)


def _dot_tn(a, b):
    return lax.dot_general(a, b, (((0,), (0,)), ((), ())), preferred_element_type=F32)


def _adaln_kernel(c_ref, w_ref, b_ref, o_ref):
    s = _silu(c_ref[...])
    o_ref[0] = _dot(s.astype(BF16), w_ref[0].astype(BF16)) + b_ref[0]


def _adaln(cs, ada_w, ada_b):
    depth, d, n = ada_w.shape
    mp = cs.shape[0]
    tn = 1024
    return pl.pallas_call(
        _adaln_kernel,
        grid=(depth, n // tn),
        in_specs=[pl.BlockSpec((mp, d), lambda l, j: (0, 0)),
                  pl.BlockSpec((1, d, tn), lambda l, j: (l, 0, j)),
                  pl.BlockSpec((1, 1, tn), lambda l, j: (l, 0, j))],
        out_specs=pl.BlockSpec((1, mp, tn), lambda l, j: (l, 0, j)),
        out_shape=jax.ShapeDtypeStruct((depth, mp, n), F32),
        compiler_params=_cparams(("arbitrary", "arbitrary"), 32),
        name="adaln",
    )(cs, ada_w, ada_b.reshape(depth, 1, n))


def _proj_kernel(x_ref, g_ref, sh_ref, sc_ref, w_ref, *rest, rows, sub, with_gates):
    if with_gates:
        wg_ref, p_ref, gate_ref, h_ref = rest
    else:
        p_ref, h_ref = rest

    @pl.when(pl.program_id(1) == 0)
    def _():
        for r0 in range(0, rows, sub):
            h = _rms(x_ref[0, r0:r0 + sub, :], g_ref[...]) * (1.0 + sc_ref[0]) + sh_ref[0]
            hb = h.astype(BF16)
            h_ref[r0:r0 + sub, :] = hb
            if with_gates:
                gate_ref[0, r0:r0 + sub, :] = _dot(hb, wg_ref[...])
            p_ref[0, r0:r0 + sub, :] = _dot(hb, w_ref[...]).astype(BF16)

    @pl.when(pl.program_id(1) != 0)
    def _():
        for r0 in range(0, rows, sub):
            p_ref[0, r0:r0 + sub, :] = _dot(h_ref[r0:r0 + sub, :], w_ref[...]).astype(BF16)


def _proj(x3, pre_g, shift, scale, w, wg=None):
    nb, rows, d = x3.shape
    n = w.shape[1]
    tn = 1024
    per_block = shift.shape[0] != 1
    mod_map = (lambda i, j: (i, 0, 0)) if per_block else (lambda i, j: (0, 0, 0))
    in_specs = [pl.BlockSpec((1, rows, d), lambda i, j: (i, 0, 0)),
                pl.BlockSpec((1, d), lambda i, j: (0, 0)),
                pl.BlockSpec((1, 1, d), mod_map),
                pl.BlockSpec((1, 1, d), mod_map),
                pl.BlockSpec((d, tn), lambda i, j: (0, j))]
    out_specs = [pl.BlockSpec((1, rows, tn), lambda i, j: (i, 0, j))]
    out_shape = [jax.ShapeDtypeStruct((nb, rows, n), BF16)]
    args = [x3, pre_g.reshape(1, d), shift, scale, w]
    if wg is not None:
        in_specs.append(pl.BlockSpec((d, GATE_LANES), lambda i, j: (0, 0)))
        out_specs.append(pl.BlockSpec((1, rows, GATE_LANES), lambda i, j: (i, 0, 0)))
        out_shape.append(jax.ShapeDtypeStruct((nb, rows, GATE_LANES), F32))
        args.append(wg)
    outs = pl.pallas_call(
        functools.partial(_proj_kernel, rows=rows, sub=min(512, rows), with_gates=wg is not None),
        grid=(nb, n // tn),
        in_specs=in_specs, out_specs=out_specs, out_shape=out_shape,
        scratch_shapes=[pltpu.VMEM((rows, d), BF16)],
        compiler_params=_cparams(("arbitrary", "arbitrary"), 48),
        name="proj",
    )(*args)
    return outs if wg is not None else outs[0]


def _mlstm_kernel(k_ref, v_ref, q_ref, o_ref, gates_ref, wq_ref, wk_ref, bias_ref, hg_ref, c0_ref, m0_ref,
                  mem_ref, cf_ref, mf_ref,
                  pm_ref, cum_ref, brow_ref, kt_ref, qc_ref, rf_ref, rb_ref, st_ref, *, seq):
    T = CHUNK
    hd = B_HEAD_DIM
    nc = seq // T
    row = lax.broadcasted_iota(jnp.int32, (T, T), 0)
    col = lax.broadcasted_iota(jnp.int32, (T, T), 1)
    lower = row >= col
    upper = col >= row
    fwd_lane = lax.broadcasted_iota(jnp.int32, (T, LANES), 1) < B_HEADS
    trow = lax.broadcasted_iota(jnp.int32, (T, LANES), 0)

    def gate_body(c, carry):
        sl = pl.ds(pl.multiple_of(c * T, T), T)
        g = gates_ref[0, sl, :] + bias_ref[...]
        lf = _log_sigmoid(g[:, LANES:])
        cum = jnp.where(fwd_lane, _dot_f32(lower.astype(F32), lf), _dot_f32(upper.astype(F32), lf))
        b = g[:, :LANES] - cum
        pf = b
        ps = b
        for sh in (1, 2, 4, 8, 16, 32, 64):
            pf = jnp.where(trow >= sh, jnp.maximum(pf, pltpu.roll(pf, sh, 0)), pf)
            ps = jnp.where(trow < T - sh, jnp.maximum(ps, pltpu.roll(ps, T - sh, 0)), ps)
        pm_ref[sl, :] = jnp.where(fwd_lane, pf, ps)
        cum_ref[sl, :] = cum
        brow_ref[c] = b.T[:N_CHAINS]
        return carry

    lax.fori_loop(0, nc, gate_body, 0)

    ridx = lax.broadcasted_iota(jnp.int32, (T, hd), 0)

    def conv_body(c, carry):
        start = pl.multiple_of(c * T, T)
        sl = pl.ds(start, T)
        prev_start = pl.multiple_of(jnp.maximum(start - 16, 0), 16)
        next_start = pl.multiple_of(jnp.minimum(start + T, seq - 16), 16)
        has_prev = jnp.where(c > 0, 1.0, 0.0)
        has_next = jnp.where(c < nc - 1, 1.0, 0.0)
        for h in range(B_HEADS):
            hs = slice(h * hd, (h + 1) * hd)
            for src_ref, w_ref, is_k in ((k_ref, wk_ref, True), (q_ref, wq_ref, False)):
                cur = src_ref[0, sl, hs].astype(F32)
                prev_row = src_ref[0, pl.ds(prev_start, 16), hs].astype(F32)[15:16] * has_prev
                next_row = src_ref[0, pl.ds(next_start, 16), hs].astype(F32)[0:1] * has_next
                w = w_ref[:, hs]
                xm = jnp.where(ridx == 0, prev_row, pltpu.roll(cur, 1, 0))
                xp = jnp.where(ridx == T - 1, next_row, pltpu.roll(cur, T - 1, 0))
                y = _silu(xm * w[0:1] + cur * w[1:2] + xp * w[2:3])
                if is_k:
                    kt_ref[h, c] = (y * hd ** -0.5).T.astype(BF16)
                else:
                    qc_ref[sl, hs] = y.astype(BF16)
        return carry

    lax.fori_loop(0, nc, conv_body, 0)

    ones_tile = (lax.broadcasted_iota(jnp.int32, (T, hd), 1) == 0).astype(BF16)
    for j in range(N_CHAINS):
        st_ref[j] = c0_ref[0, j]

    def scan_body(i, ms):
        chains = range(N_CHAINS)
        heads = [j % B_HEADS for j in chains]
        cs = [i if j < B_HEADS else nc - 1 - i for j in chains]
        sls = [pl.ds(pl.multiple_of(cs[j] * T, T), T) for j in chains]
        hss = [slice(h * hd, (h + 1) * hd) for h in heads]
        edge = [cs[j] * T + (T - 1 if j < B_HEADS else 0) for j in chains]
        b_row = [brow_ref[cs[j], j:j + 1, :] for j in chains]
        m_col = [jnp.maximum(ms[j], pm_ref[sls[j], j:j + 1]) for j in chains]
        m_wide = [jnp.broadcast_to(m_col[j], (T, T)) for j in chains]
        q = [qc_ref[sls[j], hss[j]] for j in chains]
        kt = [kt_ref[heads[j], cs[j]] for j in chains]
        dmat = [jnp.where(lower if j < B_HEADS else upper, jnp.exp(b_row[j] - m_wide[j]), 0.0) for j in chains]
        p = [(_dot(q[j], kt[j]) * dmat[j]).astype(BF16) for j in chains]
        v_aug = [jnp.concatenate([v_ref[0, sls[j], hss[j]], ones_tile], axis=1) for j in chains]
        st = [st_ref[j] for j in chains]
        w_prev = [jnp.exp(ms[j] - m_wide[j]) for j in chains]
        acc = [_dot(p[j], v_aug[j]) + jnp.concatenate([w_prev[j], w_prev[j]], axis=1) * _dot(q[j], st[j].astype(BF16))
               for j in chains]
        for j in chains:
            out_ref = rf_ref if j < B_HEADS else rb_ref
            den = acc[j][:, hd:hd + 1]
            floor = jnp.exp(-(cum_ref[sls[j], j:j + 1] + m_col[j]))
            out_ref[sls[j], hss[j]] = acc[j][:, :hd] / jnp.maximum(jnp.abs(den), floor)
        new_ms = []
        for j in chains:
            btot = cum_ref[pl.ds(edge[j], 1), j:j + 1]
            m_far = jnp.maximum(ms[j], pm_ref[pl.ds(edge[j], 1), j:j + 1])
            m_new = btot + m_far
            kw_t = (kt[j].astype(F32) * jnp.exp(btot + b_row[j] - m_new)).astype(BF16)
            st_ref[j] = jnp.exp(ms[j] - m_far) * st[j] + _dot(kw_t, v_aug[j])
            new_ms.append(m_new)
        return tuple(new_ms)

    ms = lax.fori_loop(0, nc, scan_body, tuple(m0_ref[0, j][:, 0:1] for j in range(N_CHAINS)))
    for j in range(N_CHAINS):
        cf_ref[0, j] = st_ref[j]
        mf_ref[0, j] = jnp.broadcast_to(ms[j], (1, hd))

    def out_body(c, carry):
        sl = pl.ds(pl.multiple_of(c * T, T), T)
        for h in range(B_HEADS):
            hs = slice(h * hd, (h + 1) * hd)
            y = jax.nn.sigmoid(o_ref[0, sl, hs].astype(F32)) * (rf_ref[sl, hs] + rb_ref[sl, hs])
            mem_ref[0, sl, hs] = (_ln(y) * hg_ref[:, hs]).astype(BF16)
        return carry

    lax.fori_loop(0, nc, out_body, 0)


def _mlstm(p, gates, wq, wk, bias_row, head_g, state):
    b, seq, _ = p.shape
    c0, m0 = state
    hd = B_HEAD_DIM
    nc = seq // CHUNK
    sec = lambda idx: pl.BlockSpec((1, seq, B_DIM), lambda i, idx=idx: (i, 0, idx))
    full = lambda shape: pl.BlockSpec(shape, lambda i: (0,) * len(shape))
    st_c = pl.BlockSpec((1, N_CHAINS, hd, 2 * hd), lambda i: (i, 0, 0, 0))
    st_m = pl.BlockSpec((1, N_CHAINS, 1, hd), lambda i: (i, 0, 0, 0))
    return pl.pallas_call(
        functools.partial(_mlstm_kernel, seq=seq),
        grid=(b,),
        in_specs=[sec(0), sec(1), sec(2), sec(3),
                  pl.BlockSpec((1, seq, GATE_LANES), lambda i: (i, 0, 0)),
                  full((3, B_DIM)), full((3, B_DIM)), full((1, GATE_LANES)), full((1, B_DIM)),
                  st_c, st_m],
        out_specs=[pl.BlockSpec((1, seq, B_DIM), lambda i: (i, 0, 0)), st_c, st_m],
        out_shape=[jax.ShapeDtypeStruct((b, seq, B_DIM), BF16),
                   jax.ShapeDtypeStruct(c0.shape, F32),
                   jax.ShapeDtypeStruct(m0.shape, F32)],
        scratch_shapes=[pltpu.VMEM((seq, LANES), F32), pltpu.VMEM((seq, LANES), F32),
                        pltpu.VMEM((nc, N_CHAINS, CHUNK), F32),
                        pltpu.VMEM((B_HEADS, nc, hd, CHUNK), BF16), pltpu.VMEM((seq, B_DIM), BF16),
                        pltpu.VMEM((seq, B_DIM), F32), pltpu.VMEM((seq, B_DIM), F32),
                        pltpu.VMEM((N_CHAINS, hd, 2 * hd), F32)],
        compiler_params=_cparams(("arbitrary",), 56),
        name="mlstm",
    )(p, p, p, p, gates, wq, wk, bias_row, head_g, c0, m0)


def _merge0_kernel(u_ref, va_ref, mem_ref, x_ref, sw_ref, sb_ref, wo_ref, pg_ref, gate_ref, out_ref, cat_ref, *, tm):
    T = CHUNK
    gd = A_DIM // A_GROUPS
    for c in range(tm // T):
        sl = slice(c * T, (c + 1) * T)
        vn = _ln(_gelu(va_ref[0, sl, :].astype(F32))).astype(BF16)
        u = _gelu(u_ref[0, sl, :].astype(F32))
        for g in range(A_GROUPS):
            gs = slice(g * gd, (g + 1) * gd)
            mixed = _dot(sw_ref[g], vn[:, gs]) + sb_ref[g]
            cat_ref[sl, gs] = (u[:, gs] * mixed).astype(BF16)
    cat_ref[:, A_DIM:] = mem_ref[0]
    y = _dot(cat_ref[...], wo_ref[...])
    out_ref[0] = x_ref[0] + gate_ref[0] * _rms(y, pg_ref[...])


def _merge0(p, mem, x, sgu_w, sgu_bb, w_out, post_g, gate, tm):
    nb = x.shape[0]
    d = D_MODEL
    per_block = gate.shape[0] != 1
    blocks_per_gate = nb // gate.shape[0]
    gate_map = (lambda i: (i // blocks_per_gate, 0, 0)) if per_block else (lambda i: (0, 0, 0))
    return pl.pallas_call(
        functools.partial(_merge0_kernel, tm=tm),
        grid=(nb,),
        in_specs=[pl.BlockSpec((1, tm, A_DIM), lambda i: (i, 0, 4)),
                  pl.BlockSpec((1, tm, A_DIM), lambda i: (i, 0, 5)),
                  pl.BlockSpec((1, tm, B_DIM), lambda i: (i, 0, 0)),
                  pl.BlockSpec((1, tm, d), lambda i: (i, 0, 0)),
                  pl.BlockSpec((A_GROUPS, CHUNK, CHUNK), lambda i: (0, 0, 0)),
                  pl.BlockSpec((A_GROUPS, CHUNK, CHUNK), lambda i: (0, 0, 0)),
                  pl.BlockSpec((A_DIM + B_DIM, d), lambda i: (0, 0)),
                  pl.BlockSpec((1, d), lambda i: (0, 0)),
                  pl.BlockSpec((1, 1, d), gate_map)],
        out_specs=pl.BlockSpec((1, tm, d), lambda i: (i, 0, 0)),
        out_shape=jax.ShapeDtypeStruct((nb, tm, d), F32),
        scratch_shapes=[pltpu.VMEM((tm, A_DIM + B_DIM), BF16)],
        compiler_params=_cparams(("arbitrary",), 32),
        name="merge0",
    )(p, p, mem, x, sgu_w, sgu_bb, w_out, post_g.reshape(1, d), gate)


FFN_TK = 256
FFN_HALO = 64
FFN_PAD = 8


def _ffn_kernel(*refs, rows, tile_rows, width, grid2d, n_tiles):
    if grid2d:
        (x_ref, xu_ref, xd_ref, g_ref, sh_ref, sc_ref, wup_ref, cw_ref, wd_ref, pg_ref, gate_ref,
         o_ref, h_ref, gs0_ref, gs1_ref, act_ref) = refs
    else:
        (x_ref, g_ref, sh_ref, sc_ref, wup_ref, cw_ref, wd_ref, pg_ref, gate_ref,
         o_ref, h_ref, gs0_ref, gs1_ref, act_ref) = refs
    halo = FFN_HALO if grid2d else 0
    tk = FFN_TK
    ks = D_FF // tk
    hrows = rows + 2 * halo
    sub = 512

    def normmod(xv):
        return _rms(xv, g_ref[...]) * (1.0 + sc_ref[0]) + sh_ref[0]

    for r0 in range(0, rows, sub):
        h_ref[halo + r0:halo + r0 + sub, :] = normmod(x_ref[0, r0:r0 + sub, :]).astype(BF16)
    if grid2d:
        t = pl.program_id(0) % n_tiles
        h_ref[0:halo, :] = (normmod(xu_ref[0]) * (t > 0).astype(F32)).astype(BF16)
        h_ref[halo + rows:hrows, :] = (normmod(xd_ref[0]) * (t < n_tiles - 1).astype(F32)).astype(BF16)

    for gs_ref in (gs0_ref, gs1_ref):
        gs_ref[0:FFN_PAD, :] = jnp.zeros((FFN_PAD, tk), F32)
        gs_ref[FFN_PAD + hrows:2 * FFN_PAD + hrows, :] = jnp.zeros((FFN_PAD, tk), F32)

    def gate_branch(slab, gs_ref):
        for r0 in range(0, hrows, sub):
            r1 = min(r0 + sub, hrows)
            gs_ref[FFN_PAD + r0:FFN_PAD + r1, :] = _dot(h_ref[r0:r1, :], wup_ref[ks + slab])

    iota8 = lax.broadcasted_iota(jnp.int32, (8, tk), 0)

    def cut(vals, first):
        pieces = []
        for g0 in range(0, tile_rows, width):
            if first:
                pieces += [jnp.where(iota8 == 0, 0.0, vals[g0:g0 + 8]), vals[g0 + 8:g0 + width]]
            else:
                pieces += [vals[g0:g0 + width - 8], jnp.where(iota8 == 7, 0.0, vals[g0 + width - 8:g0 + width])]
        return jnp.concatenate(pieces, axis=0)

    def tiles(slab, gs_ref):
        cw = cw_ref[slab]
        for base in range(0, rows, tile_rows):
            dys = (0, 1, 2) if grid2d else (1,)
            g_rows = [gs_ref[FFN_PAD + halo + base + (dy - 1) * halo:FFN_PAD + halo + base + (dy - 1) * halo + tile_rows, :]
                      for dy in dys]
            left, mid, right = (functools.reduce(lambda s, t: s + t,
                                                 [cw[3 * dy + dx:3 * dy + dx + 1] * g for dy, g in zip(dys, g_rows)])
                                for dx in range(3))
            conv = (mid + cut(pltpu.roll(left, 1, 0), True) + cut(pltpu.roll(right, tile_rows - 1, 0), False))
            a = _dot(h_ref[halo + base:halo + base + tile_rows, :], wup_ref[slab])
            t = jnp.tanh(conv * (0.7978845608028654 + (0.7978845608028654 * 0.044715) * (conv * conv)))
            p = a * conv
            act_ref[slab, base:base + tile_rows, :] = (p + p * t).astype(BF16)

    assert ks % 2 == 1
    gate_branch(0, gs0_ref)

    def body(j, carry):
        k0 = 2 * j
        gate_branch(k0 + 1, gs1_ref)
        tiles(k0, gs0_ref)
        gate_branch(k0 + 2, gs0_ref)
        tiles(k0 + 1, gs1_ref)
        return carry

    lax.fori_loop(0, ks // 2, body, 0)
    tiles(ks - 1, gs0_ref)

    out_rows = 256
    for r0 in range(0, rows, out_rows):
        f = _dot(act_ref[0, r0:r0 + out_rows, :], wd_ref[0])
        for k in range(1, ks):
            f = f + _dot(act_ref[k, r0:r0 + out_rows, :], wd_ref[k])
        o_ref[0, r0:r0 + out_rows, :] = x_ref[0, r0:r0 + out_rows, :] + gate_ref[0] * _rms(f, pg_ref[...])


def _ffn(x3, pre_g, shift, scale, w_up3, conv3, w_down3, post_g, gate, *, rows, width, grid2d):
    nb, seq, d = x3.shape
    tk = FFN_TK
    ks = D_FF // tk
    n_tiles = seq // rows
    tile_rows = max(width, 128)
    halo = FFN_HALO if grid2d else 0
    assert seq % rows == 0 and rows % 512 == 0 and tile_rows % width == 0 and (not grid2d or width == FFN_HALO)
    per_block = shift.shape[0] != 1
    mod_map = (lambda i: (i // n_tiles, 0, 0)) if per_block else (lambda i: (0, 0, 0))
    vec = lambda: pl.BlockSpec((1, d), lambda i: (0, 0))
    mod = lambda: pl.BlockSpec((1, 1, d), mod_map)
    resident = lambda shape: pl.BlockSpec(shape, lambda i: (0,) * len(shape), pipeline_mode=pl.Buffered(1))
    in_specs = [pl.BlockSpec((1, rows, d), lambda i: (i // n_tiles, i % n_tiles, 0))]
    args = [x3]
    if grid2d:
        per = rows // halo
        last = seq // halo - 1
        in_specs += [pl.BlockSpec((1, halo, d), lambda i: (i // n_tiles, jnp.maximum((i % n_tiles) * per - 1, 0), 0)),
                     pl.BlockSpec((1, halo, d), lambda i: (i // n_tiles, jnp.minimum((i % n_tiles + 1) * per, last), 0))]
        args += [x3, x3]
    in_specs += [vec(), mod(), mod(),
                 resident((2 * ks, d, tk)), resident((ks, 9, tk)), resident((ks, tk, d)),
                 vec(), mod()]
    args += [pre_g.reshape(1, d), shift, scale, w_up3, conv3, w_down3, post_g.reshape(1, d), gate]
    return pl.pallas_call(
        functools.partial(_ffn_kernel, rows=rows, tile_rows=tile_rows, width=width, grid2d=grid2d, n_tiles=n_tiles),
        grid=(nb * n_tiles,),
        in_specs=in_specs,
        out_specs=pl.BlockSpec((1, rows, d), lambda i: (i // n_tiles, i % n_tiles, 0)),
        out_shape=jax.ShapeDtypeStruct((nb, seq, d), F32),
        scratch_shapes=[pltpu.VMEM((rows + 2 * halo, d), BF16),
                        pltpu.VMEM((rows + 2 * halo + 2 * FFN_PAD, tk), F32),
                        pltpu.VMEM((rows + 2 * halo + 2 * FFN_PAD, tk), F32),
                        pltpu.VMEM((ks, rows, tk), BF16)],
        compiler_params=_cparams(("arbitrary",), 56),
        name="ffn",
    )(*args)


RET_CHUNK = 256


def _ret_kernel(k_ref, v_ref, q_ref, kc_ref, vc_ref, dec_ref, r_ref, rf_ref, rb_ref, *, seq, ctx_len):
    T = RET_CHUNK
    nc = seq // T
    t_col = lax.broadcasted_iota(jnp.int32, (T, 1), 0).astype(F32)
    lg = [_log_sigmoid(dec_ref[0, direction][0:1, 0:1]) for direction in range(2)]
    scale = C_QK_DIM ** -0.5
    zeta = (jnp.exp(lg[0] * (T - 1 - t_col)) * scale, jnp.exp(lg[1] * t_col) * scale)
    xi = (jnp.exp(lg[0] * (t_col + 1.0)), jnp.exp(lg[1] * (T - t_col)))
    gch = (jnp.exp(lg[0] * T), jnp.exp(lg[1] * T))
    row = lax.broadcasted_iota(jnp.int32, (T, T), 0)
    col = lax.broadcasted_iota(jnp.int32, (T, T), 1)
    rel = (row - col).astype(F32)
    d_intra = (jnp.where(row >= col, jnp.exp(lg[0] * rel), 0.0)
               + jnp.where(col >= row, jnp.exp(-lg[1] * rel), 0.0)) * scale

    def absorb(state_ref, direction, k, v, first):
        upd = _dot_tn((k.astype(F32) * zeta[direction]).astype(BF16), v)
        state_ref[...] = upd if first else gch[direction] * state_ref[...] + upd

    ncc = ctx_len // T
    for direction, state_ref in ((0, rf_ref), (1, rb_ref)):
        order = list(range(ncc)) if direction == 0 else list(range(ncc - 1, -1, -1))
        for n, c in enumerate(order):
            absorb(state_ref, direction, kc_ref[0, c * T:(c + 1) * T, :], vc_ref[0, c * T:(c + 1) * T, :], n == 0)

    def scan(i, first_visit):
        slf = pl.ds(pl.multiple_of(i * T, T), T)
        slb = pl.ds(pl.multiple_of((nc - 1 - i) * T, T), T)
        q, k, v = q_ref[0, slf, :], k_ref[0, slf, :], v_ref[0, slf, :]
        scores = _dot_nt(q, k) * d_intra
        fwd = _dot(scores.astype(BF16), v) + xi[0] * _dot(q, rf_ref[...].astype(BF16))
        absorb(rf_ref, 0, k, v, False)
        qb, kb, vb = q_ref[0, slb, :], k_ref[0, slb, :], v_ref[0, slb, :]
        bwd = xi[1] * _dot(qb, rb_ref[...].astype(BF16))
        absorb(rb_ref, 1, kb, vb, False)
        if first_visit:
            r_ref[0, slf, :] = fwd
            r_ref[0, slb, :] = bwd
        else:
            r_ref[0, slf, :] += fwd
            r_ref[0, slb, :] += bwd

    assert nc % 2 == 0

    def first_half(i, carry):
        scan(i, True)
        return carry

    def second_half(i, carry):
        scan(i, False)
        return carry

    lax.fori_loop(0, nc // 2, first_half, 0)
    lax.fori_loop(nc // 2, nc, second_half, 0)


def _ret(p, p_ctx, dec_b):
    b, seq, _ = p.shape
    ctx_len = p_ctx.shape[1]
    assert seq % RET_CHUNK == 0 and ctx_len % RET_CHUNK == 0
    return pl.pallas_call(
        functools.partial(_ret_kernel, seq=seq, ctx_len=ctx_len),
        grid=(b, C_HEADS),
        in_specs=[pl.BlockSpec((1, seq, C_QK_DIM), lambda i, h: (i, 0, h)),
                  pl.BlockSpec((1, seq, C_V_DIM), lambda i, h: (i, 0, 2 + h)),
                  pl.BlockSpec((1, seq, C_QK_DIM), lambda i, h: (i, 0, 12 + h)),
                  pl.BlockSpec((1, ctx_len, C_QK_DIM), lambda i, h: (i, 0, h)),
                  pl.BlockSpec((1, ctx_len, C_V_DIM), lambda i, h: (i, 0, 2 + h)),
                  pl.BlockSpec((1, 2, 8, LANES), lambda i, h: (h, 0, 0, 0))],
        out_specs=pl.BlockSpec((1, seq, C_V_DIM), lambda i, h: (i, 0, h)),
        out_shape=jax.ShapeDtypeStruct((b, seq, C_HEADS * C_V_DIM), F32),
        scratch_shapes=[pltpu.VMEM((C_QK_DIM, C_V_DIM), F32), pltpu.VMEM((C_QK_DIM, C_V_DIM), F32)],
        compiler_params=_cparams(("arbitrary", "arbitrary"), 48),
        name="ret",
    )(p, p, p, p_ctx, p_ctx, dec_b)


def _merge1_kernel(r_ref, g_ref, x_ref, hg_ref, wo_ref, pg_ref, gate_ref, out_ref, z_ref):
    for h in range(C_HEADS):
        hs = slice(h * C_V_DIM, (h + 1) * C_V_DIM)
        z = _silu(g_ref[0, :, hs].astype(F32)) * (_ln(r_ref[0, :, hs]) * hg_ref[:, hs])
        z_ref[:, hs] = z.astype(BF16)
    y = _dot(z_ref[...], wo_ref[...])
    out_ref[0] = x_ref[0] + gate_ref[0] * _rms(y, pg_ref[...])


def _merge1(r, p, x, head_g, w_out, post_g, gate, tm):
    nb = x.shape[0]
    d = D_MODEL
    hv = C_HEADS * C_V_DIM
    blocks_per_gate = nb // gate.shape[0]
    return pl.pallas_call(
        _merge1_kernel,
        grid=(nb,),
        in_specs=[pl.BlockSpec((1, tm, hv), lambda i: (i, 0, 0)),
                  pl.BlockSpec((1, tm, hv), lambda i: (i, 0, 2)),
                  pl.BlockSpec((1, tm, d), lambda i: (i, 0, 0)),
                  pl.BlockSpec((1, hv), lambda i: (0, 0)),
                  pl.BlockSpec((hv, d), lambda i: (0, 0)),
                  pl.BlockSpec((1, d), lambda i: (0, 0)),
                  pl.BlockSpec((1, 1, d), lambda i: (i // blocks_per_gate, 0, 0))],
        out_specs=pl.BlockSpec((1, tm, d), lambda i: (i, 0, 0)),
        out_shape=jax.ShapeDtypeStruct((nb, tm, d), F32),
        scratch_shapes=[pltpu.VMEM((tm, hv), BF16)],
        compiler_params=_cparams(("arbitrary",), 48),
        name="merge1",
    )(r, p, x, head_g, w_out, post_g.reshape(1, d), gate)


def kernel(x, c, ctx, c_ctx, ada_w, ada_b, pre_g, post_g, ffn_up, ffn_conv, ffn_down,
           ab_w_in, ab_qk_conv, ab_gate_b, ab_sgu_w, ab_sgu_b, ab_head_g, ab_w_out,
           ret_w_in, ret_decay, ret_head_g, ret_w_out):
    b, seq, d = x.shape
    ctx_len = ctx.shape[1]
    depth = ada_w.shape[0]
    tm = 512
    ctx_rows = min(2048, b * ctx_len)

    n_cond = b + 1
    n_pad = -n_cond % 8
    cs = jnp.concatenate([c, c_ctx[None, :], jnp.zeros((n_pad, d), F32)], axis=0)
    mods = _adaln(cs, ada_w, ada_b)

    def mod_l(layer, i):
        return mods[layer, :b, i * d:(i + 1) * d].reshape(b, 1, d)

    def mod_c(layer, i):
        return mods[layer, b:b + 1, i * d:(i + 1) * d].reshape(1, 1, d)

    n_slab = D_FF // FFN_TK
    ffn_up3 = ffn_up.astype(BF16).reshape(depth, d, 2 * n_slab, FFN_TK).transpose(0, 2, 1, 3)
    ffn_conv3 = ffn_conv.reshape(depth, 9, n_slab, FFN_TK).transpose(0, 2, 1, 3)
    ffn_down3 = (0.5 * ffn_down).astype(BF16).reshape(depth, n_slab, FFN_TK, d)

    def ffn_l(layer, xl):
        return _ffn(xl, pre_g[layer, 1], mod_l(layer, 3), mod_l(layer, 4), ffn_up3[layer], ffn_conv3[layer],
                    ffn_down3[layer], post_g[layer, 1], mod_l(layer, 5), rows=1024, width=GRID_W, grid2d=True)

    def ffn_c(layer, xc):
        out = _ffn(xc.reshape(1, b * ctx_len, d), pre_g[layer, 1], mod_c(layer, 3), mod_c(layer, 4), ffn_up3[layer],
                   ffn_conv3[layer], ffn_down3[layer], post_g[layer, 1], mod_c(layer, 5),
                   rows=min(1024, b * ctx_len), width=ctx_len, grid2d=False)
        return out.reshape(xc.shape)

    w_in = ab_w_in[0]
    g_lo = 2 * B_DIM
    g_hi = g_lo + N_GATES * B_HEADS
    w_main = jnp.concatenate([w_in[:, :g_lo], w_in[:, g_hi:]], axis=1).astype(BF16)
    w_gates = w_in[:, g_lo:g_hi].reshape(d, N_GATES, B_HEADS)
    lane_pad = ((0, 0), (0, LANES - N_CHAINS))
    w_gate = jnp.concatenate([jnp.pad(w_gates[:, 0::2].reshape(d, N_CHAINS), lane_pad),
                              jnp.pad(w_gates[:, 1::2].reshape(d, N_CHAINS), lane_pad)], axis=1).astype(BF16)
    bias_row = jnp.concatenate([jnp.pad(ab_gate_b[0][0::2].reshape(1, N_CHAINS), lane_pad),
                                jnp.pad(ab_gate_b[0][1::2].reshape(1, N_CHAINS), lane_pad)], axis=1)
    wq = ab_qk_conv[0][:, :B_DIM]
    wk = ab_qk_conv[0][:, B_DIM:]
    head_g0 = ab_head_g[0].reshape(1, B_DIM)
    sgu_w = ab_sgu_w[0].astype(BF16)
    sgu_bb = jnp.broadcast_to(ab_sgu_b[0][:, :, None], (A_GROUPS, CHUNK, CHUNK))
    w_out0 = ab_w_out[0].astype(BF16)

    p_c, gates_c = _proj(ctx.reshape(-1, ctx_rows, d), pre_g[0, 0], mod_c(0, 0), mod_c(0, 1), w_main, w_gate)
    p_l, gates_l = _proj(x, pre_g[0, 0], mod_l(0, 0), mod_l(0, 1), w_main, w_gate)
    p_c = p_c.reshape(b, ctx_len, -1)
    gates_c = gates_c.reshape(b, ctx_len, GATE_LANES)
    zero_state = (jnp.zeros((b, N_CHAINS, B_HEAD_DIM, 2 * B_HEAD_DIM), F32),
                  jnp.zeros((b, N_CHAINS, 1, B_HEAD_DIM), F32))
    mem_c, *ctx_state = _mlstm(p_c, gates_c, wq, wk, bias_row, head_g0, zero_state)
    mem_l, *_ = _mlstm(p_l, gates_l, wq, wk, bias_row, head_g0, tuple(ctx_state))

    rt = lambda t: t.reshape(-1, tm, t.shape[-1])
    x = _merge0(rt(p_l), rt(mem_l), rt(x), sgu_w, sgu_bb, w_out0, post_g[0, 0], mod_l(0, 2), tm).reshape(b, seq, d)
    ctx = _merge0(rt(p_c), rt(mem_c), rt(ctx), sgu_w, sgu_bb, w_out0, post_g[0, 0], mod_c(0, 2), tm).reshape(b, ctx_len, d)
    x = ffn_l(0, x)
    ctx = ffn_c(0, ctx)

    w_ret = ret_w_in[0].astype(BF16)
    kv_cols = C_HEADS * (C_QK_DIM + C_V_DIM)
    dec_b = jnp.broadcast_to(ret_decay[0].T[:, :, None, None], (C_HEADS, 2, 8, LANES))
    p_c = _proj(ctx.reshape(-1, ctx_rows, d), pre_g[1, 0], mod_c(1, 0), mod_c(1, 1), w_ret[:, :kv_cols])
    p_l = _proj(x, pre_g[1, 0], mod_l(1, 0), mod_l(1, 1), w_ret)
    r = _ret(p_l, p_c.reshape(b, ctx_len, -1), dec_b)
    x = _merge1(rt(r), rt(p_l), rt(x), ret_head_g[0].reshape(1, -1), ret_w_out[0].astype(BF16),
                post_g[1, 0], mod_l(1, 2), tm).reshape(b, seq, d)
    x = ffn_l(1, x)
    return x
```

```python
import functools

import jax
import jax.numpy as jnp
from jax import lax
from jax.experimental import pallas as pl
from jax.experimental.pallas import tpu as pltpu

F32 = jnp.float32
BF16 = jnp.bfloat16
HIGHEST = lax.Precision.HIGHEST

D_MODEL = 1024
GRID_W = 64
CHUNK = 128
EPS = 1e-6
N_MOD = 6
A_GROUPS = 4
A_DIM = 512
B_HEADS = 4
B_HEAD_DIM = 128
B_DIM = 512
N_GATES = 4
C_HEADS = 4
C_QK_DIM = 256
C_V_DIM = 512
D_FF = 2816

LANES = 128
GATE_LANES = 2 * LANES
N_CHAINS = 2 * B_HEADS
V7X_VMEM_BYTES = 64 * 1024 * 1024
MIB = 1024 * 1024


def _cparams(semantics, vmem_mib):
    assert vmem_mib * MIB < V7X_VMEM_BYTES
    return pltpu.CompilerParams(dimension_semantics=semantics, vmem_limit_bytes=vmem_mib * MIB)


def _gelu(x):
    return x * (0.5 * (1.0 + jnp.tanh(0.7978845608028654 * (x + 0.044715 * (x * x * x)))))


def _silu(x):
    return x * jax.nn.sigmoid(x)


def _log_sigmoid(x):
    return jnp.minimum(x, 0.0) - jnp.log1p(jnp.exp(-jnp.abs(x)))


def _rms(x, g):
    return x * lax.rsqrt(jnp.mean(x * x, axis=-1, keepdims=True) + EPS) * g


def _ln(x):
    xc = x - jnp.mean(x, axis=-1, keepdims=True)
    return xc * lax.rsqrt(jnp.mean(xc * xc, axis=-1, keepdims=True) + EPS)


def _dot(a, b):
    return jnp.dot(a, b, preferred_element_type=F32)


def _dot_f32(a, b):
    return jnp.dot(a, b, precision=HIGHEST, preferred_element_type=F32)


def _dot_nt(a, b):
    return lax.dot_general(a, b, (((1,), (1,)), ((), ())), preferred_element_type=F32)


def _dot_tn(a, b):
    return lax.dot_general(a, b, (((0,), (0,)), ((), ())), preferred_element_type=F32)


def _adaln_kernel(c_ref, w_ref, b_ref, o_ref):
    s = _silu(c_ref[...])
    o_ref[0] = _dot(s.astype(BF16), w_ref[0].astype(BF16)) + b_ref[0]


def _adaln(cs, ada_w, ada_b):
    depth, d, n = ada_w.shape
    mp = cs.shape[0]
    tn = 1024
    return pl.pallas_call(
        _adaln_kernel,
        grid=(depth, n // tn),
        in_specs=[pl.BlockSpec((mp, d), lambda l, j: (0, 0)),
                  pl.BlockSpec((1, d, tn), lambda l, j: (l, 0, j)),
                  pl.BlockSpec((1, 1, tn), lambda l, j: (l, 0, j))],
        out_specs=pl.BlockSpec((1, mp, tn), lambda l, j: (l, 0, j)),
        out_shape=jax.ShapeDtypeStruct((depth, mp, n), F32),
        compiler_params=_cparams(("arbitrary", "arbitrary"), 32),
        name="adaln",
    )(cs, ada_w, ada_b.reshape(depth, 1, n))


def _proj_kernel(x_ref, g_ref, sh_ref, sc_ref, w_ref, *rest, rows, sub, with_gates):
    if with_gates:
        wg_ref, p_ref, gate_ref, h_ref = rest
    else:
        p_ref, h_ref = rest

    @pl.when(pl.program_id(1) == 0)
    def _():
        for r0 in range(0, rows, sub):
            h = _rms(x_ref[0, r0:r0 + sub, :], g_ref[...]) * (1.0 + sc_ref[0]) + sh_ref[0]
            hb = h.astype(BF16)
            h_ref[r0:r0 + sub, :] = hb
            if with_gates:
                gate_ref[0, r0:r0 + sub, :] = _dot(hb, wg_ref[...])
            p_ref[0, r0:r0 + sub, :] = _dot(hb, w_ref[...]).astype(BF16)

    @pl.when(pl.program_id(1) != 0)
    def _():
        for r0 in range(0, rows, sub):
            p_ref[0, r0:r0 + sub, :] = _dot(h_ref[r0:r0 + sub, :], w_ref[...]).astype(BF16)


def _proj(x3, pre_g, shift, scale, w, wg=None):
    nb, rows, d = x3.shape
    n = w.shape[1]
    tn = 1024
    per_block = shift.shape[0] != 1
    mod_map = (lambda i, j: (i, 0, 0)) if per_block else (lambda i, j: (0, 0, 0))
    in_specs = [pl.BlockSpec((1, rows, d), lambda i, j: (i, 0, 0)),
                pl.BlockSpec((1, d), lambda i, j: (0, 0)),
                pl.BlockSpec((1, 1, d), mod_map),
                pl.BlockSpec((1, 1, d), mod_map),
                pl.BlockSpec((d, tn), lambda i, j: (0, j))]
    out_specs = [pl.BlockSpec((1, rows, tn), lambda i, j: (i, 0, j))]
    out_shape = [jax.ShapeDtypeStruct((nb, rows, n), BF16)]
    args = [x3, pre_g.reshape(1, d), shift, scale, w]
    if wg is not None:
        in_specs.append(pl.BlockSpec((d, GATE_LANES), lambda i, j: (0, 0)))
        out_specs.append(pl.BlockSpec((1, rows, GATE_LANES), lambda i, j: (i, 0, 0)))
        out_shape.append(jax.ShapeDtypeStruct((nb, rows, GATE_LANES), F32))
        args.append(wg)
    outs = pl.pallas_call(
        functools.partial(_proj_kernel, rows=rows, sub=min(512, rows), with_gates=wg is not None),
        grid=(nb, n // tn),
        in_specs=in_specs, out_specs=out_specs, out_shape=out_shape,
        scratch_shapes=[pltpu.VMEM((rows, d), BF16)],
        compiler_params=_cparams(("arbitrary", "arbitrary"), 48),
        name="proj",
    )(*args)
    return outs if wg is not None else outs[0]


def _mlstm_kernel(k_ref, v_ref, q_ref, o_ref, gates_ref, wq_ref, wk_ref, bias_ref, hg_ref, c0_ref, m0_ref,
                  mem_ref, cf_ref, mf_ref,
                  pm_ref, cum_ref, brow_ref, kt_ref, qc_ref, rf_ref, rb_ref, st_ref, *, seq):
    T = CHUNK
    hd = B_HEAD_DIM
    nc = seq // T
    row = lax.broadcasted_iota(jnp.int32, (T, T), 0)
    col = lax.broadcasted_iota(jnp.int32, (T, T), 1)
    lower = row >= col
    upper = col >= row
    fwd_lane = lax.broadcasted_iota(jnp.int32, (T, LANES), 1) < B_HEADS
    trow = lax.broadcasted_iota(jnp.int32, (T, LANES), 0)

    def gate_body(c, carry):
        sl = pl.ds(pl.multiple_of(c * T, T), T)
        g = gates_ref[0, sl, :] + bias_ref[...]
        lf = _log_sigmoid(g[:, LANES:])
        prefix = _dot_f32(lower.astype(F32), lf)
        cum = jnp.where(fwd_lane, prefix, prefix[T - 1:T] - prefix + lf)
        b = g[:, :LANES] - cum
        pf = b
        ps = b
        for sh in (1, 2, 4, 8, 16, 32, 64):
            pf = jnp.where(trow >= sh, jnp.maximum(pf, pltpu.roll(pf, sh, 0)), pf)
            ps = jnp.where(trow < T - sh, jnp.maximum(ps, pltpu.roll(ps, T - sh, 0)), ps)
        pm_ref[sl, :] = jnp.where(fwd_lane, pf, ps)
        cum_ref[sl, :] = cum
        brow_ref[c] = b.T[:N_CHAINS]
        return carry

    lax.fori_loop(0, nc, gate_body, 0)

    ridx = lax.broadcasted_iota(jnp.int32, (T, hd), 0)

    def conv_body(c, carry):
        start = pl.multiple_of(c * T, T)
        sl = pl.ds(start, T)
        prev_start = pl.multiple_of(jnp.maximum(start - 16, 0), 16)
        next_start = pl.multiple_of(jnp.minimum(start + T, seq - 16), 16)
        has_prev = jnp.where(c > 0, 1.0, 0.0)
        has_next = jnp.where(c < nc - 1, 1.0, 0.0)
        for h in range(B_HEADS):
            hs = slice(h * hd, (h + 1) * hd)
            for src_ref, w_ref, is_k in ((k_ref, wk_ref, True), (q_ref, wq_ref, False)):
                cur = src_ref[0, sl, hs].astype(F32)
                prev_row = src_ref[0, pl.ds(prev_start, 16), hs].astype(F32)[15:16] * has_prev
                next_row = src_ref[0, pl.ds(next_start, 16), hs].astype(F32)[0:1] * has_next
                w = w_ref[:, hs]
                xm = jnp.where(ridx == 0, prev_row, pltpu.roll(cur, 1, 0))
                xp = jnp.where(ridx == T - 1, next_row, pltpu.roll(cur, T - 1, 0))
                y = _silu(xm * w[0:1] + cur * w[1:2] + xp * w[2:3])
                if is_k:
                    kt_ref[h, c] = (y * hd ** -0.5).T.astype(BF16)
                else:
                    qc_ref[sl, hs] = y.astype(BF16)
        return carry

    lax.fori_loop(0, nc, conv_body, 0)

    ones_tile = (lax.broadcasted_iota(jnp.int32, (T, hd), 1) == 0).astype(BF16)
    for j in range(N_CHAINS):
        st_ref[j] = c0_ref[0, j]

    def scan_body(i, ms):
        chains = range(N_CHAINS)
        heads = [j % B_HEADS for j in chains]
        cs = [i if j < B_HEADS else nc - 1 - i for j in chains]
        sls = [pl.ds(pl.multiple_of(cs[j] * T, T), T) for j in chains]
        hss = [slice(h * hd, (h + 1) * hd) for h in heads]
        edge = [cs[j] * T + (T - 1 if j < B_HEADS else 0) for j in chains]
        b_row = [brow_ref[cs[j], j:j + 1, :] for j in chains]
        m_col = [jnp.maximum(ms[j], pm_ref[sls[j], j:j + 1]) for j in chains]
        m_wide = [jnp.broadcast_to(m_col[j], (T, T)) for j in chains]
        q = [qc_ref[sls[j], hss[j]] for j in chains]
        kt = [kt_ref[heads[j], cs[j]] for j in chains]
        dmat = [jnp.where(lower if j < B_HEADS else upper, jnp.exp(b_row[j] - m_wide[j]), 0.0) for j in chains]
        p = [(_dot(q[j], kt[j]) * dmat[j]).astype(BF16) for j in chains]
        v_aug = [jnp.concatenate([v_ref[0, sls[j], hss[j]], ones_tile], axis=1) for j in chains]
        st = [st_ref[j] for j in chains]
        w_prev = [jnp.exp(ms[j] - m_wide[j]) for j in chains]
        acc = [_dot(p[j], v_aug[j]) + jnp.concatenate([w_prev[j], w_prev[j]], axis=1) * _dot(q[j], st[j].astype(BF16))
               for j in chains]
        for j in chains:
            out_ref = rf_ref if j < B_HEADS else rb_ref
            den = acc[j][:, hd:hd + 1]
            floor = jnp.exp(-(cum_ref[sls[j], j:j + 1] + m_col[j]))
            out_ref[sls[j], hss[j]] = acc[j][:, :hd] / jnp.maximum(jnp.abs(den), floor)
        new_ms = []
        for j in chains:
            btot = cum_ref[pl.ds(edge[j], 1), j:j + 1]
            m_far = jnp.maximum(ms[j], pm_ref[pl.ds(edge[j], 1), j:j + 1])
            m_new = btot + m_far
            kw_t = (kt[j].astype(F32) * jnp.exp(btot + b_row[j] - m_new)).astype(BF16)
            st_ref[j] = jnp.exp(ms[j] - m_far) * st[j] + _dot(kw_t, v_aug[j])
            new_ms.append(m_new)
        return tuple(new_ms)

    ms = lax.fori_loop(0, nc, scan_body, tuple(m0_ref[0, j][:, 0:1] for j in range(N_CHAINS)), unroll=2)
    for j in range(N_CHAINS):
        cf_ref[0, j] = st_ref[j]
        mf_ref[0, j] = jnp.broadcast_to(ms[j], (1, hd))

    def out_body(c, carry):
        sl = pl.ds(pl.multiple_of(c * T, T), T)
        for h in range(B_HEADS):
            hs = slice(h * hd, (h + 1) * hd)
            y = jax.nn.sigmoid(o_ref[0, sl, hs].astype(F32)) * (rf_ref[sl, hs] + rb_ref[sl, hs])
            mem_ref[0, sl, hs] = (_ln(y) * hg_ref[:, hs]).astype(BF16)
        return carry

    lax.fori_loop(0, nc, out_body, 0)


def _mlstm(p, gates, wq, wk, bias_row, head_g, state):
    b, seq, _ = p.shape
    c0, m0 = state
    hd = B_HEAD_DIM
    nc = seq // CHUNK
    sec = lambda idx: pl.BlockSpec((1, seq, B_DIM), lambda i, idx=idx: (i, 0, idx))
    full = lambda shape: pl.BlockSpec(shape, lambda i: (0,) * len(shape))
    st_c = pl.BlockSpec((1, N_CHAINS, hd, 2 * hd), lambda i: (i, 0, 0, 0))
    st_m = pl.BlockSpec((1, N_CHAINS, 1, hd), lambda i: (i, 0, 0, 0))
    return pl.pallas_call(
        functools.partial(_mlstm_kernel, seq=seq),
        grid=(b,),
        in_specs=[sec(0), sec(1), sec(2), sec(3),
                  pl.BlockSpec((1, seq, GATE_LANES), lambda i: (i, 0, 0)),
                  full((3, B_DIM)), full((3, B_DIM)), full((1, GATE_LANES)), full((1, B_DIM)),
                  st_c, st_m],
        out_specs=[pl.BlockSpec((1, seq, B_DIM), lambda i: (i, 0, 0)), st_c, st_m],
        out_shape=[jax.ShapeDtypeStruct((b, seq, B_DIM), BF16),
                   jax.ShapeDtypeStruct(c0.shape, F32),
                   jax.ShapeDtypeStruct(m0.shape, F32)],
        scratch_shapes=[pltpu.VMEM((seq, LANES), F32), pltpu.VMEM((seq, LANES), F32),
                        pltpu.VMEM((nc, N_CHAINS, CHUNK), F32),
                        pltpu.VMEM((B_HEADS, nc, hd, CHUNK), BF16), pltpu.VMEM((seq, B_DIM), BF16),
                        pltpu.VMEM((seq, B_DIM), F32), pltpu.VMEM((seq, B_DIM), F32),
                        pltpu.VMEM((N_CHAINS, hd, 2 * hd), F32)],
        compiler_params=_cparams(("arbitrary",), 56),
        name="mlstm",
    )(p, p, p, p, gates, wq, wk, bias_row, head_g, c0, m0)


def _merge0_kernel(u_ref, va_ref, mem_ref, x_ref, sw_ref, sb_ref, wo_ref, pg_ref, gate_ref, out_ref, cat_ref, *, tm):
    T = CHUNK
    gd = A_DIM // A_GROUPS
    for c in range(tm // T):
        sl = slice(c * T, (c + 1) * T)
        vn = _ln(_gelu(va_ref[0, sl, :].astype(F32))).astype(BF16)
        u = _gelu(u_ref[0, sl, :].astype(F32))
        for g in range(A_GROUPS):
            gs = slice(g * gd, (g + 1) * gd)
            mixed = _dot(sw_ref[g], vn[:, gs]) + sb_ref[g]
            cat_ref[sl, gs] = (u[:, gs] * mixed).astype(BF16)
    cat_ref[:, A_DIM:] = mem_ref[0]
    y = _dot(cat_ref[...], wo_ref[...])
    out_ref[0] = x_ref[0] + gate_ref[0] * _rms(y, pg_ref[...])


def _merge0(p, mem, x, sgu_w, sgu_bb, w_out, post_g, gate, tm):
    nb = x.shape[0]
    d = D_MODEL
    per_block = gate.shape[0] != 1
    blocks_per_gate = nb // gate.shape[0]
    gate_map = (lambda i: (i // blocks_per_gate, 0, 0)) if per_block else (lambda i: (0, 0, 0))
    return pl.pallas_call(
        functools.partial(_merge0_kernel, tm=tm),
        grid=(nb,),
        in_specs=[pl.BlockSpec((1, tm, A_DIM), lambda i: (i, 0, 4)),
                  pl.BlockSpec((1, tm, A_DIM), lambda i: (i, 0, 5)),
                  pl.BlockSpec((1, tm, B_DIM), lambda i: (i, 0, 0)),
                  pl.BlockSpec((1, tm, d), lambda i: (i, 0, 0)),
                  pl.BlockSpec((A_GROUPS, CHUNK, CHUNK), lambda i: (0, 0, 0)),
                  pl.BlockSpec((A_GROUPS, CHUNK, CHUNK), lambda i: (0, 0, 0)),
                  pl.BlockSpec((A_DIM + B_DIM, d), lambda i: (0, 0)),
                  pl.BlockSpec((1, d), lambda i: (0, 0)),
                  pl.BlockSpec((1, 1, d), gate_map)],
        out_specs=pl.BlockSpec((1, tm, d), lambda i: (i, 0, 0)),
        out_shape=jax.ShapeDtypeStruct((nb, tm, d), F32),
        scratch_shapes=[pltpu.VMEM((tm, A_DIM + B_DIM), BF16)],
        compiler_params=_cparams(("arbitrary",), 32),
        name="merge0",
    )(p, p, mem, x, sgu_w, sgu_bb, w_out, post_g.reshape(1, d), gate)


FFN_TK = 256
FFN_HALO = 64
FFN_PAD = 8
FFN_MM_ROWS = 512


def _ffn_kernel(*refs, rows, tile_rows, width, grid2d, n_tiles):
    if grid2d:
        (x_ref, xu_ref, xd_ref, g_ref, sh_ref, sc_ref, wup_ref, cw_ref, wd_ref, pg_ref, gate_ref,
         o_ref, h_ref, gs0_ref, gs1_ref, act_ref) = refs
    else:
        (x_ref, g_ref, sh_ref, sc_ref, wup_ref, cw_ref, wd_ref, pg_ref, gate_ref,
         o_ref, h_ref, gs0_ref, gs1_ref, act_ref) = refs
    halo = FFN_HALO if grid2d else 0
    tk = FFN_TK
    ks = D_FF // tk
    hrows = rows + 2 * halo
    sub = 512

    def normmod(xv):
        return _rms(xv, g_ref[...]) * (1.0 + sc_ref[0]) + sh_ref[0]

    for r0 in range(0, rows, sub):
        h_ref[halo + r0:halo + r0 + sub, :] = normmod(x_ref[0, r0:r0 + sub, :]).astype(BF16)
    if grid2d:
        t = pl.program_id(0) % n_tiles
        h_ref[0:halo, :] = (normmod(xu_ref[0]) * (t > 0).astype(F32)).astype(BF16)
        h_ref[halo + rows:hrows, :] = (normmod(xd_ref[0]) * (t < n_tiles - 1).astype(F32)).astype(BF16)

    for gs_ref in (gs0_ref, gs1_ref):
        gs_ref[0:FFN_PAD, :] = jnp.zeros((FFN_PAD, tk), F32)
        gs_ref[FFN_PAD + hrows:2 * FFN_PAD + hrows, :] = jnp.zeros((FFN_PAD, tk), F32)

    def gate_branch(slab, gs_ref):
        n_chunks = hrows // FFN_MM_ROWS
        for r0, r1 in zip(range(0, hrows, hrows // n_chunks), range(hrows // n_chunks, hrows + 1, hrows // n_chunks)):
            gs_ref[FFN_PAD + r0:FFN_PAD + r1, :] = _dot(h_ref[r0:r1, :], wup_ref[ks + slab])

    iota8 = lax.broadcasted_iota(jnp.int32, (8, tk), 0)

    def cut(vals, first):
        pieces = []
        for g0 in range(0, tile_rows, width):
            if first:
                pieces += [jnp.where(iota8 == 0, 0.0, vals[g0:g0 + 8]), vals[g0 + 8:g0 + width]]
            else:
                pieces += [vals[g0:g0 + width - 8], jnp.where(iota8 == 7, 0.0, vals[g0 + width - 8:g0 + width])]
        return jnp.concatenate(pieces, axis=0)

    def tiles(slab, gs_ref):
        cw = cw_ref[slab]
        for base in range(0, rows, tile_rows):
            if base % FFN_MM_ROWS == 0:
                a_rows = _dot(h_ref[halo + base:halo + base + FFN_MM_ROWS, :], wup_ref[slab])
            a = a_rows[base % FFN_MM_ROWS:base % FFN_MM_ROWS + tile_rows]
            dys = (0, 1, 2) if grid2d else (1,)
            g_rows = [gs_ref[FFN_PAD + halo + base + (dy - 1) * halo:FFN_PAD + halo + base + (dy - 1) * halo + tile_rows, :]
                      for dy in dys]
            left, mid, right = (functools.reduce(lambda s, t: s + t,
                                                 [cw[3 * dy + dx:3 * dy + dx + 1] * g for dy, g in zip(dys, g_rows)])
                                for dx in range(3))
            conv = (mid + cut(pltpu.roll(left, 1, 0), True) + cut(pltpu.roll(right, tile_rows - 1, 0), False))
            t = jnp.tanh(conv * (0.7978845608028654 + (0.7978845608028654 * 0.044715) * (conv * conv)))
            p = a * conv
            act_ref[slab, base:base + tile_rows, :] = (p + p * t).astype(BF16)

    assert ks % 2 == 1
    gate_branch(0, gs0_ref)

    def body(j, carry):
        k0 = 2 * j
        gate_branch(k0 + 1, gs1_ref)
        tiles(k0, gs0_ref)
        gate_branch(k0 + 2, gs0_ref)
        tiles(k0 + 1, gs1_ref)
        return carry

    lax.fori_loop(0, ks // 2, body, 0)
    tiles(ks - 1, gs0_ref)

    for r0 in range(0, rows, FFN_MM_ROWS):
        for n0 in range(0, D_MODEL, tk):
            f = _dot(act_ref[0, r0:r0 + FFN_MM_ROWS, :], wd_ref[0, :, n0:n0 + tk])
            for k in range(1, ks):
                f = f + _dot(act_ref[k, r0:r0 + FFN_MM_ROWS, :], wd_ref[k, :, n0:n0 + tk])
            o_ref[0, r0:r0 + FFN_MM_ROWS, n0:n0 + tk] = f
    for r0 in range(0, rows, sub):
        f = o_ref[0, r0:r0 + sub, :]
        o_ref[0, r0:r0 + sub, :] = x_ref[0, r0:r0 + sub, :] + gate_ref[0] * _rms(f, pg_ref[...])


def _ffn(x3, pre_g, shift, scale, w_up3, conv3, w_down3, post_g, gate, *, rows, width, grid2d):
    nb, seq, d = x3.shape
    tk = FFN_TK
    ks = D_FF // tk
    n_tiles = seq // rows
    tile_rows = max(width, 128)
    halo = FFN_HALO if grid2d else 0
    assert seq % rows == 0 and rows % 512 == 0 and tile_rows % width == 0 and (not grid2d or width == FFN_HALO)
    per_block = shift.shape[0] != 1
    mod_map = (lambda i: (i // n_tiles, 0, 0)) if per_block else (lambda i: (0, 0, 0))
    vec = lambda: pl.BlockSpec((1, d), lambda i: (0, 0))
    mod = lambda: pl.BlockSpec((1, 1, d), mod_map)
    resident = lambda shape: pl.BlockSpec(shape, lambda i: (0,) * len(shape), pipeline_mode=pl.Buffered(1))
    in_specs = [pl.BlockSpec((1, rows, d), lambda i: (i // n_tiles, i % n_tiles, 0))]
    args = [x3]
    if grid2d:
        per = rows // halo
        last = seq // halo - 1
        in_specs += [pl.BlockSpec((1, halo, d), lambda i: (i // n_tiles, jnp.maximum((i % n_tiles) * per - 1, 0), 0)),
                     pl.BlockSpec((1, halo, d), lambda i: (i // n_tiles, jnp.minimum((i % n_tiles + 1) * per, last), 0))]
        args += [x3, x3]
    in_specs += [vec(), mod(), mod(),
                 resident((2 * ks, d, tk)), resident((ks, 9, tk)), resident((ks, tk, d)),
                 vec(), mod()]
    args += [pre_g.reshape(1, d), shift, scale, w_up3, conv3, w_down3, post_g.reshape(1, d), gate]
    return pl.pallas_call(
        functools.partial(_ffn_kernel, rows=rows, tile_rows=tile_rows, width=width, grid2d=grid2d, n_tiles=n_tiles),
        grid=(nb * n_tiles,),
        in_specs=in_specs,
        out_specs=pl.BlockSpec((1, rows, d), lambda i: (i // n_tiles, i % n_tiles, 0)),
        out_shape=jax.ShapeDtypeStruct((nb, seq, d), F32),
        scratch_shapes=[pltpu.VMEM((rows + 2 * halo, d), BF16),
                        pltpu.VMEM((rows + 2 * halo + 2 * FFN_PAD, tk), F32),
                        pltpu.VMEM((rows + 2 * halo + 2 * FFN_PAD, tk), F32),
                        pltpu.VMEM((ks, rows, tk), BF16)],
        compiler_params=_cparams(("arbitrary",), 56),
        name="ffn",
    )(*args)


RET_CHUNK = 256


def _ret_kernel(k_ref, v_ref, q_ref, kc_ref, vc_ref, dec_ref, r_ref, rf_ref, rb_ref, *, seq, ctx_len):
    T = RET_CHUNK
    nc = seq // T
    t_col = lax.broadcasted_iota(jnp.int32, (T, 1), 0).astype(F32)
    lg = [_log_sigmoid(dec_ref[0, direction][0:1, 0:1]) for direction in range(2)]
    scale = C_QK_DIM ** -0.5
    zeta = (jnp.exp(lg[0] * (T - 1 - t_col)) * scale, jnp.exp(lg[1] * t_col) * scale)
    xi = (jnp.exp(lg[0] * (t_col + 1.0)), jnp.exp(lg[1] * (T - t_col)))
    gch = (jnp.exp(lg[0] * T), jnp.exp(lg[1] * T))
    row = lax.broadcasted_iota(jnp.int32, (T, T), 0)
    col = lax.broadcasted_iota(jnp.int32, (T, T), 1)
    rel = (row - col).astype(F32)
    d_intra = (jnp.where(row >= col, jnp.exp(lg[0] * rel), 0.0)
               + jnp.where(col >= row, jnp.exp(-lg[1] * rel), 0.0)) * scale

    def absorb(state_ref, direction, k, v, first):
        upd = _dot_tn((k.astype(F32) * zeta[direction]).astype(BF16), v)
        state_ref[...] = upd if first else gch[direction] * state_ref[...] + upd

    ncc = ctx_len // T
    for direction, state_ref in ((0, rf_ref), (1, rb_ref)):
        order = list(range(ncc)) if direction == 0 else list(range(ncc - 1, -1, -1))
        for n, c in enumerate(order):
            absorb(state_ref, direction, kc_ref[0, c * T:(c + 1) * T, :], vc_ref[0, c * T:(c + 1) * T, :], n == 0)

    def scan(i, first_visit):
        slf = pl.ds(pl.multiple_of(i * T, T), T)
        slb = pl.ds(pl.multiple_of((nc - 1 - i) * T, T), T)
        q, k, v = q_ref[0, slf, :], k_ref[0, slf, :], v_ref[0, slf, :]
        scores = _dot_nt(q, k) * d_intra
        fwd = _dot(scores.astype(BF16), v) + xi[0] * _dot(q, rf_ref[...].astype(BF16))
        absorb(rf_ref, 0, k, v, False)
        qb, kb, vb = q_ref[0, slb, :], k_ref[0, slb, :], v_ref[0, slb, :]
        bwd = xi[1] * _dot(qb, rb_ref[...].astype(BF16))
        absorb(rb_ref, 1, kb, vb, False)
        if first_visit:
            r_ref[0, slf, :] = fwd
            r_ref[0, slb, :] = bwd
        else:
            r_ref[0, slf, :] += fwd
            r_ref[0, slb, :] += bwd

    assert nc % 2 == 0

    def first_half(i, carry):
        scan(i, True)
        return carry

    def second_half(i, carry):
        scan(i, False)
        return carry

    lax.fori_loop(0, nc // 2, first_half, 0, unroll=2)
    lax.fori_loop(nc // 2, nc, second_half, 0, unroll=2)


def _ret(p, p_ctx, dec_b):
    b, seq, _ = p.shape
    ctx_len = p_ctx.shape[1]
    assert seq % RET_CHUNK == 0 and ctx_len % RET_CHUNK == 0
    return pl.pallas_call(
        functools.partial(_ret_kernel, seq=seq, ctx_len=ctx_len),
        grid=(b, C_HEADS),
        in_specs=[pl.BlockSpec((1, seq, C_QK_DIM), lambda i, h: (i, 0, h)),
                  pl.BlockSpec((1, seq, C_V_DIM), lambda i, h: (i, 0, 2 + h)),
                  pl.BlockSpec((1, seq, C_QK_DIM), lambda i, h: (i, 0, 12 + h)),
                  pl.BlockSpec((1, ctx_len, C_QK_DIM), lambda i, h: (i, 0, h)),
                  pl.BlockSpec((1, ctx_len, C_V_DIM), lambda i, h: (i, 0, 2 + h)),
                  pl.BlockSpec((1, 2, 8, LANES), lambda i, h: (h, 0, 0, 0))],
        out_specs=pl.BlockSpec((1, seq, C_V_DIM), lambda i, h: (i, 0, h)),
        out_shape=jax.ShapeDtypeStruct((b, seq, C_HEADS * C_V_DIM), F32),
        scratch_shapes=[pltpu.VMEM((C_QK_DIM, C_V_DIM), F32), pltpu.VMEM((C_QK_DIM, C_V_DIM), F32)],
        compiler_params=_cparams(("arbitrary", "arbitrary"), 48),
        name="ret",
    )(p, p, p, p_ctx, p_ctx, dec_b)


def _merge1_kernel(r_ref, g_ref, x_ref, hg_ref, wo_ref, pg_ref, gate_ref, out_ref, z_ref):
    for h in range(C_HEADS):
        hs = slice(h * C_V_DIM, (h + 1) * C_V_DIM)
        z = _silu(g_ref[0, :, hs].astype(F32)) * (_ln(r_ref[0, :, hs]) * hg_ref[:, hs])
        z_ref[:, hs] = z.astype(BF16)
    y = _dot(z_ref[...], wo_ref[...])
    out_ref[0] = x_ref[0] + gate_ref[0] * _rms(y, pg_ref[...])


def _merge1(r, p, x, head_g, w_out, post_g, gate, tm):
    nb = x.shape[0]
    d = D_MODEL
    hv = C_HEADS * C_V_DIM
    blocks_per_gate = nb // gate.shape[0]
    return pl.pallas_call(
        _merge1_kernel,
        grid=(nb,),
        in_specs=[pl.BlockSpec((1, tm, hv), lambda i: (i, 0, 0)),
                  pl.BlockSpec((1, tm, hv), lambda i: (i, 0, 2)),
                  pl.BlockSpec((1, tm, d), lambda i: (i, 0, 0)),
                  pl.BlockSpec((1, hv), lambda i: (0, 0)),
                  pl.BlockSpec((hv, d), lambda i: (0, 0)),
                  pl.BlockSpec((1, d), lambda i: (0, 0)),
                  pl.BlockSpec((1, 1, d), lambda i: (i // blocks_per_gate, 0, 0))],
        out_specs=pl.BlockSpec((1, tm, d), lambda i: (i, 0, 0)),
        out_shape=jax.ShapeDtypeStruct((nb, tm, d), F32),
        scratch_shapes=[pltpu.VMEM((tm, hv), BF16)],
        compiler_params=_cparams(("arbitrary",), 48),
        name="merge1",
    )(r, p, x, head_g, w_out, post_g.reshape(1, d), gate)


def kernel(x, c, ctx, c_ctx, ada_w, ada_b, pre_g, post_g, ffn_up, ffn_conv, ffn_down,
           ab_w_in, ab_qk_conv, ab_gate_b, ab_sgu_w, ab_sgu_b, ab_head_g, ab_w_out,
           ret_w_in, ret_decay, ret_head_g, ret_w_out):
    b, seq, d = x.shape
    ctx_len = ctx.shape[1]
    depth = ada_w.shape[0]
    tm = 512
    ctx_rows = min(2048, b * ctx_len)

    n_cond = b + 1
    n_pad = -n_cond % 8
    cs = jnp.concatenate([c, c_ctx[None, :], jnp.zeros((n_pad, d), F32)], axis=0)
    mods = _adaln(cs, ada_w, ada_b)

    def mod_l(layer, i):
        return mods[layer, :b, i * d:(i + 1) * d].reshape(b, 1, d)

    def mod_c(layer, i):
        return mods[layer, b:b + 1, i * d:(i + 1) * d].reshape(1, 1, d)

    n_slab = D_FF // FFN_TK
    ffn_up3 = ffn_up.astype(BF16).reshape(depth, d, 2 * n_slab, FFN_TK).transpose(0, 2, 1, 3)
    ffn_conv3 = ffn_conv.reshape(depth, 9, n_slab, FFN_TK).transpose(0, 2, 1, 3)
    ffn_down3 = (0.5 * ffn_down).astype(BF16).reshape(depth, n_slab, FFN_TK, d)

    def ffn_l(layer, xl):
        return _ffn(xl, pre_g[layer, 1], mod_l(layer, 3), mod_l(layer, 4), ffn_up3[layer], ffn_conv3[layer],
                    ffn_down3[layer], post_g[layer, 1], mod_l(layer, 5), rows=1024, width=GRID_W, grid2d=True)

    def ffn_c(layer, xc):
        out = _ffn(xc.reshape(1, b * ctx_len, d), pre_g[layer, 1], mod_c(layer, 3), mod_c(layer, 4), ffn_up3[layer],
                   ffn_conv3[layer], ffn_down3[layer], post_g[layer, 1], mod_c(layer, 5),
                   rows=min(1024, b * ctx_len), width=ctx_len, grid2d=False)
        return out.reshape(xc.shape)

    w_in = ab_w_in[0]
    g_lo = 2 * B_DIM
    g_hi = g_lo + N_GATES * B_HEADS
    w_main = jnp.concatenate([w_in[:, :g_lo], w_in[:, g_hi:]], axis=1).astype(BF16)
    w_gates = w_in[:, g_lo:g_hi].reshape(d, N_GATES, B_HEADS)
    lane_pad = ((0, 0), (0, LANES - N_CHAINS))
    w_gate = jnp.concatenate([jnp.pad(w_gates[:, 0::2].reshape(d, N_CHAINS), lane_pad),
                              jnp.pad(w_gates[:, 1::2].reshape(d, N_CHAINS), lane_pad)], axis=1).astype(BF16)
    bias_row = jnp.concatenate([jnp.pad(ab_gate_b[0][0::2].reshape(1, N_CHAINS), lane_pad),
                                jnp.pad(ab_gate_b[0][1::2].reshape(1, N_CHAINS), lane_pad)], axis=1)
    wq = ab_qk_conv[0][:, :B_DIM]
    wk = ab_qk_conv[0][:, B_DIM:]
    head_g0 = ab_head_g[0].reshape(1, B_DIM)
    sgu_w = ab_sgu_w[0].astype(BF16)
    sgu_bb = jnp.broadcast_to(ab_sgu_b[0][:, :, None], (A_GROUPS, CHUNK, CHUNK))
    w_out0 = ab_w_out[0].astype(BF16)

    p_c, gates_c = _proj(ctx.reshape(-1, ctx_rows, d), pre_g[0, 0], mod_c(0, 0), mod_c(0, 1), w_main, w_gate)
    p_l, gates_l = _proj(x, pre_g[0, 0], mod_l(0, 0), mod_l(0, 1), w_main, w_gate)
    p_c = p_c.reshape(b, ctx_len, -1)
    gates_c = gates_c.reshape(b, ctx_len, GATE_LANES)
    zero_state = (jnp.zeros((b, N_CHAINS, B_HEAD_DIM, 2 * B_HEAD_DIM), F32),
                  jnp.zeros((b, N_CHAINS, 1, B_HEAD_DIM), F32))
    mem_c, *ctx_state = _mlstm(p_c, gates_c, wq, wk, bias_row, head_g0, zero_state)
    mem_l, *_ = _mlstm(p_l, gates_l, wq, wk, bias_row, head_g0, tuple(ctx_state))

    rt = lambda t: t.reshape(-1, tm, t.shape[-1])
    x = _merge0(rt(p_l), rt(mem_l), rt(x), sgu_w, sgu_bb, w_out0, post_g[0, 0], mod_l(0, 2), tm).reshape(b, seq, d)
    ctx = _merge0(rt(p_c), rt(mem_c), rt(ctx), sgu_w, sgu_bb, w_out0, post_g[0, 0], mod_c(0, 2), tm).reshape(b, ctx_len, d)
    x = ffn_l(0, x)
    ctx = ffn_c(0, ctx)

    w_ret = ret_w_in[0].astype(BF16)
    kv_cols = C_HEADS * (C_QK_DIM + C_V_DIM)
    dec_b = jnp.broadcast_to(ret_decay[0].T[:, :, None, None], (C_HEADS, 2, 8, LANES))
    p_c = _proj(ctx.reshape(-1, ctx_rows, d), pre_g[1, 0], mod_c(1, 0), mod_c(1, 1), w_ret[:, :kv_cols])
    p_l = _proj(x, pre_g[1, 0], mod_l(1, 0), mod_l(1, 1), w_ret)
    r = _ret(p_l, p_c.reshape(b, ctx_len, -1), dec_b)
    x = _merge1(rt(r), rt(p_l), rt(x), ret_head_g[0].reshape(1, -1), ret_w_out[0].astype(BF16),
                post_g[1, 0], mod_l(1, 2), tm).reshape(b, seq, d)
    x = ffn_l(1, x)
    return x
```

```python
import functools

import jax
import jax.numpy as jnp
from jax import lax
from jax.experimental import pallas as pl
from jax.experimental.pallas import tpu as pltpu

F32 = jnp.float32
BF16 = jnp.bfloat16
HIGHEST = lax.Precision.HIGHEST

D_MODEL = 1024
GRID_W = 64
CHUNK = 128
EPS = 1e-6
N_MOD = 6
A_GROUPS = 4
A_DIM = 512
B_HEADS = 4
B_HEAD_DIM = 128
B_DIM = 512
N_GATES = 4
C_HEADS = 4
C_QK_DIM = 256
C_V_DIM = 512
D_FF = 2816

LANES = 128
GATE_LANES = 2 * LANES
N_CHAINS = 2 * B_HEADS
ML_CHUNK = 256
V7X_VMEM_BYTES = 64 * 1024 * 1024
MIB = 1024 * 1024


def _cparams(semantics, vmem_mib):
    assert vmem_mib * MIB < V7X_VMEM_BYTES
    return pltpu.CompilerParams(dimension_semantics=semantics, vmem_limit_bytes=vmem_mib * MIB)


def _gelu(x):
    return x * (0.5 * (1.0 + jnp.tanh(0.7978845608028654 * (x + 0.044715 * (x * x * x)))))


def _silu(x):
    return x * jax.nn.sigmoid(x)


def _log_sigmoid(x):
    return jnp.minimum(x, 0.0) - jnp.log1p(jnp.exp(-jnp.abs(x)))


def _rms(x, g):
    return x * lax.rsqrt(jnp.mean(x * x, axis=-1, keepdims=True) + EPS) * g


def _ln(x):
    xc = x - jnp.mean(x, axis=-1, keepdims=True)
    return xc * lax.rsqrt(jnp.mean(xc * xc, axis=-1, keepdims=True) + EPS)


def _dot(a, b):
    return jnp.dot(a, b, preferred_element_type=F32)


def _dot_f32(a, b):
    return jnp.dot(a, b, precision=HIGHEST, preferred_element_type=F32)


def _dot_nt(a, b):
    return lax.dot_general(a, b, (((1,), (1,)), ((), ())), preferred_element_type=F32)


def _dot_tn(a, b):
    return lax.dot_general(a, b, (((0,), (0,)), ((), ())), preferred_element_type=F32)


def _adaln_kernel(c_ref, w_ref, b_ref, o_ref):
    s = _silu(c_ref[...])
    o_ref[0] = _dot(s.astype(BF16), w_ref[0].astype(BF16)) + b_ref[0]


def _adaln(cs, ada_w, ada_b):
    depth, d, n = ada_w.shape
    mp = cs.shape[0]
    tn = 1024
    return pl.pallas_call(
        _adaln_kernel,
        grid=(depth, n // tn),
        in_specs=[pl.BlockSpec((mp, d), lambda l, j: (0, 0)),
                  pl.BlockSpec((1, d, tn), lambda l, j: (l, 0, j)),
                  pl.BlockSpec((1, 1, tn), lambda l, j: (l, 0, j))],
        out_specs=pl.BlockSpec((1, mp, tn), lambda l, j: (l, 0, j)),
        out_shape=jax.ShapeDtypeStruct((depth, mp, n), F32),
        compiler_params=_cparams(("arbitrary", "arbitrary"), 32),
        name="adaln",
    )(cs, ada_w, ada_b.reshape(depth, 1, n))


def _proj_kernel(x_ref, g_ref, sh_ref, sc_ref, w_ref, *rest, rows, sub, with_gates):
    if with_gates:
        wg_ref, p_ref, gate_ref, h_ref = rest
    else:
        p_ref, h_ref = rest

    @pl.when(pl.program_id(1) == 0)
    def _():
        for r0 in range(0, rows, sub):
            h = _rms(x_ref[0, r0:r0 + sub, :], g_ref[...]) * (1.0 + sc_ref[0]) + sh_ref[0]
            hb = h.astype(BF16)
            h_ref[r0:r0 + sub, :] = hb
            if with_gates:
                gate_ref[0, r0:r0 + sub, :] = _dot(hb, wg_ref[...])
            p_ref[0, r0:r0 + sub, :] = _dot(hb, w_ref[...]).astype(BF16)

    @pl.when(pl.program_id(1) != 0)
    def _():
        tn = w_ref.shape[1]
        mm_rows = min(rows, 1024)
        for n0 in range(0, tn, 256):
            for r0 in range(0, rows, mm_rows):
                p_ref[0, r0:r0 + mm_rows, n0:n0 + 256] = _dot(h_ref[r0:r0 + mm_rows, :],
                                                              w_ref[:, n0:n0 + 256]).astype(BF16)


def _proj(x3, pre_g, shift, scale, w, wg=None):
    nb, rows, d = x3.shape
    n = w.shape[1]
    tn = 1024
    per_block = shift.shape[0] != 1
    mod_map = (lambda i, j: (i, 0, 0)) if per_block else (lambda i, j: (0, 0, 0))
    in_specs = [pl.BlockSpec((1, rows, d), lambda i, j: (i, 0, 0)),
                pl.BlockSpec((1, d), lambda i, j: (0, 0)),
                pl.BlockSpec((1, 1, d), mod_map),
                pl.BlockSpec((1, 1, d), mod_map),
                pl.BlockSpec((d, tn), lambda i, j: (0, j))]
    out_specs = [pl.BlockSpec((1, rows, tn), lambda i, j: (i, 0, j))]
    out_shape = [jax.ShapeDtypeStruct((nb, rows, n), BF16)]
    args = [x3, pre_g.reshape(1, d), shift, scale, w]
    if wg is not None:
        in_specs.append(pl.BlockSpec((d, GATE_LANES), lambda i, j: (0, 0)))
        out_specs.append(pl.BlockSpec((1, rows, GATE_LANES), lambda i, j: (i, 0, 0)))
        out_shape.append(jax.ShapeDtypeStruct((nb, rows, GATE_LANES), F32))
        args.append(wg)
    outs = pl.pallas_call(
        functools.partial(_proj_kernel, rows=rows, sub=min(512, rows), with_gates=wg is not None),
        grid=(nb, n // tn),
        in_specs=in_specs, out_specs=out_specs, out_shape=out_shape,
        scratch_shapes=[pltpu.VMEM((rows, d), BF16)],
        compiler_params=_cparams(("arbitrary", "arbitrary"), 48),
        name="proj",
    )(*args)
    return outs if wg is not None else outs[0]


def _mlstm_kernel(k_ref, v_ref, q_ref, o_ref, gates_ref, wq_ref, wk_ref, bias_ref, hg_ref, c0_ref, m0_ref,
                  mem_ref, cf_ref, mf_ref,
                  pm_ref, cum_ref, brow_ref, kt_ref, qc_ref, rf_ref, rb_ref, st_ref, *, seq):
    T = ML_CHUNK
    hd = B_HEAD_DIM
    nc = seq // T
    row = lax.broadcasted_iota(jnp.int32, (T, T), 0)
    col = lax.broadcasted_iota(jnp.int32, (T, T), 1)
    lower = row >= col
    upper = col >= row
    fwd_lane = lax.broadcasted_iota(jnp.int32, (T, LANES), 1) < B_HEADS
    trow = lax.broadcasted_iota(jnp.int32, (T, LANES), 0)

    def gate_body(c, carry):
        sl = pl.ds(pl.multiple_of(c * T, T), T)
        g = gates_ref[0, sl, :] + bias_ref[...]
        lf = _log_sigmoid(g[:, LANES:])
        prefix = _dot_f32(lower.astype(F32), lf)
        cum = jnp.where(fwd_lane, prefix, prefix[T - 1:T] - prefix + lf)
        b = g[:, :LANES] - cum
        pf = b
        ps = b
        for sh in (1 << e for e in range(T.bit_length() - 1)):
            pf = jnp.where(trow >= sh, jnp.maximum(pf, pltpu.roll(pf, sh, 0)), pf)
            ps = jnp.where(trow < T - sh, jnp.maximum(ps, pltpu.roll(ps, T - sh, 0)), ps)
        pm_ref[sl, :] = jnp.where(fwd_lane, pf, ps)
        cum_ref[sl, :] = cum
        brow_ref[c] = b.T[:N_CHAINS]
        return carry

    lax.fori_loop(0, nc, gate_body, 0)

    ridx = lax.broadcasted_iota(jnp.int32, (T, hd), 0)

    def conv_body(c, carry):
        start = pl.multiple_of(c * T, T)
        sl = pl.ds(start, T)
        prev_start = pl.multiple_of(jnp.maximum(start - 16, 0), 16)
        next_start = pl.multiple_of(jnp.minimum(start + T, seq - 16), 16)
        has_prev = jnp.where(c > 0, 1.0, 0.0)
        has_next = jnp.where(c < nc - 1, 1.0, 0.0)
        for h in range(B_HEADS):
            hs = slice(h * hd, (h + 1) * hd)
            for src_ref, w_ref, is_k in ((k_ref, wk_ref, True), (q_ref, wq_ref, False)):
                cur = src_ref[0, sl, hs].astype(F32)
                prev_row = src_ref[0, pl.ds(prev_start, 16), hs].astype(F32)[15:16] * has_prev
                next_row = src_ref[0, pl.ds(next_start, 16), hs].astype(F32)[0:1] * has_next
                w = w_ref[:, hs]
                xm = jnp.where(ridx == 0, prev_row, pltpu.roll(cur, 1, 0))
                xp = jnp.where(ridx == T - 1, next_row, pltpu.roll(cur, T - 1, 0))
                y = _silu(xm * w[0:1] + cur * w[1:2] + xp * w[2:3])
                if is_k:
                    kt_ref[h, c] = (y * hd ** -0.5).T.astype(BF16)
                else:
                    qc_ref[sl, hs] = y.astype(BF16)
        return carry

    lax.fori_loop(0, nc, conv_body, 0)

    ones_tile = (lax.broadcasted_iota(jnp.int32, (T, hd), 1) == 0).astype(BF16)
    for j in range(N_CHAINS):
        st_ref[j] = c0_ref[0, j]

    def scan_body(i, ms):
        chains = range(N_CHAINS)
        heads = [j % B_HEADS for j in chains]
        cs = [i if j < B_HEADS else nc - 1 - i for j in chains]
        sls = [pl.ds(pl.multiple_of(cs[j] * T, T), T) for j in chains]
        hss = [slice(h * hd, (h + 1) * hd) for h in heads]
        edge = [cs[j] * T + (T - 1 if j < B_HEADS else 0) for j in chains]
        b_row = [brow_ref[cs[j], j:j + 1, :] for j in chains]
        m_col = [jnp.maximum(ms[j], pm_ref[sls[j], j:j + 1]) for j in chains]
        m_wide = [jnp.broadcast_to(m_col[j], (T, T)) for j in chains]
        q = [qc_ref[sls[j], hss[j]] for j in chains]
        kt = [kt_ref[heads[j], cs[j]] for j in chains]
        dmat = [jnp.where(lower if j < B_HEADS else upper, jnp.exp(b_row[j] - m_wide[j]), 0.0) for j in chains]
        p = [(_dot(q[j], kt[j]) * dmat[j]).astype(BF16) for j in chains]
        v_aug = [jnp.concatenate([v_ref[0, sls[j], hss[j]], ones_tile], axis=1) for j in chains]
        st = [st_ref[j] for j in chains]
        qs = [(q[j].astype(F32) * jnp.exp(ms[j] - m_wide[j][:, :hd])).astype(BF16) for j in chains]
        acc = [_dot(p[j], v_aug[j]) + _dot(qs[j], st[j].astype(BF16)) for j in chains]
        for j in chains:
            out_ref = rf_ref if j < B_HEADS else rb_ref
            den = acc[j][:, hd:hd + 1]
            floor = jnp.exp(-(cum_ref[sls[j], j:j + 1] + m_col[j]))
            out_ref[sls[j], hss[j]] = acc[j][:, :hd] / jnp.maximum(jnp.abs(den), floor)
        new_ms = []
        for j in chains:
            btot = cum_ref[pl.ds(edge[j], 1), j:j + 1]
            m_far = jnp.maximum(ms[j], pm_ref[pl.ds(edge[j], 1), j:j + 1])
            m_new = btot + m_far
            kw_t = (kt[j].astype(F32) * jnp.exp(btot + b_row[j] - m_new)).astype(BF16)
            st_ref[j] = jnp.exp(ms[j] - m_far) * st[j] + _dot(kw_t, v_aug[j])
            new_ms.append(m_new)
        return tuple(new_ms)

    ms = lax.fori_loop(0, nc, scan_body, tuple(m0_ref[0, j][:, 0:1] for j in range(N_CHAINS)), unroll=4)
    for j in range(N_CHAINS):
        cf_ref[0, j] = st_ref[j]
        mf_ref[0, j] = jnp.broadcast_to(ms[j], (1, hd))

    def out_body(c, carry):
        sl = pl.ds(pl.multiple_of(c * T, T), T)
        for h in range(B_HEADS):
            hs = slice(h * hd, (h + 1) * hd)
            y = jax.nn.sigmoid(o_ref[0, sl, hs].astype(F32)) * (rf_ref[sl, hs] + rb_ref[sl, hs])
            mem_ref[0, sl, hs] = (_ln(y) * hg_ref[:, hs]).astype(BF16)
        return carry

    lax.fori_loop(0, nc, out_body, 0)


def _mlstm(p, gates, wq, wk, bias_row, head_g, state):
    b, seq, _ = p.shape
    c0, m0 = state
    hd = B_HEAD_DIM
    assert seq % ML_CHUNK == 0
    nc = seq // ML_CHUNK
    sec = lambda idx: pl.BlockSpec((1, seq, B_DIM), lambda i, idx=idx: (i, 0, idx))
    full = lambda shape: pl.BlockSpec(shape, lambda i: (0,) * len(shape))
    st_c = pl.BlockSpec((1, N_CHAINS, hd, 2 * hd), lambda i: (i, 0, 0, 0))
    st_m = pl.BlockSpec((1, N_CHAINS, 1, hd), lambda i: (i, 0, 0, 0))
    return pl.pallas_call(
        functools.partial(_mlstm_kernel, seq=seq),
        grid=(b,),
        in_specs=[sec(0), sec(1), sec(2), sec(3),
                  pl.BlockSpec((1, seq, GATE_LANES), lambda i: (i, 0, 0)),
                  full((3, B_DIM)), full((3, B_DIM)), full((1, GATE_LANES)), full((1, B_DIM)),
                  st_c, st_m],
        out_specs=[pl.BlockSpec((1, seq, B_DIM), lambda i: (i, 0, 0)), st_c, st_m],
        out_shape=[jax.ShapeDtypeStruct((b, seq, B_DIM), BF16),
                   jax.ShapeDtypeStruct(c0.shape, F32),
                   jax.ShapeDtypeStruct(m0.shape, F32)],
        scratch_shapes=[pltpu.VMEM((seq, LANES), F32), pltpu.VMEM((seq, LANES), F32),
                        pltpu.VMEM((nc, N_CHAINS, ML_CHUNK), F32),
                        pltpu.VMEM((B_HEADS, nc, hd, ML_CHUNK), BF16), pltpu.VMEM((seq, B_DIM), BF16),
                        pltpu.VMEM((seq, B_DIM), F32), pltpu.VMEM((seq, B_DIM), F32),
                        pltpu.VMEM((N_CHAINS, hd, 2 * hd), F32)],
        compiler_params=_cparams(("arbitrary",), 56),
        name="mlstm",
    )(p, p, p, p, gates, wq, wk, bias_row, head_g, c0, m0)


def _merge0_kernel(u_ref, va_ref, mem_ref, x_ref, sw_ref, sb_ref, wo_ref, pg_ref, gate_ref, out_ref, cat_ref, *, tm):
    T = CHUNK
    gd = A_DIM // A_GROUPS
    for c in range(tm // T):
        sl = slice(c * T, (c + 1) * T)
        vn = _ln(_gelu(va_ref[0, sl, :].astype(F32))).astype(BF16)
        u = _gelu(u_ref[0, sl, :].astype(F32))
        for g in range(A_GROUPS):
            gs = slice(g * gd, (g + 1) * gd)
            mixed = _dot(sw_ref[g], vn[:, gs]) + sb_ref[g]
            cat_ref[sl, gs] = (u[:, gs] * mixed).astype(BF16)
    cat_ref[:, A_DIM:] = mem_ref[0]
    y = _dot(cat_ref[...], wo_ref[...])
    out_ref[0] = x_ref[0] + gate_ref[0] * _rms(y, pg_ref[...])


def _merge0(p, mem, x, sgu_w, sgu_bb, w_out, post_g, gate, tm):
    nb = x.shape[0]
    d = D_MODEL
    per_block = gate.shape[0] != 1
    blocks_per_gate = nb // gate.shape[0]
    gate_map = (lambda i: (i // blocks_per_gate, 0, 0)) if per_block else (lambda i: (0, 0, 0))
    return pl.pallas_call(
        functools.partial(_merge0_kernel, tm=tm),
        grid=(nb,),
        in_specs=[pl.BlockSpec((1, tm, A_DIM), lambda i: (i, 0, 4)),
                  pl.BlockSpec((1, tm, A_DIM), lambda i: (i, 0, 5)),
                  pl.BlockSpec((1, tm, B_DIM), lambda i: (i, 0, 0)),
                  pl.BlockSpec((1, tm, d), lambda i: (i, 0, 0)),
                  pl.BlockSpec((A_GROUPS, CHUNK, CHUNK), lambda i: (0, 0, 0)),
                  pl.BlockSpec((A_GROUPS, CHUNK, CHUNK), lambda i: (0, 0, 0)),
                  pl.BlockSpec((A_DIM + B_DIM, d), lambda i: (0, 0)),
                  pl.BlockSpec((1, d), lambda i: (0, 0)),
                  pl.BlockSpec((1, 1, d), gate_map)],
        out_specs=pl.BlockSpec((1, tm, d), lambda i: (i, 0, 0)),
        out_shape=jax.ShapeDtypeStruct((nb, tm, d), F32),
        scratch_shapes=[pltpu.VMEM((tm, A_DIM + B_DIM), BF16)],
        compiler_params=_cparams(("arbitrary",), 32),
        name="merge0",
    )(p, p, mem, x, sgu_w, sgu_bb, w_out, post_g.reshape(1, d), gate)


FFN_TK = 256
FFN_HALO = 64
FFN_PAD = 8
FFN_MM_ROWS = 512


def _ffn_kernel(*refs, rows, tile_rows, width, grid2d, n_tiles):
    if grid2d:
        (x_ref, xu_ref, xd_ref, g_ref, sh_ref, sc_ref, wup_ref, cw_ref, wd_ref, pg_ref, gate_ref,
         o_ref, h_ref, gs0_ref, gs1_ref, act_ref) = refs
    else:
        (x_ref, g_ref, sh_ref, sc_ref, wup_ref, cw_ref, wd_ref, pg_ref, gate_ref,
         o_ref, h_ref, gs0_ref, gs1_ref, act_ref) = refs
    halo = FFN_HALO if grid2d else 0
    tk = FFN_TK
    ks = D_FF // tk
    hrows = rows + 2 * halo
    sub = 512

    def normmod(xv):
        return _rms(xv, g_ref[...]) * (1.0 + sc_ref[0]) + sh_ref[0]

    for gs_ref in (gs0_ref, gs1_ref):
        gs_ref[0:FFN_PAD, :] = jnp.zeros((FFN_PAD, tk), F32)
        gs_ref[FFN_PAD + hrows:2 * FFN_PAD + hrows, :] = jnp.zeros((FFN_PAD, tk), F32)

    g_step = hrows // (hrows // FFN_MM_ROWS)
    g_chunks = [(r0, r0 + g_step) for r0 in range(0, hrows, g_step)]

    def gate_chunk(slab, gs_ref, r0, r1):
        gs_ref[FFN_PAD + r0:FFN_PAD + r1, :] = _dot(h_ref[r0:r1, :], wup_ref[ks + slab])

    def gate_branch(slab, gs_ref):
        for r0, r1 in g_chunks:
            gate_chunk(slab, gs_ref, r0, r1)

    pieces = [(halo + r0, halo + r0 + sub, lambda r0=r0: normmod(x_ref[0, r0:r0 + sub, :])) for r0 in range(0, rows, sub)]
    if grid2d:
        t = pl.program_id(0) % n_tiles
        pieces = ([(0, halo, lambda: normmod(xu_ref[0]) * (t > 0).astype(F32))] + pieces
                  + [(halo + rows, hrows, lambda: normmod(xd_ref[0]) * (t < n_tiles - 1).astype(F32))])
    pending = list(g_chunks)
    for p0, p1, piece in pieces:
        h_ref[p0:p1, :] = piece().astype(BF16)
        while pending and pending[0][1] <= p1:
            gate_chunk(0, gs0_ref, *pending.pop(0))
    assert not pending

    iota8 = lax.broadcasted_iota(jnp.int32, (8, tk), 0)

    def cut(vals, first):
        pieces = []
        for g0 in range(0, tile_rows, width):
            if first:
                pieces += [jnp.where(iota8 == 0, 0.0, vals[g0:g0 + 8]), vals[g0 + 8:g0 + width]]
            else:
                pieces += [vals[g0:g0 + width - 8], jnp.where(iota8 == 7, 0.0, vals[g0 + width - 8:g0 + width])]
        return jnp.concatenate(pieces, axis=0)

    def tiles(slab, gs_ref):
        cw = cw_ref[slab]
        for base in range(0, rows, tile_rows):
            if base % FFN_MM_ROWS == 0:
                a_rows = _dot(h_ref[halo + base:halo + base + FFN_MM_ROWS, :], wup_ref[slab])
            a = a_rows[base % FFN_MM_ROWS:base % FFN_MM_ROWS + tile_rows]
            dys = (0, 1, 2) if grid2d else (1,)
            g_rows = [gs_ref[FFN_PAD + halo + base + (dy - 1) * halo:FFN_PAD + halo + base + (dy - 1) * halo + tile_rows, :]
                      for dy in dys]
            left, mid, right = (functools.reduce(lambda s, t: s + t,
                                                 [cw[3 * dy + dx:3 * dy + dx + 1] * g for dy, g in zip(dys, g_rows)])
                                for dx in range(3))
            conv = (mid + cut(pltpu.roll(left, 1, 0), True) + cut(pltpu.roll(right, tile_rows - 1, 0), False))
            t = jnp.tanh(conv * (0.7978845608028654 + (0.7978845608028654 * 0.044715) * (conv * conv)))
            p = a * conv
            act_ref[slab, base:base + tile_rows, :] = (p + p * t).astype(BF16)

    assert ks % 2 == 1

    def body(j, carry):
        k0 = 2 * j
        gate_branch(k0 + 1, gs1_ref)
        tiles(k0, gs0_ref)
        gate_branch(k0 + 2, gs0_ref)
        tiles(k0 + 1, gs1_ref)
        return carry

    lax.fori_loop(0, ks // 2, body, 0)
    tiles(ks - 1, gs0_ref)

    for r0 in range(0, rows, FFN_MM_ROWS):
        for n0 in range(0, D_MODEL, tk):
            f = _dot(act_ref[0, r0:r0 + FFN_MM_ROWS, :], wd_ref[0, :, n0:n0 + tk])
            for k in range(1, ks):
                f = f + _dot(act_ref[k, r0:r0 + FFN_MM_ROWS, :], wd_ref[k, :, n0:n0 + tk])
            o_ref[0, r0:r0 + FFN_MM_ROWS, n0:n0 + tk] = f
        f = o_ref[0, r0:r0 + FFN_MM_ROWS, :]
        o_ref[0, r0:r0 + FFN_MM_ROWS, :] = (x_ref[0, r0:r0 + FFN_MM_ROWS, :]
                                            + gate_ref[0] * _rms(f, pg_ref[...]))


def _ffn(x3, pre_g, shift, scale, w_up3, conv3, w_down3, post_g, gate, *, rows, width, grid2d):
    nb, seq, d = x3.shape
    tk = FFN_TK
    ks = D_FF // tk
    n_tiles = seq // rows
    tile_rows = max(width, 128)
    halo = FFN_HALO if grid2d else 0
    assert seq % rows == 0 and rows % 512 == 0 and tile_rows % width == 0 and (not grid2d or width == FFN_HALO)
    per_block = shift.shape[0] != 1
    mod_map = (lambda i: (i // n_tiles, 0, 0)) if per_block else (lambda i: (0, 0, 0))
    vec = lambda: pl.BlockSpec((1, d), lambda i: (0, 0))
    mod = lambda: pl.BlockSpec((1, 1, d), mod_map)
    resident = lambda shape: pl.BlockSpec(shape, lambda i: (0,) * len(shape), pipeline_mode=pl.Buffered(1))
    in_specs = [pl.BlockSpec((1, rows, d), lambda i: (i // n_tiles, i % n_tiles, 0))]
    args = [x3]
    if grid2d:
        per = rows // halo
        last = seq // halo - 1
        in_specs += [pl.BlockSpec((1, halo, d), lambda i: (i // n_tiles, jnp.maximum((i % n_tiles) * per - 1, 0), 0)),
                     pl.BlockSpec((1, halo, d), lambda i: (i // n_tiles, jnp.minimum((i % n_tiles + 1) * per, last), 0))]
        args += [x3, x3]
    in_specs += [vec(), mod(), mod(),
                 resident((2 * ks, d, tk)), resident((ks, 9, tk)), resident((ks, tk, d)),
                 vec(), mod()]
    args += [pre_g.reshape(1, d), shift, scale, w_up3, conv3, w_down3, post_g.reshape(1, d), gate]
    return pl.pallas_call(
        functools.partial(_ffn_kernel, rows=rows, tile_rows=tile_rows, width=width, grid2d=grid2d, n_tiles=n_tiles),
        grid=(nb * n_tiles,),
        in_specs=in_specs,
        out_specs=pl.BlockSpec((1, rows, d), lambda i: (i // n_tiles, i % n_tiles, 0)),
        out_shape=jax.ShapeDtypeStruct((nb, seq, d), F32),
        scratch_shapes=[pltpu.VMEM((rows + 2 * halo, d), BF16),
                        pltpu.VMEM((rows + 2 * halo + 2 * FFN_PAD, tk), F32),
                        pltpu.VMEM((rows + 2 * halo + 2 * FFN_PAD, tk), F32),
                        pltpu.VMEM((ks, rows, tk), BF16)],
        compiler_params=_cparams(("arbitrary",), 56),
        name="ffn",
    )(*args)


RET_CHUNK = 256


def _ret_kernel(k_ref, v_ref, q_ref, kc_ref, vc_ref, dec_ref, r_ref, rf_ref, rb_ref, *, seq, ctx_len):
    T = RET_CHUNK
    nc = seq // T
    t_col = lax.broadcasted_iota(jnp.int32, (T, 1), 0).astype(F32)
    lg = [_log_sigmoid(dec_ref[0, direction][0:1, 0:1]) for direction in range(2)]
    scale = C_QK_DIM ** -0.5
    zeta = (jnp.exp(lg[0] * (T - 1 - t_col)) * scale, jnp.exp(lg[1] * t_col) * scale)
    xi = (jnp.exp(lg[0] * (t_col + 1.0)), jnp.exp(lg[1] * (T - t_col)))
    gch = (jnp.exp(lg[0] * T), jnp.exp(lg[1] * T))
    row = lax.broadcasted_iota(jnp.int32, (T, T), 0)
    col = lax.broadcasted_iota(jnp.int32, (T, T), 1)
    rel = (row - col).astype(F32)
    d_intra = (jnp.where(row >= col, jnp.exp(lg[0] * rel), 0.0)
               + jnp.where(col >= row, jnp.exp(-lg[1] * rel), 0.0)) * scale

    def absorb(state_ref, direction, k, v, first):
        upd = _dot_tn((k.astype(F32) * zeta[direction]).astype(BF16), v)
        state_ref[...] = upd if first else gch[direction] * state_ref[...] + upd

    ncc = ctx_len // T
    for direction, state_ref in ((0, rf_ref), (1, rb_ref)):
        order = list(range(ncc)) if direction == 0 else list(range(ncc - 1, -1, -1))
        for n, c in enumerate(order):
            absorb(state_ref, direction, kc_ref[0, c * T:(c + 1) * T, :], vc_ref[0, c * T:(c + 1) * T, :], n == 0)

    def scan(i, first_visit):
        slf = pl.ds(pl.multiple_of(i * T, T), T)
        slb = pl.ds(pl.multiple_of((nc - 1 - i) * T, T), T)
        q, k, v = q_ref[0, slf, :], k_ref[0, slf, :], v_ref[0, slf, :]
        scores = _dot_nt(q, k) * d_intra
        fwd = _dot(scores.astype(BF16), v) + xi[0] * _dot(q, rf_ref[...].astype(BF16))
        absorb(rf_ref, 0, k, v, False)
        qb, kb, vb = q_ref[0, slb, :], k_ref[0, slb, :], v_ref[0, slb, :]
        bwd = xi[1] * _dot(qb, rb_ref[...].astype(BF16))
        absorb(rb_ref, 1, kb, vb, False)
        if first_visit:
            r_ref[0, slf, :] = fwd
            r_ref[0, slb, :] = bwd
        else:
            r_ref[0, slf, :] += fwd
            r_ref[0, slb, :] += bwd

    assert nc % 2 == 0

    def first_half(i, carry):
        scan(i, True)
        return carry

    def second_half(i, carry):
        scan(i, False)
        return carry

    lax.fori_loop(0, nc // 2, first_half, 0, unroll=True)
    lax.fori_loop(nc // 2, nc, second_half, 0, unroll=True)


def _ret(p, p_ctx, dec_b):
    b, seq, _ = p.shape
    ctx_len = p_ctx.shape[1]
    assert seq % RET_CHUNK == 0 and ctx_len % RET_CHUNK == 0
    return pl.pallas_call(
        functools.partial(_ret_kernel, seq=seq, ctx_len=ctx_len),
        grid=(b, C_HEADS),
        in_specs=[pl.BlockSpec((1, seq, C_QK_DIM), lambda i, h: (i, 0, h)),
                  pl.BlockSpec((1, seq, C_V_DIM), lambda i, h: (i, 0, 2 + h)),
                  pl.BlockSpec((1, seq, C_QK_DIM), lambda i, h: (i, 0, 12 + h)),
                  pl.BlockSpec((1, ctx_len, C_QK_DIM), lambda i, h: (i, 0, h)),
                  pl.BlockSpec((1, ctx_len, C_V_DIM), lambda i, h: (i, 0, 2 + h)),
                  pl.BlockSpec((1, 2, 8, LANES), lambda i, h: (h, 0, 0, 0))],
        out_specs=pl.BlockSpec((1, seq, C_V_DIM), lambda i, h: (i, 0, h)),
        out_shape=jax.ShapeDtypeStruct((b, seq, C_HEADS * C_V_DIM), F32),
        scratch_shapes=[pltpu.VMEM((C_QK_DIM, C_V_DIM), F32), pltpu.VMEM((C_QK_DIM, C_V_DIM), F32)],
        compiler_params=_cparams(("arbitrary", "arbitrary"), 48),
        name="ret",
    )(p, p, p, p_ctx, p_ctx, dec_b)


def _merge1_kernel(r_ref, g_ref, x_ref, hg_ref, wo_ref, pg_ref, gate_ref, out_ref, z_ref):
    for h in range(C_HEADS):
        hs = slice(h * C_V_DIM, (h + 1) * C_V_DIM)
        z = _silu(g_ref[0, :, hs].astype(F32)) * (_ln(r_ref[0, :, hs]) * hg_ref[:, hs])
        z_ref[:, hs] = z.astype(BF16)
    y = _dot(z_ref[...], wo_ref[...])
    out_ref[0] = x_ref[0] + gate_ref[0] * _rms(y, pg_ref[...])


def _merge1(r, p, x, head_g, w_out, post_g, gate, tm):
    nb = x.shape[0]
    d = D_MODEL
    hv = C_HEADS * C_V_DIM
    blocks_per_gate = nb // gate.shape[0]
    return pl.pallas_call(
        _merge1_kernel,
        grid=(nb,),
        in_specs=[pl.BlockSpec((1, tm, hv), lambda i: (i, 0, 0)),
                  pl.BlockSpec((1, tm, hv), lambda i: (i, 0, 2)),
                  pl.BlockSpec((1, tm, d), lambda i: (i, 0, 0)),
                  pl.BlockSpec((1, hv), lambda i: (0, 0)),
                  pl.BlockSpec((hv, d), lambda i: (0, 0)),
                  pl.BlockSpec((1, d), lambda i: (0, 0)),
                  pl.BlockSpec((1, 1, d), lambda i: (i // blocks_per_gate, 0, 0))],
        out_specs=pl.BlockSpec((1, tm, d), lambda i: (i, 0, 0)),
        out_shape=jax.ShapeDtypeStruct((nb, tm, d), F32),
        scratch_shapes=[pltpu.VMEM((tm, hv), BF16)],
        compiler_params=_cparams(("arbitrary",), 48),
        name="merge1",
    )(r, p, x, head_g, w_out, post_g.reshape(1, d), gate)


def kernel(x, c, ctx, c_ctx, ada_w, ada_b, pre_g, post_g, ffn_up, ffn_conv, ffn_down,
           ab_w_in, ab_qk_conv, ab_gate_b, ab_sgu_w, ab_sgu_b, ab_head_g, ab_w_out,
           ret_w_in, ret_decay, ret_head_g, ret_w_out):
    b, seq, d = x.shape
    ctx_len = ctx.shape[1]
    depth = ada_w.shape[0]
    tm = 512
    ctx_rows = min(2048, b * ctx_len)

    n_cond = b + 1
    n_pad = -n_cond % 8
    cs = jnp.concatenate([c, c_ctx[None, :], jnp.zeros((n_pad, d), F32)], axis=0)
    mods = _adaln(cs, ada_w, ada_b)

    def mod_l(layer, i):
        return mods[layer, :b, i * d:(i + 1) * d].reshape(b, 1, d)

    def mod_c(layer, i):
        return mods[layer, b:b + 1, i * d:(i + 1) * d].reshape(1, 1, d)

    n_slab = D_FF // FFN_TK
    ffn_up3 = ffn_up.astype(BF16).reshape(depth, d, 2 * n_slab, FFN_TK).transpose(0, 2, 1, 3)
    ffn_conv3 = ffn_conv.reshape(depth, 9, n_slab, FFN_TK).transpose(0, 2, 1, 3)
    ffn_down3 = (0.5 * ffn_down).astype(BF16).reshape(depth, n_slab, FFN_TK, d)

    def ffn_l(layer, xl):
        return _ffn(xl, pre_g[layer, 1], mod_l(layer, 3), mod_l(layer, 4), ffn_up3[layer], ffn_conv3[layer],
                    ffn_down3[layer], post_g[layer, 1], mod_l(layer, 5), rows=1024, width=GRID_W, grid2d=True)

    def ffn_c(layer, xc):
        out = _ffn(xc.reshape(1, b * ctx_len, d), pre_g[layer, 1], mod_c(layer, 3), mod_c(layer, 4), ffn_up3[layer],
                   ffn_conv3[layer], ffn_down3[layer], post_g[layer, 1], mod_c(layer, 5),
                   rows=min(1024, b * ctx_len), width=ctx_len, grid2d=False)
        return out.reshape(xc.shape)

    w_in = ab_w_in[0]
    g_lo = 2 * B_DIM
    g_hi = g_lo + N_GATES * B_HEADS
    w_main = jnp.concatenate([w_in[:, :g_lo], w_in[:, g_hi:]], axis=1).astype(BF16)
    w_gates = w_in[:, g_lo:g_hi].reshape(d, N_GATES, B_HEADS)
    lane_pad = ((0, 0), (0, LANES - N_CHAINS))
    w_gate = jnp.concatenate([jnp.pad(w_gates[:, 0::2].reshape(d, N_CHAINS), lane_pad),
                              jnp.pad(w_gates[:, 1::2].reshape(d, N_CHAINS), lane_pad)], axis=1).astype(BF16)
    bias_row = jnp.concatenate([jnp.pad(ab_gate_b[0][0::2].reshape(1, N_CHAINS), lane_pad),
                                jnp.pad(ab_gate_b[0][1::2].reshape(1, N_CHAINS), lane_pad)], axis=1)
    wq = ab_qk_conv[0][:, :B_DIM]
    wk = ab_qk_conv[0][:, B_DIM:]
    head_g0 = ab_head_g[0].reshape(1, B_DIM)
    sgu_w = ab_sgu_w[0].astype(BF16)
    sgu_bb = jnp.broadcast_to(ab_sgu_b[0][:, :, None], (A_GROUPS, CHUNK, CHUNK))
    w_out0 = ab_w_out[0].astype(BF16)

    p_c, gates_c = _proj(ctx.reshape(-1, ctx_rows, d), pre_g[0, 0], mod_c(0, 0), mod_c(0, 1), w_main, w_gate)
    p_l, gates_l = _proj(x, pre_g[0, 0], mod_l(0, 0), mod_l(0, 1), w_main, w_gate)
    p_c = p_c.reshape(b, ctx_len, -1)
    gates_c = gates_c.reshape(b, ctx_len, GATE_LANES)
    zero_state = (jnp.zeros((b, N_CHAINS, B_HEAD_DIM, 2 * B_HEAD_DIM), F32),
                  jnp.zeros((b, N_CHAINS, 1, B_HEAD_DIM), F32))
    mem_c, *ctx_state = _mlstm(p_c, gates_c, wq, wk, bias_row, head_g0, zero_state)
    mem_l, *_ = _mlstm(p_l, gates_l, wq, wk, bias_row, head_g0, tuple(ctx_state))

    rt = lambda t: t.reshape(-1, tm, t.shape[-1])
    x = _merge0(rt(p_l), rt(mem_l), rt(x), sgu_w, sgu_bb, w_out0, post_g[0, 0], mod_l(0, 2), tm).reshape(b, seq, d)
    ctx = _merge0(rt(p_c), rt(mem_c), rt(ctx), sgu_w, sgu_bb, w_out0, post_g[0, 0], mod_c(0, 2), tm).reshape(b, ctx_len, d)
    x = ffn_l(0, x)
    ctx = ffn_c(0, ctx)

    w_ret = ret_w_in[0].astype(BF16)
    kv_cols = C_HEADS * (C_QK_DIM + C_V_DIM)
    dec_b = jnp.broadcast_to(ret_decay[0].T[:, :, None, None], (C_HEADS, 2, 8, LANES))
    p_c = _proj(ctx.reshape(-1, ctx_rows, d), pre_g[1, 0], mod_c(1, 0), mod_c(1, 1), w_ret[:, :kv_cols])
    p_l = _proj(x, pre_g[1, 0], mod_l(1, 0), mod_l(1, 1), w_ret)
    r = _ret(p_l, p_c.reshape(b, ctx_len, -1), dec_b)
    x = _merge1(rt(r), rt(p_l), rt(x), ret_head_g[0].reshape(1, -1), ret_w_out[0].astype(BF16),
                post_g[1, 0], mod_l(1, 2), tm).reshape(b, seq, d)
    x = ffn_l(1, x)
    return x
```

```python
import functools

import jax
import jax.numpy as jnp
from jax import lax
from jax.experimental import pallas as pl
from jax.experimental.pallas import tpu as pltpu

F32 = jnp.float32
BF16 = jnp.bfloat16
HIGHEST = lax.Precision.HIGHEST

D_MODEL = 1024
GRID_W = 64
CHUNK = 128
EPS = 1e-6
N_MOD = 6
A_GROUPS = 4
A_DIM = 512
B_HEADS = 4
B_HEAD_DIM = 128
B_DIM = 512
N_GATES = 4
C_HEADS = 4
C_QK_DIM = 256
C_V_DIM = 512
D_FF = 2816

LANES = 128
GATE_LANES = 2 * LANES
N_CHAINS = 2 * B_HEADS
ML_CHUNK = 256
V7X_VMEM_BYTES = 64 * 1024 * 1024
MIB = 1024 * 1024


def _cparams(semantics, vmem_mib):
    assert vmem_mib * MIB < V7X_VMEM_BYTES
    return pltpu.CompilerParams(dimension_semantics=semantics, vmem_limit_bytes=vmem_mib * MIB)


def _gelu(x):
    return x * (0.5 * (1.0 + jnp.tanh(0.7978845608028654 * (x + 0.044715 * (x * x * x)))))


def _silu(x):
    return x * jax.nn.sigmoid(x)


def _log_sigmoid(x):
    return jnp.minimum(x, 0.0) - jnp.log1p(jnp.exp(-jnp.abs(x)))


def _rms(x, g):
    return x * lax.rsqrt(jnp.mean(x * x, axis=-1, keepdims=True) + EPS) * g


def _ln(x):
    xc = x - jnp.mean(x, axis=-1, keepdims=True)
    return xc * lax.rsqrt(jnp.mean(xc * xc, axis=-1, keepdims=True) + EPS)


def _dot(a, b):
    return jnp.dot(a, b, preferred_element_type=F32)


def _dot_f32(a, b):
    return jnp.dot(a, b, precision=HIGHEST, preferred_element_type=F32)


def _dot_nt(a, b):
    return lax.dot_general(a, b, (((1,), (1,)), ((), ())), preferred_element_type=F32)


def _dot_tn(a, b):
    return lax.dot_general(a, b, (((0,), (0,)), ((), ())), preferred_element_type=F32)


def _adaln_kernel(c_ref, w_ref, b_ref, o_ref):
    s = _silu(c_ref[...])
    o_ref[0] = _dot(s.astype(BF16), w_ref[0].astype(BF16)) + b_ref[0]


def _adaln(cs, ada_w, ada_b):
    depth, d, n = ada_w.shape
    mp = cs.shape[0]
    tn = 1024
    return pl.pallas_call(
        _adaln_kernel,
        grid=(depth, n // tn),
        in_specs=[pl.BlockSpec((mp, d), lambda l, j: (0, 0)),
                  pl.BlockSpec((1, d, tn), lambda l, j: (l, 0, j)),
                  pl.BlockSpec((1, 1, tn), lambda l, j: (l, 0, j))],
        out_specs=pl.BlockSpec((1, mp, tn), lambda l, j: (l, 0, j)),
        out_shape=jax.ShapeDtypeStruct((depth, mp, n), F32),
        compiler_params=_cparams(("arbitrary", "arbitrary"), 32),
        name="adaln",
    )(cs, ada_w, ada_b.reshape(depth, 1, n))


def _proj_kernel(x_ref, g_ref, sh_ref, sc_ref, w_ref, *rest, rows, sub, with_gates):
    if with_gates:
        wg_ref, p_ref, gate_ref, h_ref = rest
    else:
        p_ref, h_ref = rest

    @pl.when(pl.program_id(1) == 0)
    def _():
        for r0 in range(0, rows, sub):
            h = _rms(x_ref[0, r0:r0 + sub, :], g_ref[...]) * (1.0 + sc_ref[0]) + sh_ref[0]
            hb = h.astype(BF16)
            h_ref[r0:r0 + sub, :] = hb
            if with_gates:
                gate_ref[0, r0:r0 + sub, :] = _dot(hb, wg_ref[...])
            p_ref[0, r0:r0 + sub, :] = _dot(hb, w_ref[...]).astype(BF16)

    @pl.when(pl.program_id(1) != 0)
    def _():
        tn = w_ref.shape[1]
        mm_rows = min(rows, 1024)
        for n0 in range(0, tn, 256):
            for r0 in range(0, rows, mm_rows):
                p_ref[0, r0:r0 + mm_rows, n0:n0 + 256] = _dot(h_ref[r0:r0 + mm_rows, :],
                                                              w_ref[:, n0:n0 + 256]).astype(BF16)


def _proj(x3, pre_g, shift, scale, w, wg=None):
    nb, rows, d = x3.shape
    n = w.shape[1]
    tn = 1024
    per_block = shift.shape[0] != 1
    mod_map = (lambda i, j: (i, 0, 0)) if per_block else (lambda i, j: (0, 0, 0))
    in_specs = [pl.BlockSpec((1, rows, d), lambda i, j: (i, 0, 0)),
                pl.BlockSpec((1, d), lambda i, j: (0, 0)),
                pl.BlockSpec((1, 1, d), mod_map),
                pl.BlockSpec((1, 1, d), mod_map),
                pl.BlockSpec((d, tn), lambda i, j: (0, j))]
    out_specs = [pl.BlockSpec((1, rows, tn), lambda i, j: (i, 0, j))]
    out_shape = [jax.ShapeDtypeStruct((nb, rows, n), BF16)]
    args = [x3, pre_g.reshape(1, d), shift, scale, w]
    if wg is not None:
        in_specs.append(pl.BlockSpec((d, GATE_LANES), lambda i, j: (0, 0)))
        out_specs.append(pl.BlockSpec((1, rows, GATE_LANES), lambda i, j: (i, 0, 0)))
        out_shape.append(jax.ShapeDtypeStruct((nb, rows, GATE_LANES), F32))
        args.append(wg)
    outs = pl.pallas_call(
        functools.partial(_proj_kernel, rows=rows, sub=min(512, rows), with_gates=wg is not None),
        grid=(nb, n // tn),
        in_specs=in_specs, out_specs=out_specs, out_shape=out_shape,
        scratch_shapes=[pltpu.VMEM((rows, d), BF16)],
        compiler_params=_cparams(("arbitrary", "arbitrary"), 48),
        name="proj",
    )(*args)
    return outs if wg is not None else outs[0]


def _mlstm_kernel(k_ref, v_ref, q_ref, o_ref, gates_ref, wq_ref, wk_ref, bias_ref, hg_ref, c0_ref, m0_ref,
                  mem_ref, cf_ref, mf_ref,
                  pm_ref, cum_ref, brow_ref, kt_ref, qc_ref, rf_ref, rb_ref, st_ref, *, seq):
    T = ML_CHUNK
    hd = B_HEAD_DIM
    nc = seq // T
    row = lax.broadcasted_iota(jnp.int32, (T, T), 0)
    col = lax.broadcasted_iota(jnp.int32, (T, T), 1)
    lower = row >= col
    upper = col >= row
    fwd_lane = lax.broadcasted_iota(jnp.int32, (T, LANES), 1) < B_HEADS
    trow = lax.broadcasted_iota(jnp.int32, (T, LANES), 0)

    def gate_body(c, carry):
        sl = pl.ds(pl.multiple_of(c * T, T), T)
        g = gates_ref[0, sl, :] + bias_ref[...]
        lf = _log_sigmoid(g[:, LANES:])
        prefix = _dot_f32(lower.astype(F32), lf)
        cum = jnp.where(fwd_lane, prefix, prefix[T - 1:T] - prefix + lf)
        b = g[:, :LANES] - cum
        pf = b
        ps = b
        for sh in (1 << e for e in range(T.bit_length() - 1)):
            pf = jnp.where(trow >= sh, jnp.maximum(pf, pltpu.roll(pf, sh, 0)), pf)
            ps = jnp.where(trow < T - sh, jnp.maximum(ps, pltpu.roll(ps, T - sh, 0)), ps)
        pm_ref[sl, :] = jnp.where(fwd_lane, pf, ps)
        cum_ref[sl, :] = cum
        brow_ref[c] = b.T[:N_CHAINS]
        return carry

    ridx = lax.broadcasted_iota(jnp.int32, (T, hd), 0)

    def conv_body(c, carry):
        start = pl.multiple_of(c * T, T)
        sl = pl.ds(start, T)
        prev_start = pl.multiple_of(jnp.maximum(start - 16, 0), 16)
        next_start = pl.multiple_of(jnp.minimum(start + T, seq - 16), 16)
        has_prev = jnp.where(c > 0, 1.0, 0.0)
        has_next = jnp.where(c < nc - 1, 1.0, 0.0)
        for h in range(B_HEADS):
            hs = slice(h * hd, (h + 1) * hd)
            for src_ref, w_ref, is_k in ((k_ref, wk_ref, True), (q_ref, wq_ref, False)):
                cur = src_ref[0, sl, hs].astype(F32)
                prev_row = src_ref[0, pl.ds(prev_start, 16), hs].astype(F32)[15:16] * has_prev
                next_row = src_ref[0, pl.ds(next_start, 16), hs].astype(F32)[0:1] * has_next
                w = w_ref[:, hs]
                xm = jnp.where(ridx == 0, prev_row, pltpu.roll(cur, 1, 0))
                xp = jnp.where(ridx == T - 1, next_row, pltpu.roll(cur, T - 1, 0))
                y = _silu(xm * w[0:1] + cur * w[1:2] + xp * w[2:3])
                if is_k:
                    kt_ref[h, c] = (y * hd ** -0.5).T.astype(BF16)
                else:
                    qc_ref[sl, hs] = y.astype(BF16)
        return carry

    lax.fori_loop(0, nc, lambda c, carry: conv_body(c, gate_body(c, carry)), 0)

    ones_tile = (lax.broadcasted_iota(jnp.int32, (T, hd), 1) == 0).astype(BF16)
    for j in range(N_CHAINS):
        st_ref[j] = c0_ref[0, j]

    def scan_body(i, ms):
        chains = range(N_CHAINS)
        heads = [j % B_HEADS for j in chains]
        cs = [i if j < B_HEADS else nc - 1 - i for j in chains]
        sls = [pl.ds(pl.multiple_of(cs[j] * T, T), T) for j in chains]
        hss = [slice(h * hd, (h + 1) * hd) for h in heads]
        edge = [cs[j] * T + (T - 1 if j < B_HEADS else 0) for j in chains]
        b_row = [brow_ref[cs[j], j:j + 1, :] for j in chains]
        m_col = [jnp.maximum(ms[j], pm_ref[sls[j], j:j + 1]) for j in chains]
        m_wide = [jnp.broadcast_to(m_col[j], (T, T)) for j in chains]
        q = [qc_ref[sls[j], hss[j]] for j in chains]
        kt = [kt_ref[heads[j], cs[j]] for j in chains]
        dmat = [jnp.where(lower if j < B_HEADS else upper, jnp.exp(b_row[j] - m_wide[j]), 0.0) for j in chains]
        p = [(_dot(q[j], kt[j]) * dmat[j]).astype(BF16) for j in chains]
        v_aug = [jnp.concatenate([v_ref[0, sls[j], hss[j]], ones_tile], axis=1) for j in chains]
        st = [st_ref[j] for j in chains]
        qs = [(q[j].astype(F32) * jnp.exp(ms[j] - m_wide[j][:, :hd])).astype(BF16) for j in chains]
        acc = [_dot(p[j], v_aug[j]) + _dot(qs[j], st[j].astype(BF16)) for j in chains]
        for j in chains:
            out_ref = rf_ref if j < B_HEADS else rb_ref
            den = acc[j][:, hd:hd + 1]
            floor = jnp.exp(-(cum_ref[sls[j], j:j + 1] + m_col[j]))
            out_ref[sls[j], hss[j]] = acc[j][:, :hd] / jnp.maximum(jnp.abs(den), floor)
        new_ms = []
        for j in chains:
            btot = cum_ref[pl.ds(edge[j], 1), j:j + 1]
            m_far = jnp.maximum(ms[j], pm_ref[pl.ds(edge[j], 1), j:j + 1])
            m_new = btot + m_far
            kw_t = (kt[j].astype(F32) * jnp.exp(btot + b_row[j] - m_new)).astype(BF16)
            st_ref[j] = jnp.exp(ms[j] - m_far) * st[j] + _dot(kw_t, v_aug[j])
            new_ms.append(m_new)
        return tuple(new_ms)

    ms = lax.fori_loop(0, nc, scan_body, tuple(m0_ref[0, j][:, 0:1] for j in range(N_CHAINS)), unroll=4)
    for j in range(N_CHAINS):
        cf_ref[0, j] = st_ref[j]
        mf_ref[0, j] = jnp.broadcast_to(ms[j], (1, hd))

    def out_body(c, carry):
        sl = pl.ds(pl.multiple_of(c * T, T), T)
        for h in range(B_HEADS):
            hs = slice(h * hd, (h + 1) * hd)
            y = jax.nn.sigmoid(o_ref[0, sl, hs].astype(F32)) * (rf_ref[sl, hs] + rb_ref[sl, hs])
            mem_ref[0, sl, hs] = (_ln(y) * hg_ref[:, hs]).astype(BF16)
        return carry

    lax.fori_loop(0, nc, out_body, 0)


def _mlstm(p, gates, wq, wk, bias_row, head_g, state):
    b, seq, _ = p.shape
    c0, m0 = state
    hd = B_HEAD_DIM
    assert seq % ML_CHUNK == 0
    nc = seq // ML_CHUNK
    sec = lambda idx: pl.BlockSpec((1, seq, B_DIM), lambda i, idx=idx: (i, 0, idx))
    full = lambda shape: pl.BlockSpec(shape, lambda i: (0,) * len(shape))
    st_c = pl.BlockSpec((1, N_CHAINS, hd, 2 * hd), lambda i: (i, 0, 0, 0))
    st_m = pl.BlockSpec((1, N_CHAINS, 1, hd), lambda i: (i, 0, 0, 0))
    return pl.pallas_call(
        functools.partial(_mlstm_kernel, seq=seq),
        grid=(b,),
        in_specs=[sec(0), sec(1), sec(2), sec(3),
                  pl.BlockSpec((1, seq, GATE_LANES), lambda i: (i, 0, 0)),
                  full((3, B_DIM)), full((3, B_DIM)), full((1, GATE_LANES)), full((1, B_DIM)),
                  st_c, st_m],
        out_specs=[pl.BlockSpec((1, seq, B_DIM), lambda i: (i, 0, 0)), st_c, st_m],
        out_shape=[jax.ShapeDtypeStruct((b, seq, B_DIM), BF16),
                   jax.ShapeDtypeStruct(c0.shape, F32),
                   jax.ShapeDtypeStruct(m0.shape, F32)],
        scratch_shapes=[pltpu.VMEM((seq, LANES), F32), pltpu.VMEM((seq, LANES), F32),
                        pltpu.VMEM((nc, N_CHAINS, ML_CHUNK), F32),
                        pltpu.VMEM((B_HEADS, nc, hd, ML_CHUNK), BF16), pltpu.VMEM((seq, B_DIM), BF16),
                        pltpu.VMEM((seq, B_DIM), F32), pltpu.VMEM((seq, B_DIM), F32),
                        pltpu.VMEM((N_CHAINS, hd, 2 * hd), F32)],
        compiler_params=_cparams(("arbitrary",), 56),
        name="mlstm",
    )(p, p, p, p, gates, wq, wk, bias_row, head_g, c0, m0)


def _merge0_kernel(u_ref, va_ref, mem_ref, x_ref, sw_ref, sb_ref, wo_ref, pg_ref, gate_ref, out_ref, cat_ref, *, tm):
    T = CHUNK
    gd = A_DIM // A_GROUPS
    for c in range(tm // T):
        sl = slice(c * T, (c + 1) * T)
        vn = _ln(_gelu(va_ref[0, sl, :].astype(F32))).astype(BF16)
        u = _gelu(u_ref[0, sl, :].astype(F32))
        for g in range(A_GROUPS):
            gs = slice(g * gd, (g + 1) * gd)
            mixed = _dot(sw_ref[g], vn[:, gs]) + sb_ref[g]
            cat_ref[sl, gs] = (u[:, gs] * mixed).astype(BF16)
    cat_ref[:, A_DIM:] = mem_ref[0]
    y = _dot(cat_ref[...], wo_ref[...])
    out_ref[0] = x_ref[0] + gate_ref[0] * _rms(y, pg_ref[...])


def _merge0(p, mem, x, sgu_w, sgu_bb, w_out, post_g, gate, tm):
    nb = x.shape[0]
    d = D_MODEL
    per_block = gate.shape[0] != 1
    blocks_per_gate = nb // gate.shape[0]
    gate_map = (lambda i: (i // blocks_per_gate, 0, 0)) if per_block else (lambda i: (0, 0, 0))
    return pl.pallas_call(
        functools.partial(_merge0_kernel, tm=tm),
        grid=(nb,),
        in_specs=[pl.BlockSpec((1, tm, A_DIM), lambda i: (i, 0, 4)),
                  pl.BlockSpec((1, tm, A_DIM), lambda i: (i, 0, 5)),
                  pl.BlockSpec((1, tm, B_DIM), lambda i: (i, 0, 0)),
                  pl.BlockSpec((1, tm, d), lambda i: (i, 0, 0)),
                  pl.BlockSpec((A_GROUPS, CHUNK, CHUNK), lambda i: (0, 0, 0)),
                  pl.BlockSpec((A_GROUPS, CHUNK, CHUNK), lambda i: (0, 0, 0)),
                  pl.BlockSpec((A_DIM + B_DIM, d), lambda i: (0, 0)),
                  pl.BlockSpec((1, d), lambda i: (0, 0)),
                  pl.BlockSpec((1, 1, d), gate_map)],
        out_specs=pl.BlockSpec((1, tm, d), lambda i: (i, 0, 0)),
        out_shape=jax.ShapeDtypeStruct((nb, tm, d), F32),
        scratch_shapes=[pltpu.VMEM((tm, A_DIM + B_DIM), BF16)],
        compiler_params=_cparams(("arbitrary",), 32),
        name="merge0",
    )(p, p, mem, x, sgu_w, sgu_bb, w_out, post_g.reshape(1, d), gate)


FFN_TK = 256
FFN_HALO = 64
FFN_PAD = 8
FFN_MM_ROWS = 512


def _ffn_kernel(*refs, rows, tile_rows, width, grid2d, n_tiles):
    if grid2d:
        (x_ref, xu_ref, xd_ref, g_ref, sh_ref, sc_ref, wup_ref, cw_ref, wd_ref, pg_ref, gate_ref,
         o_ref, h_ref, gs0_ref, gs1_ref, act_ref) = refs
    else:
        (x_ref, g_ref, sh_ref, sc_ref, wup_ref, cw_ref, wd_ref, pg_ref, gate_ref,
         o_ref, h_ref, gs0_ref, gs1_ref, act_ref) = refs
    halo = FFN_HALO if grid2d else 0
    tk = FFN_TK
    ks = D_FF // tk
    hrows = rows + 2 * halo
    sub = 512

    def normmod(xv):
        return _rms(xv, g_ref[...]) * (1.0 + sc_ref[0]) + sh_ref[0]

    for gs_ref in (gs0_ref, gs1_ref):
        gs_ref[0:FFN_PAD, :] = jnp.zeros((FFN_PAD, tk), F32)
        gs_ref[FFN_PAD + hrows:2 * FFN_PAD + hrows, :] = jnp.zeros((FFN_PAD, tk), F32)

    g_step = hrows // (hrows // FFN_MM_ROWS)
    g_chunks = [(r0, r0 + g_step) for r0 in range(0, hrows, g_step)]

    def gate_chunk(slab, gs_ref, r0, r1):
        gs_ref[FFN_PAD + r0:FFN_PAD + r1, :] = _dot(h_ref[r0:r1, :], wup_ref[ks + slab])

    def gate_branch(slab, gs_ref):
        for r0, r1 in g_chunks:
            gate_chunk(slab, gs_ref, r0, r1)

    pieces = [(halo + r0, halo + r0 + sub, lambda r0=r0: normmod(x_ref[0, r0:r0 + sub, :])) for r0 in range(0, rows, sub)]
    if grid2d:
        t = pl.program_id(0) % n_tiles
        pieces = ([(0, halo, lambda: normmod(xu_ref[0]) * (t > 0).astype(F32))] + pieces
                  + [(halo + rows, hrows, lambda: normmod(xd_ref[0]) * (t < n_tiles - 1).astype(F32))])
    pending = list(g_chunks)
    for p0, p1, piece in pieces:
        h_ref[p0:p1, :] = piece().astype(BF16)
        while pending and pending[0][1] <= p1:
            gate_chunk(0, gs0_ref, *pending.pop(0))
    assert not pending

    iota8 = lax.broadcasted_iota(jnp.int32, (8, tk), 0)

    def cut(vals, first):
        pieces = []
        for g0 in range(0, tile_rows, width):
            if first:
                pieces += [jnp.where(iota8 == 0, 0.0, vals[g0:g0 + 8]), vals[g0 + 8:g0 + width]]
            else:
                pieces += [vals[g0:g0 + width - 8], jnp.where(iota8 == 7, 0.0, vals[g0 + width - 8:g0 + width])]
        return jnp.concatenate(pieces, axis=0)

    def tiles(slab, gs_ref):
        cw = cw_ref[slab]
        for base in range(0, rows, tile_rows):
            if base % FFN_MM_ROWS == 0:
                a_rows = _dot(h_ref[halo + base:halo + base + FFN_MM_ROWS, :], wup_ref[slab])
            a = a_rows[base % FFN_MM_ROWS:base % FFN_MM_ROWS + tile_rows]
            dys = (0, 1, 2) if grid2d else (1,)
            g_rows = [gs_ref[FFN_PAD + halo + base + (dy - 1) * halo:FFN_PAD + halo + base + (dy - 1) * halo + tile_rows, :]
                      for dy in dys]
            left, mid, right = (functools.reduce(lambda s, t: s + t,
                                                 [cw[3 * dy + dx:3 * dy + dx + 1] * g for dy, g in zip(dys, g_rows)])
                                for dx in range(3))
            conv = (mid + cut(pltpu.roll(left, 1, 0), True) + cut(pltpu.roll(right, tile_rows - 1, 0), False))
            t = jnp.tanh(conv * (0.7978845608028654 + (0.7978845608028654 * 0.044715) * (conv * conv)))
            p = a * conv
            act_ref[slab, base:base + tile_rows, :] = (p + p * t).astype(BF16)

    assert ks % 2 == 1

    def body(j, carry):
        k0 = 2 * j
        gate_branch(k0 + 1, gs1_ref)
        tiles(k0, gs0_ref)
        gate_branch(k0 + 2, gs0_ref)
        tiles(k0 + 1, gs1_ref)
        return carry

    lax.fori_loop(0, ks // 2, body, 0)
    tiles(ks - 1, gs0_ref)

    for r0 in range(0, rows, FFN_MM_ROWS):
        for n0 in range(0, D_MODEL, tk):
            f = _dot(act_ref[0, r0:r0 + FFN_MM_ROWS, :], wd_ref[0, :, n0:n0 + tk])
            for k in range(1, ks):
                f = f + _dot(act_ref[k, r0:r0 + FFN_MM_ROWS, :], wd_ref[k, :, n0:n0 + tk])
            o_ref[0, r0:r0 + FFN_MM_ROWS, n0:n0 + tk] = f
        f = o_ref[0, r0:r0 + FFN_MM_ROWS, :]
        o_ref[0, r0:r0 + FFN_MM_ROWS, :] = (x_ref[0, r0:r0 + FFN_MM_ROWS, :]
                                            + gate_ref[0] * _rms(f, pg_ref[...]))


def _ffn(x3, pre_g, shift, scale, w_up3, conv3, w_down3, post_g, gate, *, rows, width, grid2d):
    nb, seq, d = x3.shape
    tk = FFN_TK
    ks = D_FF // tk
    n_tiles = seq // rows
    tile_rows = max(width, 128)
    halo = FFN_HALO if grid2d else 0
    assert seq % rows == 0 and rows % 512 == 0 and tile_rows % width == 0 and (not grid2d or width == FFN_HALO)
    per_block = shift.shape[0] != 1
    mod_map = (lambda i: (i // n_tiles, 0, 0)) if per_block else (lambda i: (0, 0, 0))
    vec = lambda: pl.BlockSpec((1, d), lambda i: (0, 0))
    mod = lambda: pl.BlockSpec((1, 1, d), mod_map)
    resident = lambda shape: pl.BlockSpec(shape, lambda i: (0,) * len(shape), pipeline_mode=pl.Buffered(1))
    in_specs = [pl.BlockSpec((1, rows, d), lambda i: (i // n_tiles, i % n_tiles, 0))]
    args = [x3]
    if grid2d:
        per = rows // halo
        last = seq // halo - 1
        in_specs += [pl.BlockSpec((1, halo, d), lambda i: (i // n_tiles, jnp.maximum((i % n_tiles) * per - 1, 0), 0)),
                     pl.BlockSpec((1, halo, d), lambda i: (i // n_tiles, jnp.minimum((i % n_tiles + 1) * per, last), 0))]
        args += [x3, x3]
    in_specs += [vec(), mod(), mod(),
                 resident((2 * ks, d, tk)), resident((ks, 9, tk)), resident((ks, tk, d)),
                 vec(), mod()]
    args += [pre_g.reshape(1, d), shift, scale, w_up3, conv3, w_down3, post_g.reshape(1, d), gate]
    return pl.pallas_call(
        functools.partial(_ffn_kernel, rows=rows, tile_rows=tile_rows, width=width, grid2d=grid2d, n_tiles=n_tiles),
        grid=(nb * n_tiles,),
        in_specs=in_specs,
        out_specs=pl.BlockSpec((1, rows, d), lambda i: (i // n_tiles, i % n_tiles, 0)),
        out_shape=jax.ShapeDtypeStruct((nb, seq, d), F32),
        scratch_shapes=[pltpu.VMEM((rows + 2 * halo, d), BF16),
                        pltpu.VMEM((rows + 2 * halo + 2 * FFN_PAD, tk), F32),
                        pltpu.VMEM((rows + 2 * halo + 2 * FFN_PAD, tk), F32),
                        pltpu.VMEM((ks, rows, tk), BF16)],
        compiler_params=_cparams(("arbitrary",), 56),
        name="ffn",
    )(*args)


RET_CHUNK = 256


def _ret_kernel(k_ref, v_ref, q_ref, kc_ref, vc_ref, dec_ref, r_ref, rf_ref, rb_ref, *, seq, ctx_len):
    T = RET_CHUNK
    nc = seq // T
    t_col = lax.broadcasted_iota(jnp.int32, (T, 1), 0).astype(F32)
    lg = [_log_sigmoid(dec_ref[0, direction][0:1, 0:1]) for direction in range(2)]
    scale = C_QK_DIM ** -0.5
    zeta = (jnp.exp(lg[0] * (T - 1 - t_col)) * scale, jnp.exp(lg[1] * t_col) * scale)
    xi = (jnp.exp(lg[0] * (t_col + 1.0)), jnp.exp(lg[1] * (T - t_col)))
    gch = (jnp.exp(lg[0] * T), jnp.exp(lg[1] * T))
    row = lax.broadcasted_iota(jnp.int32, (T, T), 0)
    col = lax.broadcasted_iota(jnp.int32, (T, T), 1)
    rel = (row - col).astype(F32)
    d_intra = (jnp.where(row >= col, jnp.exp(lg[0] * rel), 0.0)
               + jnp.where(col >= row, jnp.exp(-lg[1] * rel), 0.0)) * scale

    def absorb(state_ref, direction, k, v, first):
        upd = _dot_tn((k.astype(F32) * zeta[direction]).astype(BF16), v)
        state_ref[...] = upd if first else gch[direction] * state_ref[...] + upd

    ncc = ctx_len // T
    for direction, state_ref in ((0, rf_ref), (1, rb_ref)):
        order = list(range(ncc)) if direction == 0 else list(range(ncc - 1, -1, -1))
        for n, c in enumerate(order):
            absorb(state_ref, direction, kc_ref[0, c * T:(c + 1) * T, :], vc_ref[0, c * T:(c + 1) * T, :], n == 0)

    def scan(i, first_visit):
        slf = pl.ds(pl.multiple_of(i * T, T), T)
        slb = pl.ds(pl.multiple_of((nc - 1 - i) * T, T), T)
        q, k, v = q_ref[0, slf, :], k_ref[0, slf, :], v_ref[0, slf, :]
        scores = _dot_nt(q, k) * d_intra
        fwd = _dot(scores.astype(BF16), v) + xi[0] * _dot(q, rf_ref[...].astype(BF16))
        absorb(rf_ref, 0, k, v, False)
        qb, kb, vb = q_ref[0, slb, :], k_ref[0, slb, :], v_ref[0, slb, :]
        bwd = xi[1] * _dot(qb, rb_ref[...].astype(BF16))
        absorb(rb_ref, 1, kb, vb, False)
        if first_visit:
            r_ref[0, slf, :] = fwd
            r_ref[0, slb, :] = bwd
        else:
            r_ref[0, slf, :] += fwd
            r_ref[0, slb, :] += bwd

    assert nc % 2 == 0

    def first_half(i, carry):
        scan(i, True)
        return carry

    def second_half(i, carry):
        scan(i, False)
        return carry

    lax.fori_loop(0, nc // 2, first_half, 0, unroll=True)
    lax.fori_loop(nc // 2, nc, second_half, 0, unroll=True)


def _ret(p, p_ctx, dec_b):
    b, seq, _ = p.shape
    ctx_len = p_ctx.shape[1]
    assert seq % RET_CHUNK == 0 and ctx_len % RET_CHUNK == 0
    return pl.pallas_call(
        functools.partial(_ret_kernel, seq=seq, ctx_len=ctx_len),
        grid=(b, C_HEADS),
        in_specs=[pl.BlockSpec((1, seq, C_QK_DIM), lambda i, h: (i, 0, h)),
                  pl.BlockSpec((1, seq, C_V_DIM), lambda i, h: (i, 0, 2 + h)),
                  pl.BlockSpec((1, seq, C_QK_DIM), lambda i, h: (i, 0, 12 + h)),
                  pl.BlockSpec((1, ctx_len, C_QK_DIM), lambda i, h: (i, 0, h)),
                  pl.BlockSpec((1, ctx_len, C_V_DIM), lambda i, h: (i, 0, 2 + h)),
                  pl.BlockSpec((1, 2, 8, LANES), lambda i, h: (h, 0, 0, 0))],
        out_specs=pl.BlockSpec((1, seq, C_V_DIM), lambda i, h: (i, 0, h)),
        out_shape=jax.ShapeDtypeStruct((b, seq, C_HEADS * C_V_DIM), F32),
        scratch_shapes=[pltpu.VMEM((C_QK_DIM, C_V_DIM), F32), pltpu.VMEM((C_QK_DIM, C_V_DIM), F32)],
        compiler_params=_cparams(("arbitrary", "arbitrary"), 48),
        name="ret",
    )(p, p, p, p_ctx, p_ctx, dec_b)


def _merge1_kernel(r_ref, g_ref, x_ref, hg_ref, wo_ref, pg_ref, gate_ref, out_ref, z_ref):
    for h in range(C_HEADS):
        hs = slice(h * C_V_DIM, (h + 1) * C_V_DIM)
        z = _silu(g_ref[0, :, hs].astype(F32)) * (_ln(r_ref[0, :, hs]) * hg_ref[:, hs])
        z_ref[:, hs] = z.astype(BF16)
    y = _dot(z_ref[...], wo_ref[...])
    out_ref[0] = x_ref[0] + gate_ref[0] * _rms(y, pg_ref[...])


def _merge1(r, p, x, head_g, w_out, post_g, gate, tm):
    nb = x.shape[0]
    d = D_MODEL
    hv = C_HEADS * C_V_DIM
    blocks_per_gate = nb // gate.shape[0]
    return pl.pallas_call(
        _merge1_kernel,
        grid=(nb,),
        in_specs=[pl.BlockSpec((1, tm, hv), lambda i: (i, 0, 0)),
                  pl.BlockSpec((1, tm, hv), lambda i: (i, 0, 2)),
                  pl.BlockSpec((1, tm, d), lambda i: (i, 0, 0)),
                  pl.BlockSpec((1, hv), lambda i: (0, 0)),
                  pl.BlockSpec((hv, d), lambda i: (0, 0)),
                  pl.BlockSpec((1, d), lambda i: (0, 0)),
                  pl.BlockSpec((1, 1, d), lambda i: (i // blocks_per_gate, 0, 0))],
        out_specs=pl.BlockSpec((1, tm, d), lambda i: (i, 0, 0)),
        out_shape=jax.ShapeDtypeStruct((nb, tm, d), F32),
        scratch_shapes=[pltpu.VMEM((tm, hv), BF16)],
        compiler_params=_cparams(("arbitrary",), 48),
        name="merge1",
    )(r, p, x, head_g, w_out, post_g.reshape(1, d), gate)


def kernel(x, c, ctx, c_ctx, ada_w, ada_b, pre_g, post_g, ffn_up, ffn_conv, ffn_down,
           ab_w_in, ab_qk_conv, ab_gate_b, ab_sgu_w, ab_sgu_b, ab_head_g, ab_w_out,
           ret_w_in, ret_decay, ret_head_g, ret_w_out):
    b, seq, d = x.shape
    ctx_len = ctx.shape[1]
    depth = ada_w.shape[0]
    tm = 512
    ctx_rows = min(2048, b * ctx_len)

    n_cond = b + 1
    n_pad = -n_cond % 8
    cs = jnp.concatenate([c, c_ctx[None, :], jnp.zeros((n_pad, d), F32)], axis=0)
    mods = _adaln(cs, ada_w, ada_b)

    def mod_l(layer, i):
        return mods[layer, :b, i * d:(i + 1) * d].reshape(b, 1, d)

    def mod_c(layer, i):
        return mods[layer, b:b + 1, i * d:(i + 1) * d].reshape(1, 1, d)

    n_slab = D_FF // FFN_TK
    ffn_up3 = ffn_up.astype(BF16).reshape(depth, d, 2 * n_slab, FFN_TK).transpose(0, 2, 1, 3)
    ffn_conv3 = ffn_conv.reshape(depth, 9, n_slab, FFN_TK).transpose(0, 2, 1, 3)
    ffn_down3 = (0.5 * ffn_down).astype(BF16).reshape(depth, n_slab, FFN_TK, d)

    def ffn_l(layer, xl):
        return _ffn(xl, pre_g[layer, 1], mod_l(layer, 3), mod_l(layer, 4), ffn_up3[layer], ffn_conv3[layer],
                    ffn_down3[layer], post_g[layer, 1], mod_l(layer, 5), rows=1024, width=GRID_W, grid2d=True)

    def ffn_c(layer, xc):
        out = _ffn(xc.reshape(1, b * ctx_len, d), pre_g[layer, 1], mod_c(layer, 3), mod_c(layer, 4), ffn_up3[layer],
                   ffn_conv3[layer], ffn_down3[layer], post_g[layer, 1], mod_c(layer, 5),
                   rows=min(1024, b * ctx_len), width=ctx_len, grid2d=False)
        return out.reshape(xc.shape)

    w_in = ab_w_in[0]
    g_lo = 2 * B_DIM
    g_hi = g_lo + N_GATES * B_HEADS
    w_main = jnp.concatenate([w_in[:, :g_lo], w_in[:, g_hi:]], axis=1).astype(BF16)
    w_gates = w_in[:, g_lo:g_hi].reshape(d, N_GATES, B_HEADS)
    lane_pad = ((0, 0), (0, LANES - N_CHAINS))
    w_gate = jnp.concatenate([jnp.pad(w_gates[:, 0::2].reshape(d, N_CHAINS), lane_pad),
                              jnp.pad(w_gates[:, 1::2].reshape(d, N_CHAINS), lane_pad)], axis=1).astype(BF16)
    bias_row = jnp.concatenate([jnp.pad(ab_gate_b[0][0::2].reshape(1, N_CHAINS), lane_pad),
                                jnp.pad(ab_gate_b[0][1::2].reshape(1, N_CHAINS), lane_pad)], axis=1)
    wq = ab_qk_conv[0][:, :B_DIM]
    wk = ab_qk_conv[0][:, B_DIM:]
    head_g0 = ab_head_g[0].reshape(1, B_DIM)
    sgu_w = ab_sgu_w[0].astype(BF16)
    sgu_bb = jnp.broadcast_to(ab_sgu_b[0][:, :, None], (A_GROUPS, CHUNK, CHUNK))
    w_out0 = ab_w_out[0].astype(BF16)

    p_c, gates_c = _proj(ctx.reshape(-1, ctx_rows, d), pre_g[0, 0], mod_c(0, 0), mod_c(0, 1), w_main, w_gate)
    p_l, gates_l = _proj(x, pre_g[0, 0], mod_l(0, 0), mod_l(0, 1), w_main, w_gate)
    p_c = p_c.reshape(b, ctx_len, -1)
    gates_c = gates_c.reshape(b, ctx_len, GATE_LANES)
    zero_state = (jnp.zeros((b, N_CHAINS, B_HEAD_DIM, 2 * B_HEAD_DIM), F32),
                  jnp.zeros((b, N_CHAINS, 1, B_HEAD_DIM), F32))
    mem_c, *ctx_state = _mlstm(p_c, gates_c, wq, wk, bias_row, head_g0, zero_state)
    mem_l, *_ = _mlstm(p_l, gates_l, wq, wk, bias_row, head_g0, tuple(ctx_state))

    rt = lambda t: t.reshape(-1, tm, t.shape[-1])
    x = _merge0(rt(p_l), rt(mem_l), rt(x), sgu_w, sgu_bb, w_out0, post_g[0, 0], mod_l(0, 2), tm).reshape(b, seq, d)
    ctx = _merge0(rt(p_c), rt(mem_c), rt(ctx), sgu_w, sgu_bb, w_out0, post_g[0, 0], mod_c(0, 2), tm).reshape(b, ctx_len, d)
    x = ffn_l(0, x)
    ctx = ffn_c(0, ctx)

    w_ret = ret_w_in[0].astype(BF16)
    kv_cols = C_HEADS * (C_QK_DIM + C_V_DIM)
    dec_b = jnp.broadcast_to(ret_decay[0].T[:, :, None, None], (C_HEADS, 2, 8, LANES))
    p_c = _proj(ctx.reshape(-1, ctx_rows, d), pre_g[1, 0], mod_c(1, 0), mod_c(1, 1), w_ret[:, :kv_cols])
    p_l = _proj(x, pre_g[1, 0], mod_l(1, 0), mod_l(1, 1), w_ret)
    r = _ret(p_l, p_c.reshape(b, ctx_len, -1), dec_b)
    x = _merge1(rt(r), rt(p_l), rt(x), ret_head_g[0].reshape(1, -1), ret_w_out[0].astype(BF16),
                post_g[1, 0], mod_l(1, 2), tm).reshape(b, seq, d)
    x = ffn_l(1, x)
    return x
```

```python
import functools

import jax
import jax.numpy as jnp
from jax import lax
from jax.experimental import pallas as pl
from jax.experimental.pallas import tpu as pltpu

F32 = jnp.float32
BF16 = jnp.bfloat16
HIGHEST = lax.Precision.HIGHEST

D_MODEL = 1024
GRID_W = 64
CHUNK = 128
EPS = 1e-6
N_MOD = 6
A_GROUPS = 4
A_DIM = 512
B_HEADS = 4
B_HEAD_DIM = 128
B_DIM = 512
N_GATES = 4
C_HEADS = 4
C_QK_DIM = 256
C_V_DIM = 512
D_FF = 2816

LANES = 128
MXU_WIDTH = 256
BF16_ROWS = 16
GATE_LANES = 2 * LANES
N_CHAINS = 2 * B_HEADS
ML_CHUNK = 256
V7X_VMEM_BYTES = 64 * 1024 * 1024
MIB = 1024 * 1024


def _cparams(semantics, vmem_mib):
    assert vmem_mib * MIB < V7X_VMEM_BYTES
    return pltpu.CompilerParams(dimension_semantics=semantics, vmem_limit_bytes=vmem_mib * MIB)


def _gelu(x):
    return x * (0.5 * (1.0 + jnp.tanh(0.7978845608028654 * (x + 0.044715 * (x * x * x)))))


def _silu(x):
    return x * jax.nn.sigmoid(x)


def _log_sigmoid(x):
    return jnp.minimum(x, 0.0) - jnp.log1p(jnp.exp(-jnp.abs(x)))


def _rms(x, g):
    return x * lax.rsqrt(jnp.mean(x * x, axis=-1, keepdims=True) + EPS) * g


def _ln(x):
    xc = x - jnp.mean(x, axis=-1, keepdims=True)
    return xc * lax.rsqrt(jnp.mean(xc * xc, axis=-1, keepdims=True) + EPS)


def _dot(a, b):
    return jnp.dot(a, b, preferred_element_type=F32)


def _dot_f32(a, b):
    return jnp.dot(a, b, precision=HIGHEST, preferred_element_type=F32)


def _dot_nt(a, b):
    return lax.dot_general(a, b, (((1,), (1,)), ((), ())), preferred_element_type=F32)


def _dot_tn(a, b):
    return lax.dot_general(a, b, (((0,), (0,)), ((), ())), preferred_element_type=F32)


def _adaln_kernel(c_ref, w_ref, b_ref, o_ref):
    s = _silu(c_ref[...])
    o_ref[0] = _dot(s.astype(BF16), w_ref[0].astype(BF16)) + b_ref[0]


def _adaln(cs, ada_w, ada_b):
    depth, d, n = ada_w.shape
    mp = cs.shape[0]
    tn = 1024
    return pl.pallas_call(
        _adaln_kernel,
        grid=(depth, n // tn),
        in_specs=[pl.BlockSpec((mp, d), lambda l, j: (0, 0)),
                  pl.BlockSpec((1, d, tn), lambda l, j: (l, 0, j)),
                  pl.BlockSpec((1, 1, tn), lambda l, j: (l, 0, j))],
        out_specs=pl.BlockSpec((1, mp, tn), lambda l, j: (l, 0, j)),
        out_shape=jax.ShapeDtypeStruct((depth, mp, n), F32),
        compiler_params=_cparams(("arbitrary", "arbitrary"), 32),
        name="adaln",
    )(cs, ada_w, ada_b.reshape(depth, 1, n))


def _proj_kernel(x_ref, g_ref, sh_ref, sc_ref, w_ref, *rest, rows, sub, with_gates):
    if with_gates:
        wg_ref, p_ref, gate_ref, h_ref = rest
    else:
        p_ref, h_ref = rest

    @pl.when(pl.program_id(1) == 0)
    def _():
        for r0 in range(0, rows, sub):
            h = _rms(x_ref[0, r0:r0 + sub, :], g_ref[...]) * (1.0 + sc_ref[0]) + sh_ref[0]
            hb = h.astype(BF16)
            h_ref[r0:r0 + sub, :] = hb
            if with_gates:
                gate_ref[0, r0:r0 + sub, :] = _dot(hb, wg_ref[...])
            p_ref[0, r0:r0 + sub, :] = _dot(hb, w_ref[...]).astype(BF16)

    @pl.when(pl.program_id(1) != 0)
    def _():
        tn = w_ref.shape[1]
        mm_rows = min(rows, 1024)
        for n0 in range(0, tn, MXU_WIDTH):
            for r0 in range(0, rows, mm_rows):
                p_ref[0, r0:r0 + mm_rows, n0:n0 + MXU_WIDTH] = _dot(h_ref[r0:r0 + mm_rows, :],
                                                                    w_ref[:, n0:n0 + MXU_WIDTH]).astype(BF16)


def _proj(x3, pre_g, shift, scale, w, wg=None):
    nb, rows, d = x3.shape
    n = w.shape[1]
    tn = 1024
    per_block = shift.shape[0] != 1
    mod_map = (lambda i, j: (i, 0, 0)) if per_block else (lambda i, j: (0, 0, 0))
    in_specs = [pl.BlockSpec((1, rows, d), lambda i, j: (i, 0, 0)),
                pl.BlockSpec((1, d), lambda i, j: (0, 0)),
                pl.BlockSpec((1, 1, d), mod_map),
                pl.BlockSpec((1, 1, d), mod_map),
                pl.BlockSpec((d, tn), lambda i, j: (0, j))]
    out_specs = [pl.BlockSpec((1, rows, tn), lambda i, j: (i, 0, j))]
    out_shape = [jax.ShapeDtypeStruct((nb, rows, n), BF16)]
    args = [x3, pre_g.reshape(1, d), shift, scale, w]
    if wg is not None:
        in_specs.append(pl.BlockSpec((d, GATE_LANES), lambda i, j: (0, 0)))
        out_specs.append(pl.BlockSpec((1, rows, GATE_LANES), lambda i, j: (i, 0, 0)))
        out_shape.append(jax.ShapeDtypeStruct((nb, rows, GATE_LANES), F32))
        args.append(wg)
    outs = pl.pallas_call(
        functools.partial(_proj_kernel, rows=rows, sub=min(512, rows), with_gates=wg is not None),
        grid=(nb, n // tn),
        in_specs=in_specs, out_specs=out_specs, out_shape=out_shape,
        scratch_shapes=[pltpu.VMEM((rows, d), BF16)],
        compiler_params=_cparams(("arbitrary", "arbitrary"), 48),
        name="proj",
    )(*args)
    return outs if wg is not None else outs[0]


def _mlstm_kernel(k_ref, v_ref, q_ref, o_ref, gates_ref, wq_ref, wk_ref, bias_ref, hg_ref, c0_ref, m0_ref,
                  mem_ref, cf_ref, mf_ref,
                  pm_ref, cum_ref, brow_ref, kt_ref, qc_ref, rf_ref, rb_ref, st_ref, *, seq):
    T = ML_CHUNK
    hd = B_HEAD_DIM
    nc = seq // T
    row = lax.broadcasted_iota(jnp.int32, (T, T), 0)
    col = lax.broadcasted_iota(jnp.int32, (T, T), 1)
    lower = row >= col
    upper = col >= row
    fwd_lane = lax.broadcasted_iota(jnp.int32, (T, LANES), 1) < B_HEADS
    trow = lax.broadcasted_iota(jnp.int32, (T, LANES), 0)

    def gate_body(c, carry):
        sl = pl.ds(pl.multiple_of(c * T, T), T)
        g = gates_ref[0, sl, :] + bias_ref[...]
        lf = _log_sigmoid(g[:, LANES:])
        prefix = _dot_f32(lower.astype(F32), lf)
        cum = jnp.where(fwd_lane, prefix, prefix[T - 1:T] - prefix + lf)
        b = g[:, :LANES] - cum
        pf = b
        ps = b
        for sh in (1 << e for e in range(T.bit_length() - 1)):
            pf = jnp.where(trow >= sh, jnp.maximum(pf, pltpu.roll(pf, sh, 0)), pf)
            ps = jnp.where(trow < T - sh, jnp.maximum(ps, pltpu.roll(ps, T - sh, 0)), ps)
        pm_ref[sl, :] = jnp.where(fwd_lane, pf, ps)
        cum_ref[sl, :] = cum
        brow_ref[c] = b.T[:N_CHAINS]
        return carry

    ridx = lax.broadcasted_iota(jnp.int32, (T, hd), 0)

    def conv_body(c, carry):
        start = pl.multiple_of(c * T, T)
        sl = pl.ds(start, T)
        prev_start = pl.multiple_of(jnp.maximum(start - BF16_ROWS, 0), BF16_ROWS)
        next_start = pl.multiple_of(jnp.minimum(start + T, seq - BF16_ROWS), BF16_ROWS)
        has_prev = jnp.where(c > 0, 1.0, 0.0)
        has_next = jnp.where(c < nc - 1, 1.0, 0.0)
        for h in range(B_HEADS):
            hs = slice(h * hd, (h + 1) * hd)
            for src_ref, w_ref, is_k in ((k_ref, wk_ref, True), (q_ref, wq_ref, False)):
                cur = src_ref[0, sl, hs].astype(F32)
                prev_row = src_ref[0, pl.ds(prev_start, BF16_ROWS), hs].astype(F32)[BF16_ROWS - 1:] * has_prev
                next_row = src_ref[0, pl.ds(next_start, BF16_ROWS), hs].astype(F32)[0:1] * has_next
                w = w_ref[:, hs]
                xm = jnp.where(ridx == 0, prev_row, pltpu.roll(cur, 1, 0))
                xp = jnp.where(ridx == T - 1, next_row, pltpu.roll(cur, T - 1, 0))
                y = _silu(xm * w[0:1] + cur * w[1:2] + xp * w[2:3])
                if is_k:
                    kt_ref[h, c] = (y * hd ** -0.5).T.astype(BF16)
                else:
                    qc_ref[sl, hs] = y.astype(BF16)
        return carry

    lax.fori_loop(0, nc, lambda c, carry: conv_body(c, gate_body(c, carry)), 0)

    ones_tile = (lax.broadcasted_iota(jnp.int32, (T, hd), 1) == 0).astype(BF16)
    for j in range(N_CHAINS):
        st_ref[j] = c0_ref[0, j]

    def scan_body(i, ms):
        chains = range(N_CHAINS)
        heads = [j % B_HEADS for j in chains]
        cs = [i if j < B_HEADS else nc - 1 - i for j in chains]
        sls = [pl.ds(pl.multiple_of(cs[j] * T, T), T) for j in chains]
        hss = [slice(h * hd, (h + 1) * hd) for h in heads]
        edge = [cs[j] * T + (T - 1 if j < B_HEADS else 0) for j in chains]
        b_row = [brow_ref[cs[j], j:j + 1, :] for j in chains]
        m_col = [jnp.maximum(ms[j], pm_ref[sls[j], j:j + 1]) for j in chains]
        m_wide = [jnp.broadcast_to(m_col[j], (T, T)) for j in chains]
        q = [qc_ref[sls[j], hss[j]] for j in chains]
        kt = [kt_ref[heads[j], cs[j]] for j in chains]
        dmat = [jnp.where(lower if j < B_HEADS else upper, jnp.exp(b_row[j] - m_wide[j]), 0.0) for j in chains]
        p = [(_dot(q[j], kt[j]) * dmat[j]).astype(BF16) for j in chains]
        v_aug = [jnp.concatenate([v_ref[0, sls[j], hss[j]], ones_tile], axis=1) for j in chains]
        st = [st_ref[j] for j in chains]
        qs = [(q[j].astype(F32) * jnp.exp(ms[j] - m_wide[j][:, :hd])).astype(BF16) for j in chains]
        acc = [_dot(p[j], v_aug[j]) + _dot(qs[j], st[j].astype(BF16)) for j in chains]
        for j in chains:
            out_ref = rf_ref if j < B_HEADS else rb_ref
            den = acc[j][:, hd:hd + 1]
            floor = jnp.exp(-(cum_ref[sls[j], j:j + 1] + m_col[j]))
            out_ref[sls[j], hss[j]] = acc[j][:, :hd] / jnp.maximum(jnp.abs(den), floor)
        new_ms = []
        for j in chains:
            btot = cum_ref[pl.ds(edge[j], 1), j:j + 1]
            m_far = jnp.maximum(ms[j], pm_ref[pl.ds(edge[j], 1), j:j + 1])
            m_new = btot + m_far
            kw_t = (kt[j].astype(F32) * jnp.exp(btot + b_row[j] - m_new)).astype(BF16)
            st_ref[j] = jnp.exp(ms[j] - m_far) * st[j] + _dot(kw_t, v_aug[j])
            new_ms.append(m_new)
        return tuple(new_ms)

    ms = lax.fori_loop(0, nc, scan_body, tuple(m0_ref[0, j][:, 0:1] for j in range(N_CHAINS)), unroll=4)
    for j in range(N_CHAINS):
        cf_ref[0, j] = st_ref[j]
        mf_ref[0, j] = jnp.broadcast_to(ms[j], (1, hd))

    def out_body(c, carry):
        sl = pl.ds(pl.multiple_of(c * T, T), T)
        for h in range(B_HEADS):
            hs = slice(h * hd, (h + 1) * hd)
            y = jax.nn.sigmoid(o_ref[0, sl, hs].astype(F32)) * (rf_ref[sl, hs] + rb_ref[sl, hs])
            mem_ref[0, sl, hs] = (_ln(y) * hg_ref[:, hs]).astype(BF16)
        return carry

    lax.fori_loop(0, nc, out_body, 0)


def _mlstm(p, gates, wq, wk, bias_row, head_g, state):
    b, seq, _ = p.shape
    c0, m0 = state
    hd = B_HEAD_DIM
    assert seq % ML_CHUNK == 0
    nc = seq // ML_CHUNK
    sec = lambda idx: pl.BlockSpec((1, seq, B_DIM), lambda i, idx=idx: (i, 0, idx))
    full = lambda shape: pl.BlockSpec(shape, lambda i: (0,) * len(shape))
    st_c = pl.BlockSpec((1, N_CHAINS, hd, 2 * hd), lambda i: (i, 0, 0, 0))
    st_m = pl.BlockSpec((1, N_CHAINS, 1, hd), lambda i: (i, 0, 0, 0))
    return pl.pallas_call(
        functools.partial(_mlstm_kernel, seq=seq),
        grid=(b,),
        in_specs=[sec(0), sec(1), sec(2), sec(3),
                  pl.BlockSpec((1, seq, GATE_LANES), lambda i: (i, 0, 0)),
                  full((3, B_DIM)), full((3, B_DIM)), full((1, GATE_LANES)), full((1, B_DIM)),
                  st_c, st_m],
        out_specs=[pl.BlockSpec((1, seq, B_DIM), lambda i: (i, 0, 0)), st_c, st_m],
        out_shape=[jax.ShapeDtypeStruct((b, seq, B_DIM), BF16),
                   jax.ShapeDtypeStruct(c0.shape, F32),
                   jax.ShapeDtypeStruct(m0.shape, F32)],
        scratch_shapes=[pltpu.VMEM((seq, LANES), F32), pltpu.VMEM((seq, LANES), F32),
                        pltpu.VMEM((nc, N_CHAINS, ML_CHUNK), F32),
                        pltpu.VMEM((B_HEADS, nc, hd, ML_CHUNK), BF16), pltpu.VMEM((seq, B_DIM), BF16),
                        pltpu.VMEM((seq, B_DIM), F32), pltpu.VMEM((seq, B_DIM), F32),
                        pltpu.VMEM((N_CHAINS, hd, 2 * hd), F32)],
        compiler_params=_cparams(("arbitrary",), 56),
        name="mlstm",
    )(p, p, p, p, gates, wq, wk, bias_row, head_g, c0, m0)


def _merge0_kernel(u_ref, va_ref, mem_ref, x_ref, sw_ref, sb_ref, wo_ref, pg_ref, gate_ref, out_ref, cat_ref, *, tm):
    T = CHUNK
    gd = A_DIM // A_GROUPS
    for c in range(tm // T):
        sl = slice(c * T, (c + 1) * T)
        vn = _ln(_gelu(va_ref[0, sl, :].astype(F32))).astype(BF16)
        u = _gelu(u_ref[0, sl, :].astype(F32))
        for g in range(A_GROUPS):
            gs = slice(g * gd, (g + 1) * gd)
            mixed = _dot(sw_ref[g], vn[:, gs]) + sb_ref[g]
            cat_ref[sl, gs] = (u[:, gs] * mixed).astype(BF16)
    cat_ref[:, A_DIM:] = mem_ref[0]
    y = _dot(cat_ref[...], wo_ref[...])
    out_ref[0] = x_ref[0] + gate_ref[0] * _rms(y, pg_ref[...])


def _merge0(p, mem, x, sgu_w, sgu_bb, w_out, post_g, gate, tm):
    nb = x.shape[0]
    d = D_MODEL
    per_block = gate.shape[0] != 1
    blocks_per_gate = nb // gate.shape[0]
    gate_map = (lambda i: (i // blocks_per_gate, 0, 0)) if per_block else (lambda i: (0, 0, 0))
    return pl.pallas_call(
        functools.partial(_merge0_kernel, tm=tm),
        grid=(nb,),
        in_specs=[pl.BlockSpec((1, tm, A_DIM), lambda i: (i, 0, 4)),
                  pl.BlockSpec((1, tm, A_DIM), lambda i: (i, 0, 5)),
                  pl.BlockSpec((1, tm, B_DIM), lambda i: (i, 0, 0)),
                  pl.BlockSpec((1, tm, d), lambda i: (i, 0, 0)),
                  pl.BlockSpec((A_GROUPS, CHUNK, CHUNK), lambda i: (0, 0, 0)),
                  pl.BlockSpec((A_GROUPS, CHUNK, CHUNK), lambda i: (0, 0, 0)),
                  pl.BlockSpec((A_DIM + B_DIM, d), lambda i: (0, 0)),
                  pl.BlockSpec((1, d), lambda i: (0, 0)),
                  pl.BlockSpec((1, 1, d), gate_map)],
        out_specs=pl.BlockSpec((1, tm, d), lambda i: (i, 0, 0)),
        out_shape=jax.ShapeDtypeStruct((nb, tm, d), F32),
        scratch_shapes=[pltpu.VMEM((tm, A_DIM + B_DIM), BF16)],
        compiler_params=_cparams(("arbitrary",), 32),
        name="merge0",
    )(p, p, mem, x, sgu_w, sgu_bb, w_out, post_g.reshape(1, d), gate)


FFN_TK = MXU_WIDTH
FFN_HALO = 64
FFN_MM_ROWS = 512


def _ffn_kernel(*refs, rows, tile_rows, width, grid2d, n_tiles):
    if grid2d:
        (x_ref, xu_ref, xd_ref, g_ref, sh_ref, sc_ref, wup_ref, cw_ref, wd_ref, pg_ref, gate_ref,
         o_ref, h_ref, gs0_ref, gs1_ref, act_ref) = refs
    else:
        (x_ref, g_ref, sh_ref, sc_ref, wup_ref, cw_ref, wd_ref, pg_ref, gate_ref,
         o_ref, h_ref, gs0_ref, gs1_ref, act_ref) = refs
    halo = FFN_HALO if grid2d else 0
    tk = FFN_TK
    ks = D_FF // tk
    hrows = rows + 2 * halo
    sub = 512

    def normmod(xv):
        return _rms(xv, g_ref[...]) * (1.0 + sc_ref[0]) + sh_ref[0]

    g_step = hrows // (hrows // FFN_MM_ROWS)
    g_chunks = [(r0, r0 + g_step) for r0 in range(0, hrows, g_step)]

    def gate_chunk(slab, gs_ref, r0, r1):
        gs_ref[r0:r1, :] = _dot(h_ref[r0:r1, :], wup_ref[ks + slab])

    def gate_branch(slab, gs_ref):
        for r0, r1 in g_chunks:
            gate_chunk(slab, gs_ref, r0, r1)

    pieces = [(halo + r0, halo + r0 + sub, lambda r0=r0: normmod(x_ref[0, r0:r0 + sub, :])) for r0 in range(0, rows, sub)]
    if grid2d:
        t = pl.program_id(0) % n_tiles
        pieces = ([(0, halo, lambda: normmod(xu_ref[0]) * (t > 0).astype(F32))] + pieces
                  + [(halo + rows, hrows, lambda: normmod(xd_ref[0]) * (t < n_tiles - 1).astype(F32))])
    pending = list(g_chunks)
    for p0, p1, piece in pieces:
        h_ref[p0:p1, :] = piece().astype(BF16)
        while pending and pending[0][1] <= p1:
            gate_chunk(0, gs0_ref, *pending.pop(0))
    assert not pending

    iota8 = lax.broadcasted_iota(jnp.int32, (8, tk), 0)

    def cut(vals, first):
        pieces = []
        for g0 in range(0, tile_rows, width):
            if first:
                pieces += [jnp.where(iota8 == 0, 0.0, vals[g0:g0 + 8]), vals[g0 + 8:g0 + width]]
            else:
                pieces += [vals[g0:g0 + width - 8], jnp.where(iota8 == 7, 0.0, vals[g0 + width - 8:g0 + width])]
        return jnp.concatenate(pieces, axis=0)

    def tiles(slab, gs_ref):
        cw = cw_ref[slab]
        for base in range(0, rows, tile_rows):
            if base % FFN_MM_ROWS == 0:
                a_rows = _dot(h_ref[halo + base:halo + base + FFN_MM_ROWS, :], wup_ref[slab])
            a = a_rows[base % FFN_MM_ROWS:base % FFN_MM_ROWS + tile_rows]
            dys = (0, 1, 2) if grid2d else (1,)
            g_rows = [gs_ref[base + dy * halo:base + dy * halo + tile_rows, :] for dy in dys]
            left, mid, right = (functools.reduce(lambda s, t: s + t,
                                                 [cw[3 * dy + dx:3 * dy + dx + 1] * g for dy, g in zip(dys, g_rows)])
                                for dx in range(3))
            conv = (mid + cut(pltpu.roll(left, 1, 0), True) + cut(pltpu.roll(right, tile_rows - 1, 0), False))
            t = jnp.tanh(conv * (0.7978845608028654 + (0.7978845608028654 * 0.044715) * (conv * conv)))
            p = a * conv
            act_ref[slab, base:base + tile_rows, :] = (p + p * t).astype(BF16)

    assert ks % 2 == 1

    def body(j, carry):
        k0 = 2 * j
        gate_branch(k0 + 1, gs1_ref)
        tiles(k0, gs0_ref)
        gate_branch(k0 + 2, gs0_ref)
        tiles(k0 + 1, gs1_ref)
        return carry

    lax.fori_loop(0, ks // 2, body, 0)
    tiles(ks - 1, gs0_ref)

    down_rows = min(rows, 2 * FFN_MM_ROWS)
    for r0 in range(0, rows, down_rows):
        for n0 in range(0, D_MODEL, tk):
            f = _dot(act_ref[0, r0:r0 + down_rows, :], wd_ref[0, :, n0:n0 + tk])
            for k in range(1, ks):
                f = f + _dot(act_ref[k, r0:r0 + down_rows, :], wd_ref[k, :, n0:n0 + tk])
            o_ref[0, r0:r0 + down_rows, n0:n0 + tk] = f
    for r0 in range(0, rows, sub):
        f = o_ref[0, r0:r0 + sub, :]
        o_ref[0, r0:r0 + sub, :] = x_ref[0, r0:r0 + sub, :] + gate_ref[0] * _rms(f, pg_ref[...])


def _ffn(x3, pre_g, shift, scale, w_up3, conv3, w_down3, post_g, gate, *, rows, width, grid2d):
    nb, seq, d = x3.shape
    tk = FFN_TK
    ks = D_FF // tk
    n_tiles = seq // rows
    tile_rows = max(width, 128)
    halo = FFN_HALO if grid2d else 0
    assert seq % rows == 0 and rows % 512 == 0 and tile_rows % width == 0 and (not grid2d or width == FFN_HALO)
    per_block = shift.shape[0] != 1
    mod_map = (lambda i: (i // n_tiles, 0, 0)) if per_block else (lambda i: (0, 0, 0))
    vec = lambda: pl.BlockSpec((1, d), lambda i: (0, 0))
    mod = lambda: pl.BlockSpec((1, 1, d), mod_map)
    resident = lambda shape: pl.BlockSpec(shape, lambda i: (0,) * len(shape), pipeline_mode=pl.Buffered(1))
    in_specs = [pl.BlockSpec((1, rows, d), lambda i: (i // n_tiles, i % n_tiles, 0))]
    args = [x3]
    if grid2d:
        per = rows // halo
        last = seq // halo - 1
        in_specs += [pl.BlockSpec((1, halo, d), lambda i: (i // n_tiles, jnp.maximum((i % n_tiles) * per - 1, 0), 0)),
                     pl.BlockSpec((1, halo, d), lambda i: (i // n_tiles, jnp.minimum((i % n_tiles + 1) * per, last), 0))]
        args += [x3, x3]
    in_specs += [vec(), mod(), mod(),
                 resident((2 * ks, d, tk)), resident((ks, 9, tk)), resident((ks, tk, d)),
                 vec(), mod()]
    args += [pre_g.reshape(1, d), shift, scale, w_up3, conv3, w_down3, post_g.reshape(1, d), gate]
    return pl.pallas_call(
        functools.partial(_ffn_kernel, rows=rows, tile_rows=tile_rows, width=width, grid2d=grid2d, n_tiles=n_tiles),
        grid=(nb * n_tiles,),
        in_specs=in_specs,
        out_specs=pl.BlockSpec((1, rows, d), lambda i: (i // n_tiles, i % n_tiles, 0)),
        out_shape=jax.ShapeDtypeStruct((nb, seq, d), F32),
        scratch_shapes=[pltpu.VMEM((rows + 2 * halo, d), BF16),
                        pltpu.VMEM((rows + 2 * halo, tk), F32),
                        pltpu.VMEM((rows + 2 * halo, tk), F32),
                        pltpu.VMEM((ks, rows, tk), BF16)],
        compiler_params=_cparams(("arbitrary",), 56),
        name="ffn",
    )(*args)


RET_CHUNK = 256


def _ret_kernel(k_ref, v_ref, q_ref, kc_ref, vc_ref, dec_ref, r_ref, rf_ref, rb_ref, *, seq, ctx_len):
    T = RET_CHUNK
    nc = seq // T
    t_col = lax.broadcasted_iota(jnp.int32, (T, 1), 0).astype(F32)
    lg = [_log_sigmoid(dec_ref[0, direction][0:1, 0:1]) for direction in range(2)]
    scale = C_QK_DIM ** -0.5
    zeta = (jnp.exp(lg[0] * (T - 1 - t_col)) * scale, jnp.exp(lg[1] * t_col) * scale)
    xi = (jnp.exp(lg[0] * (t_col + 1.0)), jnp.exp(lg[1] * (T - t_col)))
    gch = (jnp.exp(lg[0] * T), jnp.exp(lg[1] * T))
    row = lax.broadcasted_iota(jnp.int32, (T, T), 0)
    col = lax.broadcasted_iota(jnp.int32, (T, T), 1)
    rel = (row - col).astype(F32)
    d_intra = (jnp.where(row >= col, jnp.exp(lg[0] * rel), 0.0)
               + jnp.where(col >= row, jnp.exp(-lg[1] * rel), 0.0)) * scale

    def absorb(state_ref, direction, k, v, first):
        upd = _dot_tn((k.astype(F32) * zeta[direction]).astype(BF16), v)
        state_ref[...] = upd if first else gch[direction] * state_ref[...] + upd

    ncc = ctx_len // T
    for direction, state_ref in ((0, rf_ref), (1, rb_ref)):
        order = list(range(ncc)) if direction == 0 else list(range(ncc - 1, -1, -1))
        for n, c in enumerate(order):
            absorb(state_ref, direction, kc_ref[0, c * T:(c + 1) * T, :], vc_ref[0, c * T:(c + 1) * T, :], n == 0)

    def scan(i, first_visit):
        slf = pl.ds(pl.multiple_of(i * T, T), T)
        slb = pl.ds(pl.multiple_of((nc - 1 - i) * T, T), T)
        q, k, v = q_ref[0, slf, :], k_ref[0, slf, :], v_ref[0, slf, :]
        scores = _dot_nt(q, k) * d_intra
        fwd = _dot(scores.astype(BF16), v) + xi[0] * _dot(q, rf_ref[...].astype(BF16))
        absorb(rf_ref, 0, k, v, False)
        qb, kb, vb = q_ref[0, slb, :], k_ref[0, slb, :], v_ref[0, slb, :]
        bwd = xi[1] * _dot(qb, rb_ref[...].astype(BF16))
        absorb(rb_ref, 1, kb, vb, False)
        if first_visit:
            r_ref[0, slf, :] = fwd
            r_ref[0, slb, :] = bwd
        else:
            r_ref[0, slf, :] += fwd
            r_ref[0, slb, :] += bwd

    assert nc % 2 == 0

    def first_half(i, carry):
        scan(i, True)
        return carry

    def second_half(i, carry):
        scan(i, False)
        return carry

    lax.fori_loop(0, nc // 2, first_half, 0, unroll=True)
    lax.fori_loop(nc // 2, nc, second_half, 0, unroll=True)


def _ret(p, p_ctx, dec_b):
    b, seq, _ = p.shape
    ctx_len = p_ctx.shape[1]
    assert seq % RET_CHUNK == 0 and ctx_len % RET_CHUNK == 0
    return pl.pallas_call(
        functools.partial(_ret_kernel, seq=seq, ctx_len=ctx_len),
        grid=(b, C_HEADS),
        in_specs=[pl.BlockSpec((1, seq, C_QK_DIM), lambda i, h: (i, 0, h)),
                  pl.BlockSpec((1, seq, C_V_DIM), lambda i, h: (i, 0, 2 + h)),
                  pl.BlockSpec((1, seq, C_QK_DIM), lambda i, h: (i, 0, 12 + h)),
                  pl.BlockSpec((1, ctx_len, C_QK_DIM), lambda i, h: (i, 0, h)),
                  pl.BlockSpec((1, ctx_len, C_V_DIM), lambda i, h: (i, 0, 2 + h)),
                  pl.BlockSpec((1, 2, 8, LANES), lambda i, h: (h, 0, 0, 0))],
        out_specs=pl.BlockSpec((1, seq, C_V_DIM), lambda i, h: (i, 0, h)),
        out_shape=jax.ShapeDtypeStruct((b, seq, C_HEADS * C_V_DIM), F32),
        scratch_shapes=[pltpu.VMEM((C_QK_DIM, C_V_DIM), F32), pltpu.VMEM((C_QK_DIM, C_V_DIM), F32)],
        compiler_params=_cparams(("arbitrary", "arbitrary"), 48),
        name="ret",
    )(p, p, p, p_ctx, p_ctx, dec_b)


def _merge1_kernel(r_ref, g_ref, x_ref, hg_ref, wo_ref, pg_ref, gate_ref, out_ref, z_ref):
    for h in range(C_HEADS):
        hs = slice(h * C_V_DIM, (h + 1) * C_V_DIM)
        z = _silu(g_ref[0, :, hs].astype(F32)) * (_ln(r_ref[0, :, hs]) * hg_ref[:, hs])
        z_ref[:, hs] = z.astype(BF16)
    y = _dot(z_ref[...], wo_ref[...])
    out_ref[0] = x_ref[0] + gate_ref[0] * _rms(y, pg_ref[...])


def _merge1(r, p, x, head_g, w_out, post_g, gate, tm):
    nb = x.shape[0]
    d = D_MODEL
    hv = C_HEADS * C_V_DIM
    blocks_per_gate = nb // gate.shape[0]
    return pl.pallas_call(
        _merge1_kernel,
        grid=(nb,),
        in_specs=[pl.BlockSpec((1, tm, hv), lambda i: (i, 0, 0)),
                  pl.BlockSpec((1, tm, hv), lambda i: (i, 0, 2)),
                  pl.BlockSpec((1, tm, d), lambda i: (i, 0, 0)),
                  pl.BlockSpec((1, hv), lambda i: (0, 0)),
                  pl.BlockSpec((hv, d), lambda i: (0, 0)),
                  pl.BlockSpec((1, d), lambda i: (0, 0)),
                  pl.BlockSpec((1, 1, d), lambda i: (i // blocks_per_gate, 0, 0))],
        out_specs=pl.BlockSpec((1, tm, d), lambda i: (i, 0, 0)),
        out_shape=jax.ShapeDtypeStruct((nb, tm, d), F32),
        scratch_shapes=[pltpu.VMEM((tm, hv), BF16)],
        compiler_params=_cparams(("arbitrary",), 48),
        name="merge1",
    )(r, p, x, head_g, w_out, post_g.reshape(1, d), gate)


def kernel(x, c, ctx, c_ctx, ada_w, ada_b, pre_g, post_g, ffn_up, ffn_conv, ffn_down,
           ab_w_in, ab_qk_conv, ab_gate_b, ab_sgu_w, ab_sgu_b, ab_head_g, ab_w_out,
           ret_w_in, ret_decay, ret_head_g, ret_w_out):
    b, seq, d = x.shape
    ctx_len = ctx.shape[1]
    depth = ada_w.shape[0]
    tm = 512
    ctx_rows = min(2048, b * ctx_len)

    n_cond = b + 1
    n_pad = -n_cond % 8
    cs = jnp.concatenate([c, c_ctx[None, :], jnp.zeros((n_pad, d), F32)], axis=0)
    mods = _adaln(cs, ada_w, ada_b)

    def mod_l(layer, i):
        return mods[layer, :b, i * d:(i + 1) * d].reshape(b, 1, d)

    def mod_c(layer, i):
        return mods[layer, b:b + 1, i * d:(i + 1) * d].reshape(1, 1, d)

    n_slab = D_FF // FFN_TK
    ffn_up3 = ffn_up.astype(BF16).reshape(depth, d, 2 * n_slab, FFN_TK).transpose(0, 2, 1, 3)
    ffn_conv3 = ffn_conv.reshape(depth, 9, n_slab, FFN_TK).transpose(0, 2, 1, 3)
    ffn_down3 = (0.5 * ffn_down).astype(BF16).reshape(depth, n_slab, FFN_TK, d)

    def ffn_l(layer, xl):
        return _ffn(xl, pre_g[layer, 1], mod_l(layer, 3), mod_l(layer, 4), ffn_up3[layer], ffn_conv3[layer],
                    ffn_down3[layer], post_g[layer, 1], mod_l(layer, 5), rows=1024, width=GRID_W, grid2d=True)

    def ffn_c(layer, xc):
        out = _ffn(xc.reshape(1, b * ctx_len, d), pre_g[layer, 1], mod_c(layer, 3), mod_c(layer, 4), ffn_up3[layer],
                   ffn_conv3[layer], ffn_down3[layer], post_g[layer, 1], mod_c(layer, 5),
                   rows=min(1024, b * ctx_len), width=ctx_len, grid2d=False)
        return out.reshape(xc.shape)

    w_in = ab_w_in[0]
    g_lo = 2 * B_DIM
    g_hi = g_lo + N_GATES * B_HEADS
    w_main = jnp.concatenate([w_in[:, :g_lo], w_in[:, g_hi:]], axis=1).astype(BF16)
    w_gates = w_in[:, g_lo:g_hi].reshape(d, N_GATES, B_HEADS)
    lane_pad = ((0, 0), (0, LANES - N_CHAINS))
    w_gate = jnp.concatenate([jnp.pad(w_gates[:, 0::2].reshape(d, N_CHAINS), lane_pad),
                              jnp.pad(w_gates[:, 1::2].reshape(d, N_CHAINS), lane_pad)], axis=1).astype(BF16)
    bias_row = jnp.concatenate([jnp.pad(ab_gate_b[0][0::2].reshape(1, N_CHAINS), lane_pad),
                                jnp.pad(ab_gate_b[0][1::2].reshape(1, N_CHAINS), lane_pad)], axis=1)
    wq = ab_qk_conv[0][:, :B_DIM]
    wk = ab_qk_conv[0][:, B_DIM:]
    head_g0 = ab_head_g[0].reshape(1, B_DIM)
    sgu_w = ab_sgu_w[0].astype(BF16)
    sgu_bb = jnp.broadcast_to(ab_sgu_b[0][:, :, None], (A_GROUPS, CHUNK, CHUNK))
    w_out0 = ab_w_out[0].astype(BF16)

    p_c, gates_c = _proj(ctx.reshape(-1, ctx_rows, d), pre_g[0, 0], mod_c(0, 0), mod_c(0, 1), w_main, w_gate)
    p_l, gates_l = _proj(x, pre_g[0, 0], mod_l(0, 0), mod_l(0, 1), w_main, w_gate)
    p_c = p_c.reshape(b, ctx_len, -1)
    gates_c = gates_c.reshape(b, ctx_len, GATE_LANES)
    zero_state = (jnp.zeros((b, N_CHAINS, B_HEAD_DIM, 2 * B_HEAD_DIM), F32),
                  jnp.zeros((b, N_CHAINS, 1, B_HEAD_DIM), F32))
    mem_c, *ctx_state = _mlstm(p_c, gates_c, wq, wk, bias_row, head_g0, zero_state)
    mem_l, *_ = _mlstm(p_l, gates_l, wq, wk, bias_row, head_g0, tuple(ctx_state))

    rt = lambda t: t.reshape(-1, tm, t.shape[-1])
    x = _merge0(rt(p_l), rt(mem_l), rt(x), sgu_w, sgu_bb, w_out0, post_g[0, 0], mod_l(0, 2), tm).reshape(b, seq, d)
    ctx = _merge0(rt(p_c), rt(mem_c), rt(ctx), sgu_w, sgu_bb, w_out0, post_g[0, 0], mod_c(0, 2), tm).reshape(b, ctx_len, d)
    x = ffn_l(0, x)
    ctx = ffn_c(0, ctx)

    w_ret = ret_w_in[0].astype(BF16)
    kv_cols = C_HEADS * (C_QK_DIM + C_V_DIM)
    dec_b = jnp.broadcast_to(ret_decay[0].T[:, :, None, None], (C_HEADS, 2, 8, LANES))
    p_c = _proj(ctx.reshape(-1, ctx_rows, d), pre_g[1, 0], mod_c(1, 0), mod_c(1, 1), w_ret[:, :kv_cols])
    p_l = _proj(x, pre_g[1, 0], mod_l(1, 0), mod_l(1, 1), w_ret)
    r = _ret(p_l, p_c.reshape(b, ctx_len, -1), dec_b)
    x = _merge1(rt(r), rt(p_l), rt(x), ret_head_g[0].reshape(1, -1), ret_w_out[0].astype(BF16),
                post_g[1, 0], mod_l(1, 2), tm).reshape(b, seq, d)
    x = ffn_l(1, x)
    return x
```

```python
import functools

import jax
import jax.numpy as jnp
from jax import lax
from jax.experimental import pallas as pl
from jax.experimental.pallas import tpu as pltpu

F32 = jnp.float32
BF16 = jnp.bfloat16
HIGHEST = lax.Precision.HIGHEST

D_MODEL = 1024
GRID_W = 64
CHUNK = 128
EPS = 1e-6
N_MOD = 6
A_GROUPS = 4
A_DIM = 512
B_HEADS = 4
B_HEAD_DIM = 128
B_DIM = 512
N_GATES = 4
C_HEADS = 4
C_QK_DIM = 256
C_V_DIM = 512
D_FF = 2816

LANES = 128
MXU_WIDTH = 256
BF16_ROWS = 16
GATE_LANES = 2 * LANES
N_CHAINS = 2 * B_HEADS
ML_CHUNK = 256
V7X_VMEM_BYTES = 64 * 1024 * 1024
MIB = 1024 * 1024


def _cparams(semantics, vmem_mib):
    assert vmem_mib * MIB < V7X_VMEM_BYTES
    return pltpu.CompilerParams(dimension_semantics=semantics, vmem_limit_bytes=vmem_mib * MIB)


def _gelu(x):
    return x * (0.5 * (1.0 + jnp.tanh(0.7978845608028654 * (x + 0.044715 * (x * x * x)))))


def _silu(x):
    return x * jax.nn.sigmoid(x)


def _log_sigmoid(x):
    return jnp.minimum(x, 0.0) - jnp.log1p(jnp.exp(-jnp.abs(x)))


def _rms(x, g):
    return x * lax.rsqrt(jnp.mean(x * x, axis=-1, keepdims=True) + EPS) * g


def _ln(x):
    xc = x - jnp.mean(x, axis=-1, keepdims=True)
    return xc * lax.rsqrt(jnp.mean(xc * xc, axis=-1, keepdims=True) + EPS)


def _dot(a, b):
    return jnp.dot(a, b, preferred_element_type=F32)


def _dot_f32(a, b):
    return jnp.dot(a, b, precision=HIGHEST, preferred_element_type=F32)


def _dot_nt(a, b):
    return lax.dot_general(a, b, (((1,), (1,)), ((), ())), preferred_element_type=F32)


def _dot_tn(a, b):
    return lax.dot_general(a, b, (((0,), (0,)), ((), ())), preferred_element_type=F32)


def _adaln_kernel(c_ref, w_ref, b_ref, o_ref):
    s = _silu(c_ref[...])
    o_ref[0] = _dot(s.astype(BF16), w_ref[0].astype(BF16)) + b_ref[0]


def _adaln(cs, ada_w, ada_b):
    depth, d, n = ada_w.shape
    mp = cs.shape[0]
    tn = 1024
    return pl.pallas_call(
        _adaln_kernel,
        grid=(depth, n // tn),
        in_specs=[pl.BlockSpec((mp, d), lambda l, j: (0, 0)),
                  pl.BlockSpec((1, d, tn), lambda l, j: (l, 0, j)),
                  pl.BlockSpec((1, 1, tn), lambda l, j: (l, 0, j))],
        out_specs=pl.BlockSpec((1, mp, tn), lambda l, j: (l, 0, j)),
        out_shape=jax.ShapeDtypeStruct((depth, mp, n), F32),
        compiler_params=_cparams(("arbitrary", "arbitrary"), 32),
        name="adaln",
    )(cs, ada_w, ada_b.reshape(depth, 1, n))


def _proj_kernel(x_ref, g_ref, sh_ref, sc_ref, w_ref, *rest, rows, sub, with_gates):
    if with_gates:
        wg_ref, p_ref, gate_ref, h_ref = rest
    else:
        p_ref, h_ref = rest

    @pl.when(pl.program_id(1) == 0)
    def _():
        for r0 in range(0, rows, sub):
            h = _rms(x_ref[0, r0:r0 + sub, :], g_ref[...]) * (1.0 + sc_ref[0]) + sh_ref[0]
            hb = h.astype(BF16)
            h_ref[r0:r0 + sub, :] = hb
            if with_gates:
                gate_ref[0, r0:r0 + sub, :] = _dot(hb, wg_ref[...])
            p_ref[0, r0:r0 + sub, :] = _dot(hb, w_ref[...]).astype(BF16)

    @pl.when(pl.program_id(1) != 0)
    def _():
        tn = w_ref.shape[1]
        mm_rows = min(rows, 1024)
        for n0 in range(0, tn, MXU_WIDTH):
            for r0 in range(0, rows, mm_rows):
                p_ref[0, r0:r0 + mm_rows, n0:n0 + MXU_WIDTH] = _dot(h_ref[r0:r0 + mm_rows, :],
                                                                    w_ref[:, n0:n0 + MXU_WIDTH]).astype(BF16)


def _proj(x3, pre_g, shift, scale, w, wg=None):
    nb, rows, d = x3.shape
    n = w.shape[1]
    tn = 1024
    per_block = shift.shape[0] != 1
    mod_map = (lambda i, j: (i, 0, 0)) if per_block else (lambda i, j: (0, 0, 0))
    in_specs = [pl.BlockSpec((1, rows, d), lambda i, j: (i, 0, 0)),
                pl.BlockSpec((1, d), lambda i, j: (0, 0)),
                pl.BlockSpec((1, 1, d), mod_map),
                pl.BlockSpec((1, 1, d), mod_map),
                pl.BlockSpec((d, tn), lambda i, j: (0, j))]
    out_specs = [pl.BlockSpec((1, rows, tn), lambda i, j: (i, 0, j))]
    out_shape = [jax.ShapeDtypeStruct((nb, rows, n), BF16)]
    args = [x3, pre_g.reshape(1, d), shift, scale, w]
    if wg is not None:
        in_specs.append(pl.BlockSpec((d, GATE_LANES), lambda i, j: (0, 0)))
        out_specs.append(pl.BlockSpec((1, rows, GATE_LANES), lambda i, j: (i, 0, 0)))
        out_shape.append(jax.ShapeDtypeStruct((nb, rows, GATE_LANES), F32))
        args.append(wg)
    outs = pl.pallas_call(
        functools.partial(_proj_kernel, rows=rows, sub=min(512, rows), with_gates=wg is not None),
        grid=(nb, n // tn),
        in_specs=in_specs, out_specs=out_specs, out_shape=out_shape,
        scratch_shapes=[pltpu.VMEM((rows, d), BF16)],
        compiler_params=_cparams(("arbitrary", "arbitrary"), 48),
        name="proj",
    )(*args)
    return outs if wg is not None else outs[0]


def _mlstm_kernel(k_ref, v_ref, q_ref, o_ref, gates_ref, wq_ref, wk_ref, bias_ref, hg_ref, *rest, seq, zero_init):
    if zero_init:
        c0_ref = m0_ref = None
    else:
        c0_ref, m0_ref, *rest = rest
    mem_ref, cf_ref, mf_ref, pm_ref, cum_ref, brow_ref, kt_ref, qc_ref, rf_ref, rb_ref, st_ref = rest
    _mlstm_body(k_ref, v_ref, q_ref, o_ref, gates_ref, wq_ref, wk_ref, bias_ref, hg_ref, c0_ref, m0_ref,
                mem_ref, cf_ref, mf_ref, pm_ref, cum_ref, brow_ref, kt_ref, qc_ref, rf_ref, rb_ref, st_ref, seq=seq)


def _mlstm_body(k_ref, v_ref, q_ref, o_ref, gates_ref, wq_ref, wk_ref, bias_ref, hg_ref, c0_ref, m0_ref,
                mem_ref, cf_ref, mf_ref,
                pm_ref, cum_ref, brow_ref, kt_ref, qc_ref, rf_ref, rb_ref, st_ref, *, seq):
    T = ML_CHUNK
    hd = B_HEAD_DIM
    nc = seq // T
    row = lax.broadcasted_iota(jnp.int32, (T, T), 0)
    col = lax.broadcasted_iota(jnp.int32, (T, T), 1)
    lower = row >= col
    upper = col >= row
    fwd_lane = lax.broadcasted_iota(jnp.int32, (T, LANES), 1) < B_HEADS
    trow = lax.broadcasted_iota(jnp.int32, (T, LANES), 0)

    def gate_body(c, carry):
        sl = pl.ds(pl.multiple_of(c * T, T), T)
        g = gates_ref[0, sl, :] + bias_ref[...]
        lf = _log_sigmoid(g[:, LANES:])
        prefix = _dot_f32(lower.astype(F32), lf)
        cum = jnp.where(fwd_lane, prefix, prefix[T - 1:T] - prefix + lf)
        b = g[:, :LANES] - cum
        pf = b
        ps = b
        for sh in (1 << e for e in range(T.bit_length() - 1)):
            pf = jnp.where(trow >= sh, jnp.maximum(pf, pltpu.roll(pf, sh, 0)), pf)
            ps = jnp.where(trow < T - sh, jnp.maximum(ps, pltpu.roll(ps, T - sh, 0)), ps)
        pm_ref[sl, :] = jnp.where(fwd_lane, pf, ps)
        cum_ref[sl, :] = cum
        brow_ref[c] = b.T[:N_CHAINS]
        return carry

    ridx = lax.broadcasted_iota(jnp.int32, (T, hd), 0)

    def conv_body(c, carry):
        start = pl.multiple_of(c * T, T)
        sl = pl.ds(start, T)
        prev_start = pl.multiple_of(jnp.maximum(start - BF16_ROWS, 0), BF16_ROWS)
        next_start = pl.multiple_of(jnp.minimum(start + T, seq - BF16_ROWS), BF16_ROWS)
        has_prev = jnp.where(c > 0, 1.0, 0.0)
        has_next = jnp.where(c < nc - 1, 1.0, 0.0)
        for h in range(B_HEADS):
            hs = slice(h * hd, (h + 1) * hd)
            for src_ref, w_ref, is_k in ((k_ref, wk_ref, True), (q_ref, wq_ref, False)):
                cur = src_ref[0, sl, hs].astype(F32)
                prev_row = src_ref[0, pl.ds(prev_start, BF16_ROWS), hs].astype(F32)[BF16_ROWS - 1:] * has_prev
                next_row = src_ref[0, pl.ds(next_start, BF16_ROWS), hs].astype(F32)[0:1] * has_next
                w = w_ref[:, hs]
                xm = jnp.where(ridx == 0, prev_row, pltpu.roll(cur, 1, 0))
                xp = jnp.where(ridx == T - 1, next_row, pltpu.roll(cur, T - 1, 0))
                y = _silu(xm * w[0:1] + cur * w[1:2] + xp * w[2:3])
                if is_k:
                    kt_ref[h, c] = (y * hd ** -0.5).T.astype(BF16)
                else:
                    qc_ref[sl, hs] = y.astype(BF16)
        return carry

    lax.fori_loop(0, nc, lambda c, carry: conv_body(c, gate_body(c, carry)), 0)

    ones_tile = (lax.broadcasted_iota(jnp.int32, (T, hd), 1) == 0).astype(BF16)
    for j in range(N_CHAINS):
        st_ref[j] = jnp.zeros((hd, 2 * hd), F32) if c0_ref is None else c0_ref[0, j]
    m_init = tuple(jnp.zeros((1, 1), F32) if m0_ref is None else m0_ref[0, j][:, 0:1] for j in range(N_CHAINS))

    def scan_body(i, ms):
        chains = range(N_CHAINS)
        heads = [j % B_HEADS for j in chains]
        cs = [i if j < B_HEADS else nc - 1 - i for j in chains]
        sls = [pl.ds(pl.multiple_of(cs[j] * T, T), T) for j in chains]
        hss = [slice(h * hd, (h + 1) * hd) for h in heads]
        edge = [cs[j] * T + (T - 1 if j < B_HEADS else 0) for j in chains]
        b_row = [brow_ref[cs[j], j:j + 1, :] for j in chains]
        m_col = [jnp.maximum(ms[j], pm_ref[sls[j], j:j + 1]) for j in chains]
        m_wide = [jnp.broadcast_to(m_col[j], (T, T)) for j in chains]
        q = [qc_ref[sls[j], hss[j]] for j in chains]
        kt = [kt_ref[heads[j], cs[j]] for j in chains]
        dmat = [jnp.where(lower if j < B_HEADS else upper, jnp.exp(b_row[j] - m_wide[j]), 0.0) for j in chains]
        p = [(_dot(q[j], kt[j]) * dmat[j]).astype(BF16) for j in chains]
        v_aug = [jnp.concatenate([v_ref[0, sls[j], hss[j]], ones_tile], axis=1) for j in chains]
        st = [st_ref[j] for j in chains]
        qs = [(q[j].astype(F32) * jnp.exp(ms[j] - m_wide[j][:, :hd])).astype(BF16) for j in chains]
        acc = [_dot(p[j], v_aug[j]) + _dot(qs[j], st[j].astype(BF16)) for j in chains]
        for j in chains:
            out_ref = rf_ref if j < B_HEADS else rb_ref
            den = acc[j][:, hd:hd + 1]
            floor = jnp.exp(-(cum_ref[sls[j], j:j + 1] + m_col[j]))
            out_ref[sls[j], hss[j]] = acc[j][:, :hd] / jnp.maximum(jnp.abs(den), floor)
        new_ms = []
        for j in chains:
            btot = cum_ref[pl.ds(edge[j], 1), j:j + 1]
            m_far = jnp.maximum(ms[j], pm_ref[pl.ds(edge[j], 1), j:j + 1])
            m_new = btot + m_far
            kw_t = (kt[j].astype(F32) * jnp.exp(btot + b_row[j] - m_new)).astype(BF16)
            st_ref[j] = jnp.exp(ms[j] - m_far) * st[j] + _dot(kw_t, v_aug[j])
            new_ms.append(m_new)
        return tuple(new_ms)

    ms = lax.fori_loop(0, nc, scan_body, m_init, unroll=4)
    for j in range(N_CHAINS):
        cf_ref[0, j] = st_ref[j]
        mf_ref[0, j] = jnp.broadcast_to(ms[j], (1, hd))

    def out_body(c, carry):
        sl = pl.ds(pl.multiple_of(c * T, T), T)
        for h in range(B_HEADS):
            hs = slice(h * hd, (h + 1) * hd)
            y = jax.nn.sigmoid(o_ref[0, sl, hs].astype(F32)) * (rf_ref[sl, hs] + rb_ref[sl, hs])
            mem_ref[0, sl, hs] = (_ln(y) * hg_ref[:, hs]).astype(BF16)
        return carry

    lax.fori_loop(0, nc, out_body, 0)


def _mlstm(p, gates, wq, wk, bias_row, head_g, state=None):
    b, seq, _ = p.shape
    hd = B_HEAD_DIM
    state_shapes = [jax.ShapeDtypeStruct((b, N_CHAINS, hd, 2 * hd), F32), jax.ShapeDtypeStruct((b, N_CHAINS, 1, hd), F32)]
    assert seq % ML_CHUNK == 0
    nc = seq // ML_CHUNK
    sec = lambda idx: pl.BlockSpec((1, seq, B_DIM), lambda i, idx=idx: (i, 0, idx))
    full = lambda shape: pl.BlockSpec(shape, lambda i: (0,) * len(shape))
    st_c = pl.BlockSpec((1, N_CHAINS, hd, 2 * hd), lambda i: (i, 0, 0, 0))
    st_m = pl.BlockSpec((1, N_CHAINS, 1, hd), lambda i: (i, 0, 0, 0))
    return pl.pallas_call(
        functools.partial(_mlstm_kernel, seq=seq, zero_init=state is None),
        grid=(b,),
        in_specs=[sec(0), sec(1), sec(2), sec(3),
                  pl.BlockSpec((1, seq, GATE_LANES), lambda i: (i, 0, 0)),
                  full((3, B_DIM)), full((3, B_DIM)), full((1, GATE_LANES)), full((1, B_DIM))]
                 + ([] if state is None else [st_c, st_m]),
        out_specs=[pl.BlockSpec((1, seq, B_DIM), lambda i: (i, 0, 0)), st_c, st_m],
        out_shape=[jax.ShapeDtypeStruct((b, seq, B_DIM), BF16)] + state_shapes,
        scratch_shapes=[pltpu.VMEM((seq, LANES), F32), pltpu.VMEM((seq, LANES), F32),
                        pltpu.VMEM((nc, N_CHAINS, ML_CHUNK), F32),
                        pltpu.VMEM((B_HEADS, nc, hd, ML_CHUNK), BF16), pltpu.VMEM((seq, B_DIM), BF16),
                        pltpu.VMEM((seq, B_DIM), F32), pltpu.VMEM((seq, B_DIM), F32),
                        pltpu.VMEM((N_CHAINS, hd, 2 * hd), F32)],
        compiler_params=_cparams(("arbitrary",), 56),
        name="mlstm",
    )(p, p, p, p, gates, wq, wk, bias_row, head_g, *(() if state is None else state))


def _merge0_kernel(u_ref, va_ref, mem_ref, x_ref, sw_ref, sb_ref, wo_ref, pg_ref, gate_ref, out_ref, cat_ref, *, tm):
    T = CHUNK
    gd = A_DIM // A_GROUPS
    for c in range(tm // T):
        sl = slice(c * T, (c + 1) * T)
        vn = _ln(_gelu(va_ref[0, sl, :].astype(F32))).astype(BF16)
        u = _gelu(u_ref[0, sl, :].astype(F32))
        for g in range(A_GROUPS):
            gs = slice(g * gd, (g + 1) * gd)
            mixed = _dot(sw_ref[g], vn[:, gs]) + sb_ref[g]
            cat_ref[sl, gs] = (u[:, gs] * mixed).astype(BF16)
    cat_ref[:, A_DIM:] = mem_ref[0]
    y = _dot(cat_ref[...], wo_ref[...])
    out_ref[0] = x_ref[0] + gate_ref[0] * _rms(y, pg_ref[...])


def _merge0(p, mem, x, sgu_w, sgu_bb, w_out, post_g, gate, tm):
    nb = x.shape[0]
    d = D_MODEL
    per_block = gate.shape[0] != 1
    blocks_per_gate = nb // gate.shape[0]
    gate_map = (lambda i: (i // blocks_per_gate, 0, 0)) if per_block else (lambda i: (0, 0, 0))
    return pl.pallas_call(
        functools.partial(_merge0_kernel, tm=tm),
        grid=(nb,),
        in_specs=[pl.BlockSpec((1, tm, A_DIM), lambda i: (i, 0, 4)),
                  pl.BlockSpec((1, tm, A_DIM), lambda i: (i, 0, 5)),
                  pl.BlockSpec((1, tm, B_DIM), lambda i: (i, 0, 0)),
                  pl.BlockSpec((1, tm, d), lambda i: (i, 0, 0)),
                  pl.BlockSpec((A_GROUPS, CHUNK, CHUNK), lambda i: (0, 0, 0)),
                  pl.BlockSpec((A_GROUPS, CHUNK, CHUNK), lambda i: (0, 0, 0)),
                  pl.BlockSpec((A_DIM + B_DIM, d), lambda i: (0, 0)),
                  pl.BlockSpec((1, d), lambda i: (0, 0)),
                  pl.BlockSpec((1, 1, d), gate_map)],
        out_specs=pl.BlockSpec((1, tm, d), lambda i: (i, 0, 0)),
        out_shape=jax.ShapeDtypeStruct((nb, tm, d), F32),
        scratch_shapes=[pltpu.VMEM((tm, A_DIM + B_DIM), BF16)],
        compiler_params=_cparams(("arbitrary",), 32),
        name="merge0",
    )(p, p, mem, x, sgu_w, sgu_bb, w_out, post_g.reshape(1, d), gate)


FFN_TK = MXU_WIDTH
FFN_HALO = 64
FFN_MM_ROWS = 512


def _ffn_kernel(*refs, rows, tile_rows, width, grid2d, n_tiles):
    if grid2d:
        (x_ref, xu_ref, xd_ref, g_ref, sh_ref, sc_ref, wup_ref, cw_ref, wd_ref, pg_ref, gate_ref,
         o_ref, h_ref, gs0_ref, gs1_ref, act_ref) = refs
    else:
        (x_ref, g_ref, sh_ref, sc_ref, wup_ref, cw_ref, wd_ref, pg_ref, gate_ref,
         o_ref, h_ref, gs0_ref, gs1_ref, act_ref) = refs
    halo = FFN_HALO if grid2d else 0
    tk = FFN_TK
    ks = D_FF // tk
    hrows = rows + 2 * halo
    sub = 512

    def normmod(xv):
        return _rms(xv, g_ref[...]) * (1.0 + sc_ref[0]) + sh_ref[0]

    def up_slab(idx):
        return wup_ref[:, pl.ds(pl.multiple_of(idx * tk, tk), tk)]

    g_step = hrows // (hrows // FFN_MM_ROWS)
    g_chunks = [(r0, r0 + g_step) for r0 in range(0, hrows, g_step)]

    def gate_chunk(slab, gs_ref, r0, r1):
        gs_ref[r0:r1, :] = _dot(h_ref[r0:r1, :], up_slab(ks + slab))

    def gate_branch(slab, gs_ref):
        for r0, r1 in g_chunks:
            gate_chunk(slab, gs_ref, r0, r1)

    pieces = [(halo + r0, halo + r0 + sub, lambda r0=r0: normmod(x_ref[0, r0:r0 + sub, :])) for r0 in range(0, rows, sub)]
    if grid2d:
        t = pl.program_id(0) % n_tiles
        pieces = ([(0, halo, lambda: normmod(xu_ref[0]) * (t > 0).astype(F32))] + pieces
                  + [(halo + rows, hrows, lambda: normmod(xd_ref[0]) * (t < n_tiles - 1).astype(F32))])
    pending = list(g_chunks)
    for p0, p1, piece in pieces:
        h_ref[p0:p1, :] = piece().astype(BF16)
        while pending and pending[0][1] <= p1:
            gate_chunk(0, gs0_ref, *pending.pop(0))
    assert not pending

    iota8 = lax.broadcasted_iota(jnp.int32, (8, tk), 0)

    def cut(vals, first):
        pieces = []
        for g0 in range(0, tile_rows, width):
            if first:
                pieces += [jnp.where(iota8 == 0, 0.0, vals[g0:g0 + 8]), vals[g0 + 8:g0 + width]]
            else:
                pieces += [vals[g0:g0 + width - 8], jnp.where(iota8 == 7, 0.0, vals[g0 + width - 8:g0 + width])]
        return jnp.concatenate(pieces, axis=0)

    def tiles(slab, gs_ref):
        cw = cw_ref[slab]
        for base in range(0, rows, tile_rows):
            if base % FFN_MM_ROWS == 0:
                a_rows = _dot(h_ref[halo + base:halo + base + FFN_MM_ROWS, :], up_slab(slab))
            a = a_rows[base % FFN_MM_ROWS:base % FFN_MM_ROWS + tile_rows]
            dys = (0, 1, 2) if grid2d else (1,)
            g_rows = [gs_ref[base + dy * halo:base + dy * halo + tile_rows, :] for dy in dys]
            left, mid, right = (functools.reduce(lambda s, t: s + t,
                                                 [cw[3 * dy + dx:3 * dy + dx + 1] * g for dy, g in zip(dys, g_rows)])
                                for dx in range(3))
            conv = (mid + cut(pltpu.roll(left, 1, 0), True) + cut(pltpu.roll(right, tile_rows - 1, 0), False))
            t = jnp.tanh(conv * (0.7978845608028654 + (0.7978845608028654 * 0.044715) * (conv * conv)))
            p = a * conv
            act_ref[slab, base:base + tile_rows, :] = (p + p * t).astype(BF16)

    assert ks % 2 == 1

    def body(j, carry):
        k0 = 2 * j
        gate_branch(k0 + 1, gs1_ref)
        tiles(k0, gs0_ref)
        gate_branch(k0 + 2, gs0_ref)
        tiles(k0 + 1, gs1_ref)
        return carry

    lax.fori_loop(0, ks // 2, body, 0)
    tiles(ks - 1, gs0_ref)

    down_rows = min(rows, 2 * FFN_MM_ROWS)
    for r0 in range(0, rows, down_rows):
        for n0 in range(0, D_MODEL, tk):
            f = _dot(act_ref[0, r0:r0 + down_rows, :], wd_ref[0, :, n0:n0 + tk])
            for k in range(1, ks):
                f = f + _dot(act_ref[k, r0:r0 + down_rows, :], wd_ref[k, :, n0:n0 + tk])
            o_ref[0, r0:r0 + down_rows, n0:n0 + tk] = f
    for r0 in range(0, rows, sub):
        f = o_ref[0, r0:r0 + sub, :]
        o_ref[0, r0:r0 + sub, :] = x_ref[0, r0:r0 + sub, :] + gate_ref[0] * _rms(f, pg_ref[...])


def _ffn(x3, pre_g, shift, scale, w_up, conv3, w_down3, post_g, gate, *, rows, width, grid2d):
    nb, seq, d = x3.shape
    tk = FFN_TK
    ks = D_FF // tk
    n_tiles = seq // rows
    tile_rows = max(width, 128)
    halo = FFN_HALO if grid2d else 0
    assert seq % rows == 0 and rows % 512 == 0 and tile_rows % width == 0 and (not grid2d or width == FFN_HALO)
    per_block = shift.shape[0] != 1
    mod_map = (lambda i: (i // n_tiles, 0, 0)) if per_block else (lambda i: (0, 0, 0))
    vec = lambda: pl.BlockSpec((1, d), lambda i: (0, 0))
    mod = lambda: pl.BlockSpec((1, 1, d), mod_map)
    resident = lambda shape: pl.BlockSpec(shape, lambda i: (0,) * len(shape), pipeline_mode=pl.Buffered(1))
    in_specs = [pl.BlockSpec((1, rows, d), lambda i: (i // n_tiles, i % n_tiles, 0))]
    args = [x3]
    if grid2d:
        per = rows // halo
        last = seq // halo - 1
        in_specs += [pl.BlockSpec((1, halo, d), lambda i: (i // n_tiles, jnp.maximum((i % n_tiles) * per - 1, 0), 0)),
                     pl.BlockSpec((1, halo, d), lambda i: (i // n_tiles, jnp.minimum((i % n_tiles + 1) * per, last), 0))]
        args += [x3, x3]
    in_specs += [vec(), mod(), mod(),
                 resident((d, 2 * ks * tk)), resident((ks, 9, tk)), resident((ks, tk, d)),
                 vec(), mod()]
    args += [pre_g.reshape(1, d), shift, scale, w_up, conv3, w_down3, post_g.reshape(1, d), gate]
    return pl.pallas_call(
        functools.partial(_ffn_kernel, rows=rows, tile_rows=tile_rows, width=width, grid2d=grid2d, n_tiles=n_tiles),
        grid=(nb * n_tiles,),
        in_specs=in_specs,
        out_specs=pl.BlockSpec((1, rows, d), lambda i: (i // n_tiles, i % n_tiles, 0)),
        out_shape=jax.ShapeDtypeStruct((nb, seq, d), F32),
        scratch_shapes=[pltpu.VMEM((rows + 2 * halo, d), BF16),
                        pltpu.VMEM((rows + 2 * halo, tk), F32),
                        pltpu.VMEM((rows + 2 * halo, tk), F32),
                        pltpu.VMEM((ks, rows, tk), BF16)],
        compiler_params=_cparams(("arbitrary",), 56),
        name="ffn",
    )(*args)


RET_CHUNK = 256


def _ret_kernel(k_ref, v_ref, q_ref, kc_ref, vc_ref, dec_ref, r_ref, rf_ref, rb_ref, *, seq, ctx_len):
    T = RET_CHUNK
    nc = seq // T
    t_col = lax.broadcasted_iota(jnp.int32, (T, 1), 0).astype(F32)
    lg = [_log_sigmoid(dec_ref[0, direction][0:1, 0:1]) for direction in range(2)]
    scale = C_QK_DIM ** -0.5
    zeta = (jnp.exp(lg[0] * (T - 1 - t_col)) * scale, jnp.exp(lg[1] * t_col) * scale)
    xi = (jnp.exp(lg[0] * (t_col + 1.0)), jnp.exp(lg[1] * (T - t_col)))
    gch = (jnp.exp(lg[0] * T), jnp.exp(lg[1] * T))
    row = lax.broadcasted_iota(jnp.int32, (T, T), 0)
    col = lax.broadcasted_iota(jnp.int32, (T, T), 1)
    rel = (row - col).astype(F32)
    d_intra = (jnp.where(row >= col, jnp.exp(lg[0] * rel), 0.0)
               + jnp.where(col >= row, jnp.exp(-lg[1] * rel), 0.0)) * scale

    def absorb(state_ref, direction, k, v, first):
        upd = _dot_tn((k.astype(F32) * zeta[direction]).astype(BF16), v)
        state_ref[...] = upd if first else gch[direction] * state_ref[...] + upd

    ncc = ctx_len // T
    for direction, state_ref in ((0, rf_ref), (1, rb_ref)):
        order = list(range(ncc)) if direction == 0 else list(range(ncc - 1, -1, -1))
        for n, c in enumerate(order):
            absorb(state_ref, direction, kc_ref[0, c * T:(c + 1) * T, :], vc_ref[0, c * T:(c + 1) * T, :], n == 0)

    def scan(i, first_visit):
        slf = pl.ds(pl.multiple_of(i * T, T), T)
        slb = pl.ds(pl.multiple_of((nc - 1 - i) * T, T), T)
        q, k, v = q_ref[0, slf, :], k_ref[0, slf, :], v_ref[0, slf, :]
        scores = _dot_nt(q, k) * d_intra
        fwd = _dot(scores.astype(BF16), v) + xi[0] * _dot(q, rf_ref[...].astype(BF16))
        absorb(rf_ref, 0, k, v, False)
        qb, kb, vb = q_ref[0, slb, :], k_ref[0, slb, :], v_ref[0, slb, :]
        bwd = xi[1] * _dot(qb, rb_ref[...].astype(BF16))
        absorb(rb_ref, 1, kb, vb, False)
        if first_visit:
            r_ref[0, slf, :] = fwd
            r_ref[0, slb, :] = bwd
        else:
            r_ref[0, slf, :] += fwd
            r_ref[0, slb, :] += bwd

    assert nc % 2 == 0

    def first_half(i, carry):
        scan(i, True)
        return carry

    def second_half(i, carry):
        scan(i, False)
        return carry

    lax.fori_loop(0, nc // 2, first_half, 0, unroll=True)
    lax.fori_loop(nc // 2, nc, second_half, 0, unroll=True)


def _ret(p, p_ctx, dec_b):
    b, seq, _ = p.shape
    ctx_len = p_ctx.shape[1]
    assert seq % RET_CHUNK == 0 and ctx_len % RET_CHUNK == 0
    return pl.pallas_call(
        functools.partial(_ret_kernel, seq=seq, ctx_len=ctx_len),
        grid=(b, C_HEADS),
        in_specs=[pl.BlockSpec((1, seq, C_QK_DIM), lambda i, h: (i, 0, h)),
                  pl.BlockSpec((1, seq, C_V_DIM), lambda i, h: (i, 0, 2 + h)),
                  pl.BlockSpec((1, seq, C_QK_DIM), lambda i, h: (i, 0, 12 + h)),
                  pl.BlockSpec((1, ctx_len, C_QK_DIM), lambda i, h: (i, 0, h)),
                  pl.BlockSpec((1, ctx_len, C_V_DIM), lambda i, h: (i, 0, 2 + h)),
                  pl.BlockSpec((1, 2, 8, LANES), lambda i, h: (h, 0, 0, 0))],
        out_specs=pl.BlockSpec((1, seq, C_V_DIM), lambda i, h: (i, 0, h)),
        out_shape=jax.ShapeDtypeStruct((b, seq, C_HEADS * C_V_DIM), F32),
        scratch_shapes=[pltpu.VMEM((C_QK_DIM, C_V_DIM), F32), pltpu.VMEM((C_QK_DIM, C_V_DIM), F32)],
        compiler_params=_cparams(("arbitrary", "arbitrary"), 48),
        name="ret",
    )(p, p, p, p_ctx, p_ctx, dec_b)


def _merge1_kernel(r_ref, g_ref, x_ref, hg_ref, wo_ref, pg_ref, gate_ref, out_ref, z_ref):
    for h in range(C_HEADS):
        hs = slice(h * C_V_DIM, (h + 1) * C_V_DIM)
        z = _silu(g_ref[0, :, hs].astype(F32)) * (_ln(r_ref[0, :, hs]) * hg_ref[:, hs])
        z_ref[:, hs] = z.astype(BF16)
    y = _dot(z_ref[...], wo_ref[...])
    out_ref[0] = x_ref[0] + gate_ref[0] * _rms(y, pg_ref[...])


def _merge1(r, p, x, head_g, w_out, post_g, gate, tm):
    nb = x.shape[0]
    d = D_MODEL
    hv = C_HEADS * C_V_DIM
    blocks_per_gate = nb // gate.shape[0]
    return pl.pallas_call(
        _merge1_kernel,
        grid=(nb,),
        in_specs=[pl.BlockSpec((1, tm, hv), lambda i: (i, 0, 0)),
                  pl.BlockSpec((1, tm, hv), lambda i: (i, 0, 2)),
                  pl.BlockSpec((1, tm, d), lambda i: (i, 0, 0)),
                  pl.BlockSpec((1, hv), lambda i: (0, 0)),
                  pl.BlockSpec((hv, d), lambda i: (0, 0)),
                  pl.BlockSpec((1, d), lambda i: (0, 0)),
                  pl.BlockSpec((1, 1, d), lambda i: (i // blocks_per_gate, 0, 0))],
        out_specs=pl.BlockSpec((1, tm, d), lambda i: (i, 0, 0)),
        out_shape=jax.ShapeDtypeStruct((nb, tm, d), F32),
        scratch_shapes=[pltpu.VMEM((tm, hv), BF16)],
        compiler_params=_cparams(("arbitrary",), 48),
        name="merge1",
    )(r, p, x, head_g, w_out, post_g.reshape(1, d), gate)


def kernel(x, c, ctx, c_ctx, ada_w, ada_b, pre_g, post_g, ffn_up, ffn_conv, ffn_down,
           ab_w_in, ab_qk_conv, ab_gate_b, ab_sgu_w, ab_sgu_b, ab_head_g, ab_w_out,
           ret_w_in, ret_decay, ret_head_g, ret_w_out):
    b, seq, d = x.shape
    ctx_len = ctx.shape[1]
    depth = ada_w.shape[0]
    tm = 512
    ctx_rows = min(2048, b * ctx_len)

    n_cond = b + 1
    n_pad = -n_cond % 8
    cs = jnp.concatenate([c, c_ctx[None, :], jnp.zeros((n_pad, d), F32)], axis=0)
    mods = _adaln(cs, ada_w, ada_b)

    def mod_l(layer, i):
        return mods[layer, :b, i * d:(i + 1) * d].reshape(b, 1, d)

    def mod_c(layer, i):
        return mods[layer, b:b + 1, i * d:(i + 1) * d].reshape(1, 1, d)

    n_slab = D_FF // FFN_TK
    ffn_up_b = ffn_up.astype(BF16)
    ffn_conv3 = ffn_conv.reshape(depth, 9, n_slab, FFN_TK).transpose(0, 2, 1, 3)
    ffn_down3 = (0.5 * ffn_down).astype(BF16).reshape(depth, n_slab, FFN_TK, d)

    def ffn_l(layer, xl):
        return _ffn(xl, pre_g[layer, 1], mod_l(layer, 3), mod_l(layer, 4), ffn_up_b[layer], ffn_conv3[layer],
                    ffn_down3[layer], post_g[layer, 1], mod_l(layer, 5), rows=1024, width=GRID_W, grid2d=True)

    def ffn_c(layer, xc):
        out = _ffn(xc.reshape(1, b * ctx_len, d), pre_g[layer, 1], mod_c(layer, 3), mod_c(layer, 4), ffn_up_b[layer],
                   ffn_conv3[layer], ffn_down3[layer], post_g[layer, 1], mod_c(layer, 5),
                   rows=min(1024, b * ctx_len), width=ctx_len, grid2d=False)
        return out.reshape(xc.shape)

    w_in = ab_w_in[0]
    g_lo = 2 * B_DIM
    g_hi = g_lo + N_GATES * B_HEADS
    w_main = jnp.concatenate([w_in[:, :g_lo], w_in[:, g_hi:]], axis=1).astype(BF16)
    w_gates = w_in[:, g_lo:g_hi].reshape(d, N_GATES, B_HEADS)
    lane_pad = ((0, 0), (0, LANES - N_CHAINS))
    w_gate = jnp.concatenate([jnp.pad(w_gates[:, 0::2].reshape(d, N_CHAINS), lane_pad),
                              jnp.pad(w_gates[:, 1::2].reshape(d, N_CHAINS), lane_pad)], axis=1).astype(BF16)
    bias_row = jnp.concatenate([jnp.pad(ab_gate_b[0][0::2].reshape(1, N_CHAINS), lane_pad),
                                jnp.pad(ab_gate_b[0][1::2].reshape(1, N_CHAINS), lane_pad)], axis=1)
    wq = ab_qk_conv[0][:, :B_DIM]
    wk = ab_qk_conv[0][:, B_DIM:]
    head_g0 = ab_head_g[0].reshape(1, B_DIM)
    sgu_w = ab_sgu_w[0].astype(BF16)
    sgu_bb = jnp.broadcast_to(ab_sgu_b[0][:, :, None], (A_GROUPS, CHUNK, CHUNK))
    w_out0 = ab_w_out[0].astype(BF16)

    p_c, gates_c = _proj(ctx.reshape(-1, ctx_rows, d), pre_g[0, 0], mod_c(0, 0), mod_c(0, 1), w_main, w_gate)
    p_l, gates_l = _proj(x, pre_g[0, 0], mod_l(0, 0), mod_l(0, 1), w_main, w_gate)
    p_c = p_c.reshape(b, ctx_len, -1)
    gates_c = gates_c.reshape(b, ctx_len, GATE_LANES)
    mem_c, *ctx_state = _mlstm(p_c, gates_c, wq, wk, bias_row, head_g0)
    mem_l, *_ = _mlstm(p_l, gates_l, wq, wk, bias_row, head_g0, tuple(ctx_state))

    rt = lambda t: t.reshape(-1, tm, t.shape[-1])
    x = _merge0(rt(p_l), rt(mem_l), rt(x), sgu_w, sgu_bb, w_out0, post_g[0, 0], mod_l(0, 2), tm).reshape(b, seq, d)
    ctx = _merge0(rt(p_c), rt(mem_c), rt(ctx), sgu_w, sgu_bb, w_out0, post_g[0, 0], mod_c(0, 2), tm).reshape(b, ctx_len, d)
    x = ffn_l(0, x)
    ctx = ffn_c(0, ctx)

    w_ret = ret_w_in[0].astype(BF16)
    kv_cols = C_HEADS * (C_QK_DIM + C_V_DIM)
    dec_b = jnp.broadcast_to(ret_decay[0].T[:, :, None, None], (C_HEADS, 2, 8, LANES))
    p_c = _proj(ctx.reshape(-1, ctx_rows, d), pre_g[1, 0], mod_c(1, 0), mod_c(1, 1), w_ret[:, :kv_cols])
    p_l = _proj(x, pre_g[1, 0], mod_l(1, 0), mod_l(1, 1), w_ret)
    r = _ret(p_l, p_c.reshape(b, ctx_len, -1), dec_b)
    x = _merge1(rt(r), rt(p_l), rt(x), ret_head_g[0].reshape(1, -1), ret_w_out[0].astype(BF16),
                post_g[1, 0], mod_l(1, 2), tm).reshape(b, seq, d)
    x = ffn_l(1, x)
    return x
```

```python
import functools

import jax
import jax.numpy as jnp
from jax import lax
from jax.experimental import pallas as pl
from jax.experimental.pallas import tpu as pltpu

F32 = jnp.float32
BF16 = jnp.bfloat16
HIGHEST = lax.Precision.HIGHEST

D_MODEL = 1024
GRID_W = 64
CHUNK = 128
EPS = 1e-6
N_MOD = 6
A_GROUPS = 4
A_DIM = 512
B_HEADS = 4
B_HEAD_DIM = 128
B_DIM = 512
N_GATES = 4
C_HEADS = 4
C_QK_DIM = 256
C_V_DIM = 512
D_FF = 2816

LANES = 128
MXU_WIDTH = 256
BF16_ROWS = 16
GATE_LANES = 2 * LANES
N_CHAINS = 2 * B_HEADS
ML_CHUNK = 256
V7X_VMEM_BYTES = 64 * 1024 * 1024
MIB = 1024 * 1024


def _cparams(semantics, vmem_mib):
    assert vmem_mib * MIB < V7X_VMEM_BYTES
    return pltpu.CompilerParams(dimension_semantics=semantics, vmem_limit_bytes=vmem_mib * MIB)


def _gelu(x):
    return x * (0.5 * (1.0 + jnp.tanh(0.7978845608028654 * (x + 0.044715 * (x * x * x)))))


def _silu(x):
    return x * jax.nn.sigmoid(x)


def _log_sigmoid(x):
    return jnp.minimum(x, 0.0) - jnp.log1p(jnp.exp(-jnp.abs(x)))


def _rms(x, g):
    return x * lax.rsqrt(jnp.mean(x * x, axis=-1, keepdims=True) + EPS) * g


def _ln(x):
    xc = x - jnp.mean(x, axis=-1, keepdims=True)
    return xc * lax.rsqrt(jnp.mean(xc * xc, axis=-1, keepdims=True) + EPS)


def _dot(a, b):
    return jnp.dot(a, b, preferred_element_type=F32)


def _dot_f32(a, b):
    return jnp.dot(a, b, precision=HIGHEST, preferred_element_type=F32)


def _dot_nt(a, b):
    return lax.dot_general(a, b, (((1,), (1,)), ((), ())), preferred_element_type=F32)


def _dot_tn(a, b):
    return lax.dot_general(a, b, (((0,), (0,)), ((), ())), preferred_element_type=F32)


def _adaln_kernel(c_ref, w_ref, b_ref, o_ref):
    s = _silu(c_ref[...])
    o_ref[0] = _dot(s.astype(BF16), w_ref[0].astype(BF16)) + b_ref[0]


def _adaln(cs, ada_w, ada_b):
    depth, d, n = ada_w.shape
    mp = cs.shape[0]
    tn = 1024
    return pl.pallas_call(
        _adaln_kernel,
        grid=(depth, n // tn),
        in_specs=[pl.BlockSpec((mp, d), lambda l, j: (0, 0)),
                  pl.BlockSpec((1, d, tn), lambda l, j: (l, 0, j)),
                  pl.BlockSpec((1, 1, tn), lambda l, j: (l, 0, j))],
        out_specs=pl.BlockSpec((1, mp, tn), lambda l, j: (l, 0, j)),
        out_shape=jax.ShapeDtypeStruct((depth, mp, n), F32),
        compiler_params=_cparams(("arbitrary", "arbitrary"), 32),
        name="adaln",
    )(cs, ada_w, ada_b.reshape(depth, 1, n))


def _proj_kernel(x_ref, g_ref, sh_ref, sc_ref, w_ref, *rest, rows, sub, with_gates):
    if with_gates:
        wg_ref, p_ref, gate_ref, h_ref = rest
    else:
        p_ref, h_ref = rest

    @pl.when(pl.program_id(1) == 0)
    def _():
        for r0 in range(0, rows, sub):
            h = _rms(x_ref[0, r0:r0 + sub, :], g_ref[...]) * (1.0 + sc_ref[0]) + sh_ref[0]
            hb = h.astype(BF16)
            h_ref[r0:r0 + sub, :] = hb
            if with_gates:
                gate_ref[0, r0:r0 + sub, :] = _dot(hb, wg_ref[...])
            p_ref[0, r0:r0 + sub, :] = _dot(hb, w_ref[...]).astype(BF16)

    @pl.when(pl.program_id(1) != 0)
    def _():
        tn = w_ref.shape[1]
        mm_rows = min(rows, 1024)
        for n0 in range(0, tn, MXU_WIDTH):
            for r0 in range(0, rows, mm_rows):
                p_ref[0, r0:r0 + mm_rows, n0:n0 + MXU_WIDTH] = _dot(h_ref[r0:r0 + mm_rows, :],
                                                                    w_ref[:, n0:n0 + MXU_WIDTH]).astype(BF16)


def _proj(x3, pre_g, shift, scale, w, wg=None):
    nb, rows, d = x3.shape
    n = w.shape[1]
    tn = 1024
    per_block = shift.shape[0] != 1
    mod_map = (lambda i, j: (i, 0, 0)) if per_block else (lambda i, j: (0, 0, 0))
    in_specs = [pl.BlockSpec((1, rows, d), lambda i, j: (i, 0, 0)),
                pl.BlockSpec((1, d), lambda i, j: (0, 0)),
                pl.BlockSpec((1, 1, d), mod_map),
                pl.BlockSpec((1, 1, d), mod_map),
                pl.BlockSpec((d, tn), lambda i, j: (0, j))]
    out_specs = [pl.BlockSpec((1, rows, tn), lambda i, j: (i, 0, j))]
    out_shape = [jax.ShapeDtypeStruct((nb, rows, n), BF16)]
    args = [x3, pre_g.reshape(1, d), shift, scale, w]
    if wg is not None:
        in_specs.append(pl.BlockSpec((d, GATE_LANES), lambda i, j: (0, 0)))
        out_specs.append(pl.BlockSpec((1, rows, GATE_LANES), lambda i, j: (i, 0, 0)))
        out_shape.append(jax.ShapeDtypeStruct((nb, rows, GATE_LANES), F32))
        args.append(wg)
    outs = pl.pallas_call(
        functools.partial(_proj_kernel, rows=rows, sub=min(512, rows), with_gates=wg is not None),
        grid=(nb, n // tn),
        in_specs=in_specs, out_specs=out_specs, out_shape=out_shape,
        scratch_shapes=[pltpu.VMEM((rows, d), BF16)],
        compiler_params=_cparams(("arbitrary", "arbitrary"), 48),
        name="proj",
    )(*args)
    return outs if wg is not None else outs[0]


def _mlstm_kernel(k_ref, v_ref, q_ref, o_ref, gates_ref, wq_ref, wk_ref, bias_ref, hg_ref, *rest, seq, zero_init):
    if zero_init:
        c0_ref = m0_ref = None
    else:
        c0_ref, m0_ref, *rest = rest
    mem_ref, cf_ref, mf_ref, pm_ref, cum_ref, brow_ref, kt_ref, qc_ref, rf_ref, rb_ref, st_ref = rest
    _mlstm_body(k_ref, v_ref, q_ref, o_ref, gates_ref, wq_ref, wk_ref, bias_ref, hg_ref, c0_ref, m0_ref,
                mem_ref, cf_ref, mf_ref, pm_ref, cum_ref, brow_ref, kt_ref, qc_ref, rf_ref, rb_ref, st_ref, seq=seq)


def _mlstm_body(k_ref, v_ref, q_ref, o_ref, gates_ref, wq_ref, wk_ref, bias_ref, hg_ref, c0_ref, m0_ref,
                mem_ref, cf_ref, mf_ref,
                pm_ref, cum_ref, brow_ref, kt_ref, qc_ref, rf_ref, rb_ref, st_ref, *, seq):
    T = ML_CHUNK
    hd = B_HEAD_DIM
    nc = seq // T
    row = lax.broadcasted_iota(jnp.int32, (T, T), 0)
    col = lax.broadcasted_iota(jnp.int32, (T, T), 1)
    lower = row >= col
    upper = col >= row
    fwd_lane = lax.broadcasted_iota(jnp.int32, (T, LANES), 1) < B_HEADS
    trow = lax.broadcasted_iota(jnp.int32, (T, LANES), 0)

    def gate_body(c, carry):
        sl = pl.ds(pl.multiple_of(c * T, T), T)
        g = gates_ref[0, sl, :] + bias_ref[...]
        lf = _log_sigmoid(g[:, LANES:])
        prefix = _dot_f32(lower.astype(F32), lf)
        cum = jnp.where(fwd_lane, prefix, prefix[T - 1:T] - prefix + lf)
        b = g[:, :LANES] - cum
        pf = b
        ps = b
        for sh in (1 << e for e in range(T.bit_length() - 1)):
            pf = jnp.where(trow >= sh, jnp.maximum(pf, pltpu.roll(pf, sh, 0)), pf)
            ps = jnp.where(trow < T - sh, jnp.maximum(ps, pltpu.roll(ps, T - sh, 0)), ps)
        pm_ref[sl, :] = jnp.where(fwd_lane, pf, ps)
        cum_ref[sl, :] = cum
        brow_ref[c] = b.T[:N_CHAINS]
        return carry

    ridx = lax.broadcasted_iota(jnp.int32, (T, hd), 0)

    def conv_body(c, carry):
        start = pl.multiple_of(c * T, T)
        sl = pl.ds(start, T)
        prev_start = pl.multiple_of(jnp.maximum(start - BF16_ROWS, 0), BF16_ROWS)
        next_start = pl.multiple_of(jnp.minimum(start + T, seq - BF16_ROWS), BF16_ROWS)
        has_prev = jnp.where(c > 0, 1.0, 0.0)
        has_next = jnp.where(c < nc - 1, 1.0, 0.0)
        for h in range(B_HEADS):
            hs = slice(h * hd, (h + 1) * hd)
            for src_ref, w_ref, is_k in ((k_ref, wk_ref, True), (q_ref, wq_ref, False)):
                cur = src_ref[0, sl, hs].astype(F32)
                prev_row = src_ref[0, pl.ds(prev_start, BF16_ROWS), hs].astype(F32)[BF16_ROWS - 1:] * has_prev
                next_row = src_ref[0, pl.ds(next_start, BF16_ROWS), hs].astype(F32)[0:1] * has_next
                w = w_ref[:, hs]
                xm = jnp.where(ridx == 0, prev_row, pltpu.roll(cur, 1, 0))
                xp = jnp.where(ridx == T - 1, next_row, pltpu.roll(cur, T - 1, 0))
                y = _silu(xm * w[0:1] + cur * w[1:2] + xp * w[2:3])
                if is_k:
                    kt_ref[h, c] = (y * hd ** -0.5).T.astype(BF16)
                else:
                    qc_ref[sl, hs] = y.astype(BF16)
        return carry

    lax.fori_loop(0, nc, lambda c, carry: conv_body(c, gate_body(c, carry)), 0)

    ones_tile = (lax.broadcasted_iota(jnp.int32, (T, hd), 1) == 0).astype(BF16)
    for j in range(N_CHAINS):
        st_ref[j] = jnp.zeros((hd, 2 * hd), F32) if c0_ref is None else c0_ref[0, j]
    m_init = tuple(jnp.zeros((1, 1), F32) if m0_ref is None else m0_ref[0, j][:, 0:1] for j in range(N_CHAINS))

    def scan_body(i, ms):
        chains = range(N_CHAINS)
        heads = [j % B_HEADS for j in chains]
        cs = [i if j < B_HEADS else nc - 1 - i for j in chains]
        sls = [pl.ds(pl.multiple_of(cs[j] * T, T), T) for j in chains]
        hss = [slice(h * hd, (h + 1) * hd) for h in heads]
        edge = [cs[j] * T + (T - 1 if j < B_HEADS else 0) for j in chains]
        b_row = [brow_ref[cs[j], j:j + 1, :] for j in chains]
        m_col = [jnp.maximum(ms[j], pm_ref[sls[j], j:j + 1]) for j in chains]
        m_wide = [jnp.broadcast_to(m_col[j], (T, T)) for j in chains]
        q = [qc_ref[sls[j], hss[j]] for j in chains]
        kt = [kt_ref[heads[j], cs[j]] for j in chains]
        dmat = [jnp.where(lower if j < B_HEADS else upper, jnp.exp(b_row[j] - m_wide[j]), 0.0) for j in chains]
        p = [(_dot(q[j], kt[j]) * dmat[j]).astype(BF16) for j in chains]
        v_aug = [jnp.concatenate([v_ref[0, sls[j], hss[j]], ones_tile], axis=1) for j in chains]
        st = [st_ref[j] for j in chains]
        qs = [(q[j].astype(F32) * jnp.exp(ms[j] - m_wide[j][:, :hd])).astype(BF16) for j in chains]
        acc = [_dot(p[j], v_aug[j]) + _dot(qs[j], st[j].astype(BF16)) for j in chains]
        for j in chains:
            out_ref = rf_ref if j < B_HEADS else rb_ref
            den = acc[j][:, hd:hd + 1]
            floor = jnp.exp(-(cum_ref[sls[j], j:j + 1] + m_col[j]))
            out_ref[sls[j], hss[j]] = acc[j][:, :hd] / jnp.maximum(jnp.abs(den), floor)
        new_ms = []
        for j in chains:
            btot = cum_ref[pl.ds(edge[j], 1), j:j + 1]
            m_far = jnp.maximum(ms[j], pm_ref[pl.ds(edge[j], 1), j:j + 1])
            m_new = btot + m_far
            kw_t = (kt[j].astype(F32) * jnp.exp(btot + b_row[j] - m_new)).astype(BF16)
            st_ref[j] = jnp.exp(ms[j] - m_far) * st[j] + _dot(kw_t, v_aug[j])
            new_ms.append(m_new)
        return tuple(new_ms)

    ms = lax.fori_loop(0, nc, scan_body, m_init, unroll=4)
    for j in range(N_CHAINS):
        cf_ref[0, j] = st_ref[j]
        mf_ref[0, j] = jnp.broadcast_to(ms[j], (1, hd))

    def out_body(c, carry):
        sl = pl.ds(pl.multiple_of(c * T, T), T)
        for h in range(B_HEADS):
            hs = slice(h * hd, (h + 1) * hd)
            y = jax.nn.sigmoid(o_ref[0, sl, hs].astype(F32)) * (rf_ref[sl, hs] + rb_ref[sl, hs])
            mem_ref[0, sl, hs] = (_ln(y) * hg_ref[:, hs]).astype(BF16)
        return carry

    lax.fori_loop(0, nc, out_body, 0)


def _mlstm(p, gates, wq, wk, bias_row, head_g, state=None):
    b, seq, _ = p.shape
    hd = B_HEAD_DIM
    state_shapes = [jax.ShapeDtypeStruct((b, N_CHAINS, hd, 2 * hd), F32), jax.ShapeDtypeStruct((b, N_CHAINS, 1, hd), F32)]
    assert seq % ML_CHUNK == 0
    nc = seq // ML_CHUNK
    sec = lambda idx: pl.BlockSpec((1, seq, B_DIM), lambda i, idx=idx: (i, 0, idx))
    full = lambda shape: pl.BlockSpec(shape, lambda i: (0,) * len(shape))
    st_c = pl.BlockSpec((1, N_CHAINS, hd, 2 * hd), lambda i: (i, 0, 0, 0))
    st_m = pl.BlockSpec((1, N_CHAINS, 1, hd), lambda i: (i, 0, 0, 0))
    return pl.pallas_call(
        functools.partial(_mlstm_kernel, seq=seq, zero_init=state is None),
        grid=(b,),
        in_specs=[sec(0), sec(1), sec(2), sec(3),
                  pl.BlockSpec((1, seq, GATE_LANES), lambda i: (i, 0, 0)),
                  full((3, B_DIM)), full((3, B_DIM)), full((1, GATE_LANES)), full((1, B_DIM))]
                 + ([] if state is None else [st_c, st_m]),
        out_specs=[pl.BlockSpec((1, seq, B_DIM), lambda i: (i, 0, 0)), st_c, st_m],
        out_shape=[jax.ShapeDtypeStruct((b, seq, B_DIM), BF16)] + state_shapes,
        scratch_shapes=[pltpu.VMEM((seq, LANES), F32), pltpu.VMEM((seq, LANES), F32),
                        pltpu.VMEM((nc, N_CHAINS, ML_CHUNK), F32),
                        pltpu.VMEM((B_HEADS, nc, hd, ML_CHUNK), BF16), pltpu.VMEM((seq, B_DIM), BF16),
                        pltpu.VMEM((seq, B_DIM), F32), pltpu.VMEM((seq, B_DIM), F32),
                        pltpu.VMEM((N_CHAINS, hd, 2 * hd), F32)],
        compiler_params=_cparams(("arbitrary",), 56),
        name="mlstm",
    )(p, p, p, p, gates, wq, wk, bias_row, head_g, *(() if state is None else state))


def _merge0_kernel(u_ref, va_ref, mem_ref, x_ref, sw_ref, sb_ref, wo_ref, pg_ref, gate_ref, out_ref, cat_ref, *, tm):
    T = CHUNK
    gd = A_DIM // A_GROUPS
    for c in range(tm // T):
        sl = slice(c * T, (c + 1) * T)
        vn = _ln(_gelu(va_ref[0, sl, :].astype(F32))).astype(BF16)
        u = _gelu(u_ref[0, sl, :].astype(F32))
        for g in range(A_GROUPS):
            gs = slice(g * gd, (g + 1) * gd)
            mixed = _dot(sw_ref[g], vn[:, gs]) + sb_ref[g]
            cat_ref[sl, gs] = (u[:, gs] * mixed).astype(BF16)
    cat_ref[:, A_DIM:] = mem_ref[0]
    y = _dot(cat_ref[...], wo_ref[...])
    out_ref[0] = x_ref[0] + gate_ref[0] * _rms(y, pg_ref[...])


def _merge0(p, mem, x, sgu_w, sgu_bb, w_out, post_g, gate, tm):
    nb = x.shape[0]
    d = D_MODEL
    per_block = gate.shape[0] != 1
    blocks_per_gate = nb // gate.shape[0]
    gate_map = (lambda i: (i // blocks_per_gate, 0, 0)) if per_block else (lambda i: (0, 0, 0))
    return pl.pallas_call(
        functools.partial(_merge0_kernel, tm=tm),
        grid=(nb,),
        in_specs=[pl.BlockSpec((1, tm, A_DIM), lambda i: (i, 0, 4)),
                  pl.BlockSpec((1, tm, A_DIM), lambda i: (i, 0, 5)),
                  pl.BlockSpec((1, tm, B_DIM), lambda i: (i, 0, 0)),
                  pl.BlockSpec((1, tm, d), lambda i: (i, 0, 0)),
                  pl.BlockSpec((A_GROUPS, CHUNK, CHUNK), lambda i: (0, 0, 0)),
                  pl.BlockSpec((A_GROUPS, CHUNK, CHUNK), lambda i: (0, 0, 0)),
                  pl.BlockSpec((A_DIM + B_DIM, d), lambda i: (0, 0)),
                  pl.BlockSpec((1, d), lambda i: (0, 0)),
                  pl.BlockSpec((1, 1, d), gate_map)],
        out_specs=pl.BlockSpec((1, tm, d), lambda i: (i, 0, 0)),
        out_shape=jax.ShapeDtypeStruct((nb, tm, d), F32),
        scratch_shapes=[pltpu.VMEM((tm, A_DIM + B_DIM), BF16)],
        compiler_params=_cparams(("arbitrary",), 32),
        name="merge0",
    )(p, p, mem, x, sgu_w, sgu_bb, w_out, post_g.reshape(1, d), gate)


FFN_TK = MXU_WIDTH
FFN_HALO = 64
FFN_MM_ROWS = 1024


def _ffn_kernel(*refs, rows, tile_rows, width, grid2d, n_tiles):
    if grid2d:
        (x_ref, xu_ref, xd_ref, g_ref, sh_ref, sc_ref, wup_ref, cw_ref, wd_ref, pg_ref, gate_ref,
         o_ref, h_ref, gs0_ref, gs1_ref, act_ref) = refs
    else:
        (x_ref, g_ref, sh_ref, sc_ref, wup_ref, cw_ref, wd_ref, pg_ref, gate_ref,
         o_ref, h_ref, gs0_ref, gs1_ref, act_ref) = refs
    halo = FFN_HALO if grid2d else 0
    tk = FFN_TK
    ks = D_FF // tk
    hrows = rows + 2 * halo
    sub = 512

    def normmod(xv):
        return _rms(xv, g_ref[...]) * (1.0 + sc_ref[0]) + sh_ref[0]

    def up_slab(idx):
        return wup_ref[:, pl.ds(pl.multiple_of(idx * tk, tk), tk)]

    mm_rows = min(FFN_MM_ROWS, rows)
    g_step = hrows // (hrows // mm_rows)
    g_chunks = [(r0, r0 + g_step) for r0 in range(0, hrows, g_step)]

    def gate_chunk(slab, gs_ref, r0, r1):
        gs_ref[r0:r1, :] = _dot(h_ref[r0:r1, :], up_slab(ks + slab))

    def gate_branch(slab, gs_ref):
        for r0, r1 in g_chunks:
            gate_chunk(slab, gs_ref, r0, r1)

    pieces = [(halo + r0, halo + r0 + sub, lambda r0=r0: normmod(x_ref[0, r0:r0 + sub, :])) for r0 in range(0, rows, sub)]
    if grid2d:
        t = pl.program_id(0) % n_tiles
        pieces = ([(0, halo, lambda: normmod(xu_ref[0]) * (t > 0).astype(F32))] + pieces
                  + [(halo + rows, hrows, lambda: normmod(xd_ref[0]) * (t < n_tiles - 1).astype(F32))])
    pending = list(g_chunks)
    for p0, p1, piece in pieces:
        h_ref[p0:p1, :] = piece().astype(BF16)
        while pending and pending[0][1] <= p1:
            gate_chunk(0, gs0_ref, *pending.pop(0))
    assert not pending

    iota8 = lax.broadcasted_iota(jnp.int32, (8, tk), 0)

    def cut(vals, first):
        pieces = []
        for g0 in range(0, tile_rows, width):
            if first:
                pieces += [jnp.where(iota8 == 0, 0.0, vals[g0:g0 + 8]), vals[g0 + 8:g0 + width]]
            else:
                pieces += [vals[g0:g0 + width - 8], jnp.where(iota8 == 7, 0.0, vals[g0 + width - 8:g0 + width])]
        return jnp.concatenate(pieces, axis=0)

    def tiles(slab, gs_ref):
        cw = cw_ref[slab]
        for base in range(0, rows, tile_rows):
            if base % mm_rows == 0:
                a_rows = _dot(h_ref[halo + base:halo + base + mm_rows, :], up_slab(slab))
            a = a_rows[base % mm_rows:base % mm_rows + tile_rows]
            dys = (0, 1, 2) if grid2d else (1,)
            g_rows = [gs_ref[base + dy * halo:base + dy * halo + tile_rows, :] for dy in dys]
            left, mid, right = (functools.reduce(lambda s, t: s + t,
                                                 [cw[3 * dy + dx:3 * dy + dx + 1] * g for dy, g in zip(dys, g_rows)])
                                for dx in range(3))
            conv = (mid + cut(pltpu.roll(left, 1, 0), True) + cut(pltpu.roll(right, tile_rows - 1, 0), False))
            t = jnp.tanh(conv * (0.7978845608028654 + (0.7978845608028654 * 0.044715) * (conv * conv)))
            p = a * conv
            act_ref[slab, base:base + tile_rows, :] = (p + p * t).astype(BF16)

    assert ks % 2 == 1

    def body(j, carry):
        k0 = 2 * j
        gate_branch(k0 + 1, gs1_ref)
        tiles(k0, gs0_ref)
        gate_branch(k0 + 2, gs0_ref)
        tiles(k0 + 1, gs1_ref)
        return carry

    lax.fori_loop(0, ks // 2, body, 0)
    tiles(ks - 1, gs0_ref)

    down_rows = mm_rows
    for r0 in range(0, rows, down_rows):
        for n0 in range(0, D_MODEL, tk):
            f = _dot(act_ref[0, r0:r0 + down_rows, :], wd_ref[0, :, n0:n0 + tk])
            for k in range(1, ks):
                f = f + _dot(act_ref[k, r0:r0 + down_rows, :], wd_ref[k, :, n0:n0 + tk])
            o_ref[0, r0:r0 + down_rows, n0:n0 + tk] = f
    for r0 in range(0, rows, sub):
        f = o_ref[0, r0:r0 + sub, :]
        o_ref[0, r0:r0 + sub, :] = x_ref[0, r0:r0 + sub, :] + gate_ref[0] * _rms(f, pg_ref[...])


def _ffn(x3, pre_g, shift, scale, w_up, conv3, w_down3, post_g, gate, *, rows, width, grid2d):
    nb, seq, d = x3.shape
    tk = FFN_TK
    ks = D_FF // tk
    n_tiles = seq // rows
    tile_rows = max(width, 128)
    halo = FFN_HALO if grid2d else 0
    assert seq % rows == 0 and rows % 512 == 0 and tile_rows % width == 0 and (not grid2d or width == FFN_HALO)
    per_block = shift.shape[0] != 1
    mod_map = (lambda i: (i // n_tiles, 0, 0)) if per_block else (lambda i: (0, 0, 0))
    vec = lambda: pl.BlockSpec((1, d), lambda i: (0, 0))
    mod = lambda: pl.BlockSpec((1, 1, d), mod_map)
    resident = lambda shape: pl.BlockSpec(shape, lambda i: (0,) * len(shape), pipeline_mode=pl.Buffered(1))
    in_specs = [pl.BlockSpec((1, rows, d), lambda i: (i // n_tiles, i % n_tiles, 0))]
    args = [x3]
    if grid2d:
        per = rows // halo
        last = seq // halo - 1
        in_specs += [pl.BlockSpec((1, halo, d), lambda i: (i // n_tiles, jnp.maximum((i % n_tiles) * per - 1, 0), 0)),
                     pl.BlockSpec((1, halo, d), lambda i: (i // n_tiles, jnp.minimum((i % n_tiles + 1) * per, last), 0))]
        args += [x3, x3]
    in_specs += [vec(), mod(), mod(),
                 resident((d, 2 * ks * tk)), resident((ks, 9, tk)), resident((ks, tk, d)),
                 vec(), mod()]
    args += [pre_g.reshape(1, d), shift, scale, w_up, conv3, w_down3, post_g.reshape(1, d), gate]
    return pl.pallas_call(
        functools.partial(_ffn_kernel, rows=rows, tile_rows=tile_rows, width=width, grid2d=grid2d, n_tiles=n_tiles),
        grid=(nb * n_tiles,),
        in_specs=in_specs,
        out_specs=pl.BlockSpec((1, rows, d), lambda i: (i // n_tiles, i % n_tiles, 0)),
        out_shape=jax.ShapeDtypeStruct((nb, seq, d), F32),
        scratch_shapes=[pltpu.VMEM((rows + 2 * halo, d), BF16),
                        pltpu.VMEM((rows + 2 * halo, tk), F32),
                        pltpu.VMEM((rows + 2 * halo, tk), F32),
                        pltpu.VMEM((ks, rows, tk), BF16)],
        compiler_params=_cparams(("arbitrary",), 56),
        name="ffn",
    )(*args)


RET_CHUNK = 256


def _ret_kernel(k_ref, v_ref, q_ref, kc_ref, vc_ref, dec_ref, r_ref, rf_ref, rb_ref, *, seq, ctx_len):
    T = RET_CHUNK
    nc = seq // T
    t_col = lax.broadcasted_iota(jnp.int32, (T, 1), 0).astype(F32)
    lg = [_log_sigmoid(dec_ref[0, direction][0:1, 0:1]) for direction in range(2)]
    scale = C_QK_DIM ** -0.5
    zeta = (jnp.exp(lg[0] * (T - 1 - t_col)) * scale, jnp.exp(lg[1] * t_col) * scale)
    xi = (jnp.exp(lg[0] * (t_col + 1.0)), jnp.exp(lg[1] * (T - t_col)))
    gch = (jnp.exp(lg[0] * T), jnp.exp(lg[1] * T))
    row = lax.broadcasted_iota(jnp.int32, (T, T), 0)
    col = lax.broadcasted_iota(jnp.int32, (T, T), 1)
    rel = (row - col).astype(F32)
    d_intra = (jnp.where(row >= col, jnp.exp(lg[0] * rel), 0.0)
               + jnp.where(col >= row, jnp.exp(-lg[1] * rel), 0.0)) * scale

    def absorb(state_ref, direction, k, v, first):
        upd = _dot_tn((k.astype(F32) * zeta[direction]).astype(BF16), v)
        state_ref[...] = upd if first else gch[direction] * state_ref[...] + upd

    ncc = ctx_len // T
    for direction, state_ref in ((0, rf_ref), (1, rb_ref)):
        order = list(range(ncc)) if direction == 0 else list(range(ncc - 1, -1, -1))
        for n, c in enumerate(order):
            absorb(state_ref, direction, kc_ref[0, c * T:(c + 1) * T, :], vc_ref[0, c * T:(c + 1) * T, :], n == 0)

    def scan(i, first_visit):
        slf = pl.ds(pl.multiple_of(i * T, T), T)
        slb = pl.ds(pl.multiple_of((nc - 1 - i) * T, T), T)
        q, k, v = q_ref[0, slf, :], k_ref[0, slf, :], v_ref[0, slf, :]
        scores = _dot_nt(q, k) * d_intra
        fwd = _dot(scores.astype(BF16), v) + xi[0] * _dot(q, rf_ref[...].astype(BF16))
        absorb(rf_ref, 0, k, v, False)
        qb, kb, vb = q_ref[0, slb, :], k_ref[0, slb, :], v_ref[0, slb, :]
        bwd = xi[1] * _dot(qb, rb_ref[...].astype(BF16))
        absorb(rb_ref, 1, kb, vb, False)
        if first_visit:
            r_ref[0, slf, :] = fwd
            r_ref[0, slb, :] = bwd
        else:
            r_ref[0, slf, :] += fwd
            r_ref[0, slb, :] += bwd

    assert nc % 2 == 0

    def first_half(i, carry):
        scan(i, True)
        return carry

    def second_half(i, carry):
        scan(i, False)
        return carry

    lax.fori_loop(0, nc // 2, first_half, 0, unroll=True)
    lax.fori_loop(nc // 2, nc, second_half, 0, unroll=True)


def _ret(p, p_ctx, dec_b):
    b, seq, _ = p.shape
    ctx_len = p_ctx.shape[1]
    assert seq % RET_CHUNK == 0 and ctx_len % RET_CHUNK == 0
    return pl.pallas_call(
        functools.partial(_ret_kernel, seq=seq, ctx_len=ctx_len),
        grid=(b, C_HEADS),
        in_specs=[pl.BlockSpec((1, seq, C_QK_DIM), lambda i, h: (i, 0, h)),
                  pl.BlockSpec((1, seq, C_V_DIM), lambda i, h: (i, 0, 2 + h)),
                  pl.BlockSpec((1, seq, C_QK_DIM), lambda i, h: (i, 0, 12 + h)),
                  pl.BlockSpec((1, ctx_len, C_QK_DIM), lambda i, h: (i, 0, h)),
                  pl.BlockSpec((1, ctx_len, C_V_DIM), lambda i, h: (i, 0, 2 + h)),
                  pl.BlockSpec((1, 2, 8, LANES), lambda i, h: (h, 0, 0, 0))],
        out_specs=pl.BlockSpec((1, seq, C_V_DIM), lambda i, h: (i, 0, h)),
        out_shape=jax.ShapeDtypeStruct((b, seq, C_HEADS * C_V_DIM), F32),
        scratch_shapes=[pltpu.VMEM((C_QK_DIM, C_V_DIM), F32), pltpu.VMEM((C_QK_DIM, C_V_DIM), F32)],
        compiler_params=_cparams(("arbitrary", "arbitrary"), 48),
        name="ret",
    )(p, p, p, p_ctx, p_ctx, dec_b)


def _merge1_kernel(r_ref, g_ref, x_ref, hg_ref, wo_ref, pg_ref, gate_ref, out_ref, z_ref):
    for h in range(C_HEADS):
        hs = slice(h * C_V_DIM, (h + 1) * C_V_DIM)
        z = _silu(g_ref[0, :, hs].astype(F32)) * (_ln(r_ref[0, :, hs]) * hg_ref[:, hs])
        z_ref[:, hs] = z.astype(BF16)
    y = _dot(z_ref[...], wo_ref[...])
    out_ref[0] = x_ref[0] + gate_ref[0] * _rms(y, pg_ref[...])


def _merge1(r, p, x, head_g, w_out, post_g, gate, tm):
    nb = x.shape[0]
    d = D_MODEL
    hv = C_HEADS * C_V_DIM
    blocks_per_gate = nb // gate.shape[0]
    return pl.pallas_call(
        _merge1_kernel,
        grid=(nb,),
        in_specs=[pl.BlockSpec((1, tm, hv), lambda i: (i, 0, 0)),
                  pl.BlockSpec((1, tm, hv), lambda i: (i, 0, 2)),
                  pl.BlockSpec((1, tm, d), lambda i: (i, 0, 0)),
                  pl.BlockSpec((1, hv), lambda i: (0, 0)),
                  pl.BlockSpec((hv, d), lambda i: (0, 0)),
                  pl.BlockSpec((1, d), lambda i: (0, 0)),
                  pl.BlockSpec((1, 1, d), lambda i: (i // blocks_per_gate, 0, 0))],
        out_specs=pl.BlockSpec((1, tm, d), lambda i: (i, 0, 0)),
        out_shape=jax.ShapeDtypeStruct((nb, tm, d), F32),
        scratch_shapes=[pltpu.VMEM((tm, hv), BF16)],
        compiler_params=_cparams(("arbitrary",), 48),
        name="merge1",
    )(r, p, x, head_g, w_out, post_g.reshape(1, d), gate)


def kernel(x, c, ctx, c_ctx, ada_w, ada_b, pre_g, post_g, ffn_up, ffn_conv, ffn_down,
           ab_w_in, ab_qk_conv, ab_gate_b, ab_sgu_w, ab_sgu_b, ab_head_g, ab_w_out,
           ret_w_in, ret_decay, ret_head_g, ret_w_out):
    b, seq, d = x.shape
    ctx_len = ctx.shape[1]
    depth = ada_w.shape[0]
    tm = 512
    ctx_rows = min(2048, b * ctx_len)

    n_cond = b + 1
    n_pad = -n_cond % 8
    cs = jnp.concatenate([c, c_ctx[None, :], jnp.zeros((n_pad, d), F32)], axis=0)
    mods = _adaln(cs, ada_w, ada_b)

    def mod_l(layer, i):
        return mods[layer, :b, i * d:(i + 1) * d].reshape(b, 1, d)

    def mod_c(layer, i):
        return mods[layer, b:b + 1, i * d:(i + 1) * d].reshape(1, 1, d)

    n_slab = D_FF // FFN_TK
    ffn_up_b = ffn_up.astype(BF16)
    ffn_conv3 = ffn_conv.reshape(depth, 9, n_slab, FFN_TK).transpose(0, 2, 1, 3)
    ffn_down3 = (0.5 * ffn_down).astype(BF16).reshape(depth, n_slab, FFN_TK, d)

    def ffn_l(layer, xl):
        return _ffn(xl, pre_g[layer, 1], mod_l(layer, 3), mod_l(layer, 4), ffn_up_b[layer], ffn_conv3[layer],
                    ffn_down3[layer], post_g[layer, 1], mod_l(layer, 5), rows=1024, width=GRID_W, grid2d=True)

    def ffn_c(layer, xc):
        out = _ffn(xc.reshape(1, b * ctx_len, d), pre_g[layer, 1], mod_c(layer, 3), mod_c(layer, 4), ffn_up_b[layer],
                   ffn_conv3[layer], ffn_down3[layer], post_g[layer, 1], mod_c(layer, 5),
                   rows=min(1024, b * ctx_len), width=ctx_len, grid2d=False)
        return out.reshape(xc.shape)

    w_in = ab_w_in[0]
    g_lo = 2 * B_DIM
    g_hi = g_lo + N_GATES * B_HEADS
    w_main = jnp.concatenate([w_in[:, :g_lo], w_in[:, g_hi:]], axis=1).astype(BF16)
    w_gates = w_in[:, g_lo:g_hi].reshape(d, N_GATES, B_HEADS)
    lane_pad = ((0, 0), (0, LANES - N_CHAINS))
    w_gate = jnp.concatenate([jnp.pad(w_gates[:, 0::2].reshape(d, N_CHAINS), lane_pad),
                              jnp.pad(w_gates[:, 1::2].reshape(d, N_CHAINS), lane_pad)], axis=1).astype(BF16)
    bias_row = jnp.concatenate([jnp.pad(ab_gate_b[0][0::2].reshape(1, N_CHAINS), lane_pad),
                                jnp.pad(ab_gate_b[0][1::2].reshape(1, N_CHAINS), lane_pad)], axis=1)
    wq = ab_qk_conv[0][:, :B_DIM]
    wk = ab_qk_conv[0][:, B_DIM:]
    head_g0 = ab_head_g[0].reshape(1, B_DIM)
    sgu_w = ab_sgu_w[0].astype(BF16)
    sgu_bb = jnp.broadcast_to(ab_sgu_b[0][:, :, None], (A_GROUPS, CHUNK, CHUNK))
    w_out0 = ab_w_out[0].astype(BF16)

    p_c, gates_c = _proj(ctx.reshape(-1, ctx_rows, d), pre_g[0, 0], mod_c(0, 0), mod_c(0, 1), w_main, w_gate)
    p_l, gates_l = _proj(x, pre_g[0, 0], mod_l(0, 0), mod_l(0, 1), w_main, w_gate)
    p_c = p_c.reshape(b, ctx_len, -1)
    gates_c = gates_c.reshape(b, ctx_len, GATE_LANES)
    mem_c, *ctx_state = _mlstm(p_c, gates_c, wq, wk, bias_row, head_g0)
    mem_l, *_ = _mlstm(p_l, gates_l, wq, wk, bias_row, head_g0, tuple(ctx_state))

    rt = lambda t: t.reshape(-1, tm, t.shape[-1])
    x = _merge0(rt(p_l), rt(mem_l), rt(x), sgu_w, sgu_bb, w_out0, post_g[0, 0], mod_l(0, 2), tm).reshape(b, seq, d)
    ctx = _merge0(rt(p_c), rt(mem_c), rt(ctx), sgu_w, sgu_bb, w_out0, post_g[0, 0], mod_c(0, 2), tm).reshape(b, ctx_len, d)
    x = ffn_l(0, x)
    ctx = ffn_c(0, ctx)

    w_ret = ret_w_in[0].astype(BF16)
    kv_cols = C_HEADS * (C_QK_DIM + C_V_DIM)
    dec_b = jnp.broadcast_to(ret_decay[0].T[:, :, None, None], (C_HEADS, 2, 8, LANES))
    p_c = _proj(ctx.reshape(-1, ctx_rows, d), pre_g[1, 0], mod_c(1, 0), mod_c(1, 1), w_ret[:, :kv_cols])
    p_l = _proj(x, pre_g[1, 0], mod_l(1, 0), mod_l(1, 1), w_ret)
    r = _ret(p_l, p_c.reshape(b, ctx_len, -1), dec_b)
    x = _merge1(rt(r), rt(p_l), rt(x), ret_head_g[0].reshape(1, -1), ret_w_out[0].astype(BF16),
                post_g[1, 0], mod_l(1, 2), tm).reshape(b, seq, d)
    x = ffn_l(1, x)
    return x
```

```python
import functools

import jax
import jax.numpy as jnp
from jax import lax
from jax.experimental import pallas as pl
from jax.experimental.pallas import tpu as pltpu

F32 = jnp.float32
BF16 = jnp.bfloat16
HIGHEST = lax.Precision.HIGHEST

D_MODEL = 1024
GRID_W = 64
CHUNK = 128
EPS = 1e-6
N_MOD = 6
A_GROUPS = 4
A_DIM = 512
B_HEADS = 4
B_HEAD_DIM = 128
B_DIM = 512
N_GATES = 4
C_HEADS = 4
C_QK_DIM = 256
C_V_DIM = 512
D_FF = 2816

LANES = 128
MXU_WIDTH = 256
BF16_ROWS = 16
GATE_LANES = 2 * LANES
N_CHAINS = 2 * B_HEADS
ML_CHUNK = 256
V7X_VMEM_BYTES = 64 * 1024 * 1024
MIB = 1024 * 1024


def _cparams(semantics, vmem_mib):
    assert vmem_mib * MIB < V7X_VMEM_BYTES
    return pltpu.CompilerParams(dimension_semantics=semantics, vmem_limit_bytes=vmem_mib * MIB)


def _gelu(x):
    return x * (0.5 * (1.0 + jnp.tanh(0.7978845608028654 * (x + 0.044715 * (x * x * x)))))


def _silu(x):
    return x * jax.nn.sigmoid(x)


def _log_sigmoid(x):
    return jnp.minimum(x, 0.0) - jnp.log1p(jnp.exp(-jnp.abs(x)))


def _rms(x, g):
    return x * lax.rsqrt(jnp.mean(x * x, axis=-1, keepdims=True) + EPS) * g


def _ln(x):
    xc = x - jnp.mean(x, axis=-1, keepdims=True)
    return xc * lax.rsqrt(jnp.mean(xc * xc, axis=-1, keepdims=True) + EPS)


def _dot(a, b):
    return jnp.dot(a, b, preferred_element_type=F32)


def _dot_f32(a, b):
    return jnp.dot(a, b, precision=HIGHEST, preferred_element_type=F32)


def _dot_nt(a, b):
    return lax.dot_general(a, b, (((1,), (1,)), ((), ())), preferred_element_type=F32)


def _dot_tn(a, b):
    return lax.dot_general(a, b, (((0,), (0,)), ((), ())), preferred_element_type=F32)


def _adaln_kernel(c_ref, w_ref, b_ref, o_ref):
    s = _silu(c_ref[...])
    o_ref[0] = _dot(s.astype(BF16), w_ref[0].astype(BF16)) + b_ref[0]


def _adaln(cs, ada_w, ada_b):
    depth, d, n = ada_w.shape
    mp = cs.shape[0]
    tn = 1024
    return pl.pallas_call(
        _adaln_kernel,
        grid=(depth, n // tn),
        in_specs=[pl.BlockSpec((mp, d), lambda l, j: (0, 0)),
                  pl.BlockSpec((1, d, tn), lambda l, j: (l, 0, j)),
                  pl.BlockSpec((1, 1, tn), lambda l, j: (l, 0, j))],
        out_specs=pl.BlockSpec((1, mp, tn), lambda l, j: (l, 0, j)),
        out_shape=jax.ShapeDtypeStruct((depth, mp, n), F32),
        compiler_params=_cparams(("arbitrary", "arbitrary"), 32),
        name="adaln",
    )(cs, ada_w, ada_b.reshape(depth, 1, n))


def _proj_kernel(x_ref, g_ref, sh_ref, sc_ref, w_ref, *rest, rows, sub, with_gates):
    if with_gates:
        wg_ref, p_ref, gate_ref, h_ref = rest
    else:
        p_ref, h_ref = rest

    @pl.when(pl.program_id(1) == 0)
    def _():
        for r0 in range(0, rows, sub):
            h = _rms(x_ref[0, r0:r0 + sub, :], g_ref[...]) * (1.0 + sc_ref[0]) + sh_ref[0]
            hb = h.astype(BF16)
            h_ref[r0:r0 + sub, :] = hb
            if with_gates:
                gate_ref[0, r0:r0 + sub, :] = _dot(hb, wg_ref[...])
            p_ref[0, r0:r0 + sub, :] = _dot(hb, w_ref[...]).astype(BF16)

    @pl.when(pl.program_id(1) != 0)
    def _():
        tn = w_ref.shape[1]
        mm_rows = min(rows, 1024)
        for n0 in range(0, tn, MXU_WIDTH):
            for r0 in range(0, rows, mm_rows):
                p_ref[0, r0:r0 + mm_rows, n0:n0 + MXU_WIDTH] = _dot(h_ref[r0:r0 + mm_rows, :],
                                                                    w_ref[:, n0:n0 + MXU_WIDTH]).astype(BF16)


def _proj(x3, pre_g, shift, scale, w, wg=None):
    nb, rows, d = x3.shape
    n = w.shape[1]
    tn = 1024
    per_block = shift.shape[0] != 1
    mod_map = (lambda i, j: (i, 0, 0)) if per_block else (lambda i, j: (0, 0, 0))
    in_specs = [pl.BlockSpec((1, rows, d), lambda i, j: (i, 0, 0)),
                pl.BlockSpec((1, d), lambda i, j: (0, 0)),
                pl.BlockSpec((1, 1, d), mod_map),
                pl.BlockSpec((1, 1, d), mod_map),
                pl.BlockSpec((d, tn), lambda i, j: (0, j))]
    out_specs = [pl.BlockSpec((1, rows, tn), lambda i, j: (i, 0, j))]
    out_shape = [jax.ShapeDtypeStruct((nb, rows, n), BF16)]
    args = [x3, pre_g.reshape(1, d), shift, scale, w]
    if wg is not None:
        in_specs.append(pl.BlockSpec((d, GATE_LANES), lambda i, j: (0, 0)))
        out_specs.append(pl.BlockSpec((1, rows, GATE_LANES), lambda i, j: (i, 0, 0)))
        out_shape.append(jax.ShapeDtypeStruct((nb, rows, GATE_LANES), F32))
        args.append(wg)
    outs = pl.pallas_call(
        functools.partial(_proj_kernel, rows=rows, sub=min(512, rows), with_gates=wg is not None),
        grid=(nb, n // tn),
        in_specs=in_specs, out_specs=out_specs, out_shape=out_shape,
        scratch_shapes=[pltpu.VMEM((rows, d), BF16)],
        compiler_params=_cparams(("arbitrary", "arbitrary"), 48),
        name="proj",
    )(*args)
    return outs if wg is not None else outs[0]


def _mlstm_kernel(k_ref, v_ref, q_ref, o_ref, gates_ref, wq_ref, wk_ref, bias_ref, hg_ref, *rest, seq, zero_init):
    if zero_init:
        c0_ref = m0_ref = None
    else:
        c0_ref, m0_ref, *rest = rest
    mem_ref, cf_ref, mf_ref, pm_ref, cum_ref, brow_ref, kt_ref, qc_ref, rf_ref, rb_ref, st_ref = rest
    _mlstm_body(k_ref, v_ref, q_ref, o_ref, gates_ref, wq_ref, wk_ref, bias_ref, hg_ref, c0_ref, m0_ref,
                mem_ref, cf_ref, mf_ref, pm_ref, cum_ref, brow_ref, kt_ref, qc_ref, rf_ref, rb_ref, st_ref, seq=seq)


def _mlstm_body(k_ref, v_ref, q_ref, o_ref, gates_ref, wq_ref, wk_ref, bias_ref, hg_ref, c0_ref, m0_ref,
                mem_ref, cf_ref, mf_ref,
                pm_ref, cum_ref, brow_ref, kt_ref, qc_ref, rf_ref, rb_ref, st_ref, *, seq):
    T = ML_CHUNK
    hd = B_HEAD_DIM
    nc = seq // T
    row = lax.broadcasted_iota(jnp.int32, (T, T), 0)
    col = lax.broadcasted_iota(jnp.int32, (T, T), 1)
    lower = row >= col
    upper = col >= row
    fwd_lane = lax.broadcasted_iota(jnp.int32, (T, LANES), 1) < B_HEADS
    trow = lax.broadcasted_iota(jnp.int32, (T, LANES), 0)

    def gate_body(c, carry):
        sl = pl.ds(pl.multiple_of(c * T, T), T)
        g = gates_ref[0, sl, :] + bias_ref[...]
        lf = _log_sigmoid(g[:, LANES:])
        prefix = _dot_f32(lower.astype(F32), lf)
        cum = jnp.where(fwd_lane, prefix, prefix[T - 1:T] - prefix + lf)
        b = g[:, :LANES] - cum
        pf = b
        ps = b
        for sh in (1 << e for e in range(T.bit_length() - 1)):
            pf = jnp.where(trow >= sh, jnp.maximum(pf, pltpu.roll(pf, sh, 0)), pf)
            ps = jnp.where(trow < T - sh, jnp.maximum(ps, pltpu.roll(ps, T - sh, 0)), ps)
        pm_ref[sl, :] = jnp.where(fwd_lane, pf, ps)
        cum_ref[sl, :] = cum
        brow_ref[c] = b.T[:N_CHAINS]
        return carry

    ridx = lax.broadcasted_iota(jnp.int32, (T, hd), 0)

    def conv_body(c, carry):
        start = pl.multiple_of(c * T, T)
        sl = pl.ds(start, T)
        prev_start = pl.multiple_of(jnp.maximum(start - BF16_ROWS, 0), BF16_ROWS)
        next_start = pl.multiple_of(jnp.minimum(start + T, seq - BF16_ROWS), BF16_ROWS)
        has_prev = jnp.where(c > 0, 1.0, 0.0)
        has_next = jnp.where(c < nc - 1, 1.0, 0.0)
        for h in range(B_HEADS):
            hs = slice(h * hd, (h + 1) * hd)
            for src_ref, w_ref, is_k in ((k_ref, wk_ref, True), (q_ref, wq_ref, False)):
                cur = src_ref[0, sl, hs].astype(F32)
                prev_row = src_ref[0, pl.ds(prev_start, BF16_ROWS), hs].astype(F32)[BF16_ROWS - 1:] * has_prev
                next_row = src_ref[0, pl.ds(next_start, BF16_ROWS), hs].astype(F32)[0:1] * has_next
                w = w_ref[:, hs]
                xm = jnp.where(ridx == 0, prev_row, pltpu.roll(cur, 1, 0))
                xp = jnp.where(ridx == T - 1, next_row, pltpu.roll(cur, T - 1, 0))
                y = _silu(xm * w[0:1] + cur * w[1:2] + xp * w[2:3])
                if is_k:
                    kt_ref[h, c] = (y * hd ** -0.5).T.astype(BF16)
                else:
                    qc_ref[sl, hs] = y.astype(BF16)
        return carry

    lax.fori_loop(0, nc, lambda c, carry: conv_body(c, gate_body(c, carry)), 0)

    ones_tile = (lax.broadcasted_iota(jnp.int32, (T, hd), 1) == 0).astype(BF16)
    for j in range(N_CHAINS):
        st_ref[j] = jnp.zeros((hd, 2 * hd), F32) if c0_ref is None else c0_ref[0, j]
    m_init = tuple(jnp.zeros((1, 1), F32) if m0_ref is None else m0_ref[0, j][:, 0:1] for j in range(N_CHAINS))

    def scan_body(i, ms):
        chains = range(N_CHAINS)
        heads = [j % B_HEADS for j in chains]
        cs = [i if j < B_HEADS else nc - 1 - i for j in chains]
        sls = [pl.ds(pl.multiple_of(cs[j] * T, T), T) for j in chains]
        hss = [slice(h * hd, (h + 1) * hd) for h in heads]
        edge = [cs[j] * T + (T - 1 if j < B_HEADS else 0) for j in chains]
        b_row = [brow_ref[cs[j], j:j + 1, :] for j in chains]
        m_col = [jnp.maximum(ms[j], pm_ref[sls[j], j:j + 1]) for j in chains]
        m_wide = [jnp.broadcast_to(m_col[j], (T, T)) for j in chains]
        q = [qc_ref[sls[j], hss[j]] for j in chains]
        kt = [kt_ref[heads[j], cs[j]] for j in chains]
        dmat = [jnp.where(lower if j < B_HEADS else upper, jnp.exp(b_row[j] - m_wide[j]), 0.0) for j in chains]
        p = [(_dot(q[j], kt[j]) * dmat[j]).astype(BF16) for j in chains]
        v_aug = [jnp.concatenate([v_ref[0, sls[j], hss[j]], ones_tile], axis=1) for j in chains]
        st = [st_ref[j] for j in chains]
        qs = [(q[j].astype(F32) * jnp.exp(ms[j] - m_wide[j][:, :hd])).astype(BF16) for j in chains]
        acc = [_dot(p[j], v_aug[j]) + _dot(qs[j], st[j].astype(BF16)) for j in chains]
        for j in chains:
            out_ref = rf_ref if j < B_HEADS else rb_ref
            den = acc[j][:, hd:hd + 1]
            floor = jnp.exp(-(cum_ref[sls[j], j:j + 1] + m_col[j]))
            out_ref[sls[j], hss[j]] = acc[j][:, :hd] / jnp.maximum(jnp.abs(den), floor)
        new_ms = []
        for j in chains:
            btot = cum_ref[pl.ds(edge[j], 1), j:j + 1]
            m_far = jnp.maximum(ms[j], pm_ref[pl.ds(edge[j], 1), j:j + 1])
            m_new = btot + m_far
            kw_t = (kt[j].astype(F32) * jnp.exp(btot + b_row[j] - m_new)).astype(BF16)
            st_ref[j] = jnp.exp(ms[j] - m_far) * st[j] + _dot(kw_t, v_aug[j])
            new_ms.append(m_new)
        return tuple(new_ms)

    ms = lax.fori_loop(0, nc, scan_body, m_init, unroll=8)
    for j in range(N_CHAINS):
        cf_ref[0, j] = st_ref[j]
        mf_ref[0, j] = jnp.broadcast_to(ms[j], (1, hd))

    def out_body(c, carry):
        sl = pl.ds(pl.multiple_of(c * T, T), T)
        for h in range(B_HEADS):
            hs = slice(h * hd, (h + 1) * hd)
            y = jax.nn.sigmoid(o_ref[0, sl, hs].astype(F32)) * (rf_ref[sl, hs] + rb_ref[sl, hs])
            mem_ref[0, sl, hs] = (_ln(y) * hg_ref[:, hs]).astype(BF16)
        return carry

    lax.fori_loop(0, nc, out_body, 0)


def _mlstm(p, gates, wq, wk, bias_row, head_g, state=None):
    b, seq, _ = p.shape
    hd = B_HEAD_DIM
    state_shapes = [jax.ShapeDtypeStruct((b, N_CHAINS, hd, 2 * hd), F32), jax.ShapeDtypeStruct((b, N_CHAINS, 1, hd), F32)]
    assert seq % ML_CHUNK == 0
    nc = seq // ML_CHUNK
    sec = lambda idx: pl.BlockSpec((1, seq, B_DIM), lambda i, idx=idx: (i, 0, idx))
    full = lambda shape: pl.BlockSpec(shape, lambda i: (0,) * len(shape))
    st_c = pl.BlockSpec((1, N_CHAINS, hd, 2 * hd), lambda i: (i, 0, 0, 0))
    st_m = pl.BlockSpec((1, N_CHAINS, 1, hd), lambda i: (i, 0, 0, 0))
    return pl.pallas_call(
        functools.partial(_mlstm_kernel, seq=seq, zero_init=state is None),
        grid=(b,),
        in_specs=[sec(0), sec(1), sec(2), sec(3),
                  pl.BlockSpec((1, seq, GATE_LANES), lambda i: (i, 0, 0)),
                  full((3, B_DIM)), full((3, B_DIM)), full((1, GATE_LANES)), full((1, B_DIM))]
                 + ([] if state is None else [st_c, st_m]),
        out_specs=[pl.BlockSpec((1, seq, B_DIM), lambda i: (i, 0, 0)), st_c, st_m],
        out_shape=[jax.ShapeDtypeStruct((b, seq, B_DIM), BF16)] + state_shapes,
        scratch_shapes=[pltpu.VMEM((seq, LANES), F32), pltpu.VMEM((seq, LANES), F32),
                        pltpu.VMEM((nc, N_CHAINS, ML_CHUNK), F32),
                        pltpu.VMEM((B_HEADS, nc, hd, ML_CHUNK), BF16), pltpu.VMEM((seq, B_DIM), BF16),
                        pltpu.VMEM((seq, B_DIM), F32), pltpu.VMEM((seq, B_DIM), F32),
                        pltpu.VMEM((N_CHAINS, hd, 2 * hd), F32)],
        compiler_params=_cparams(("arbitrary",), 56),
        name="mlstm",
    )(p, p, p, p, gates, wq, wk, bias_row, head_g, *(() if state is None else state))


def _merge0_kernel(u_ref, va_ref, mem_ref, x_ref, sw_ref, sb_ref, wo_ref, pg_ref, gate_ref, out_ref, cat_ref, *, tm):
    T = CHUNK
    gd = A_DIM // A_GROUPS
    for c in range(tm // T):
        sl = slice(c * T, (c + 1) * T)
        vn = _ln(_gelu(va_ref[0, sl, :].astype(F32))).astype(BF16)
        u = _gelu(u_ref[0, sl, :].astype(F32))
        for g in range(A_GROUPS):
            gs = slice(g * gd, (g + 1) * gd)
            mixed = _dot(sw_ref[g], vn[:, gs]) + sb_ref[g]
            cat_ref[sl, gs] = (u[:, gs] * mixed).astype(BF16)
    cat_ref[:, A_DIM:] = mem_ref[0]
    y = _dot(cat_ref[...], wo_ref[...])
    out_ref[0] = x_ref[0] + gate_ref[0] * _rms(y, pg_ref[...])


def _merge0(p, mem, x, sgu_w, sgu_bb, w_out, post_g, gate, tm):
    nb = x.shape[0]
    d = D_MODEL
    per_block = gate.shape[0] != 1
    blocks_per_gate = nb // gate.shape[0]
    gate_map = (lambda i: (i // blocks_per_gate, 0, 0)) if per_block else (lambda i: (0, 0, 0))
    return pl.pallas_call(
        functools.partial(_merge0_kernel, tm=tm),
        grid=(nb,),
        in_specs=[pl.BlockSpec((1, tm, A_DIM), lambda i: (i, 0, 4)),
                  pl.BlockSpec((1, tm, A_DIM), lambda i: (i, 0, 5)),
                  pl.BlockSpec((1, tm, B_DIM), lambda i: (i, 0, 0)),
                  pl.BlockSpec((1, tm, d), lambda i: (i, 0, 0)),
                  pl.BlockSpec((A_GROUPS, CHUNK, CHUNK), lambda i: (0, 0, 0)),
                  pl.BlockSpec((A_GROUPS, CHUNK, CHUNK), lambda i: (0, 0, 0)),
                  pl.BlockSpec((A_DIM + B_DIM, d), lambda i: (0, 0)),
                  pl.BlockSpec((1, d), lambda i: (0, 0)),
                  pl.BlockSpec((1, 1, d), gate_map)],
        out_specs=pl.BlockSpec((1, tm, d), lambda i: (i, 0, 0)),
        out_shape=jax.ShapeDtypeStruct((nb, tm, d), F32),
        scratch_shapes=[pltpu.VMEM((tm, A_DIM + B_DIM), BF16)],
        compiler_params=_cparams(("arbitrary",), 32),
        name="merge0",
    )(p, p, mem, x, sgu_w, sgu_bb, w_out, post_g.reshape(1, d), gate)


FFN_TK = MXU_WIDTH
FFN_HALO = 64
FFN_MM_ROWS = 1024


def _ffn_kernel(*refs, rows, tile_rows, width, grid2d, n_tiles):
    if grid2d:
        (x_ref, xu_ref, xd_ref, g_ref, sh_ref, sc_ref, wup_ref, cw_ref, wd_ref, pg_ref, gate_ref,
         o_ref, h_ref, gs0_ref, gs1_ref, act_ref) = refs
    else:
        (x_ref, g_ref, sh_ref, sc_ref, wup_ref, cw_ref, wd_ref, pg_ref, gate_ref,
         o_ref, h_ref, gs0_ref, gs1_ref, act_ref) = refs
    halo = FFN_HALO if grid2d else 0
    tk = FFN_TK
    ks = D_FF // tk
    hrows = rows + 2 * halo
    sub = 512

    def normmod(xv):
        return _rms(xv, g_ref[...]) * (1.0 + sc_ref[0]) + sh_ref[0]

    def up_slab(idx):
        return wup_ref[:, pl.ds(pl.multiple_of(idx * tk, tk), tk)]

    mm_rows = min(FFN_MM_ROWS, rows)
    g_step = hrows // (hrows // mm_rows)
    g_chunks = [(r0, r0 + g_step) for r0 in range(0, hrows, g_step)]

    def gate_chunk(slab, gs_ref, r0, r1):
        gs_ref[r0:r1, :] = _dot(h_ref[r0:r1, :], up_slab(ks + slab))

    def gate_branch(slab, gs_ref):
        for r0, r1 in g_chunks:
            gate_chunk(slab, gs_ref, r0, r1)

    pieces = [(halo + r0, halo + r0 + sub, lambda r0=r0: normmod(x_ref[0, r0:r0 + sub, :])) for r0 in range(0, rows, sub)]
    if grid2d:
        t = pl.program_id(0) % n_tiles
        pieces = ([(0, halo, lambda: normmod(xu_ref[0]) * (t > 0).astype(F32))] + pieces
                  + [(halo + rows, hrows, lambda: normmod(xd_ref[0]) * (t < n_tiles - 1).astype(F32))])
    pending = list(g_chunks)
    for p0, p1, piece in pieces:
        h_ref[p0:p1, :] = piece().astype(BF16)
        while pending and pending[0][1] <= p1:
            gate_chunk(0, gs0_ref, *pending.pop(0))
    assert not pending

    iota8 = lax.broadcasted_iota(jnp.int32, (8, tk), 0)

    def cut(vals, first):
        pieces = []
        for g0 in range(0, tile_rows, width):
            if first:
                pieces += [jnp.where(iota8 == 0, 0.0, vals[g0:g0 + 8]), vals[g0 + 8:g0 + width]]
            else:
                pieces += [vals[g0:g0 + width - 8], jnp.where(iota8 == 7, 0.0, vals[g0 + width - 8:g0 + width])]
        return jnp.concatenate(pieces, axis=0)

    def tiles(slab, gs_ref):
        cw = cw_ref[slab]
        for base in range(0, rows, tile_rows):
            if base % mm_rows == 0:
                a_rows = _dot(h_ref[halo + base:halo + base + mm_rows, :], up_slab(slab))
            a = a_rows[base % mm_rows:base % mm_rows + tile_rows]
            dys = (0, 1, 2) if grid2d else (1,)
            g_rows = [gs_ref[base + dy * halo:base + dy * halo + tile_rows, :] for dy in dys]
            left, mid, right = (functools.reduce(lambda s, t: s + t,
                                                 [cw[3 * dy + dx:3 * dy + dx + 1] * g for dy, g in zip(dys, g_rows)])
                                for dx in range(3))
            conv = (mid + cut(pltpu.roll(left, 1, 0), True) + cut(pltpu.roll(right, tile_rows - 1, 0), False))
            t = jnp.tanh(conv * (0.7978845608028654 + (0.7978845608028654 * 0.044715) * (conv * conv)))
            p = a * conv
            act_ref[slab, base:base + tile_rows, :] = (p + p * t).astype(BF16)

    assert ks % 2 == 1

    def body(j, carry):
        k0 = 2 * j
        gate_branch(k0 + 1, gs1_ref)
        tiles(k0, gs0_ref)
        gate_branch(k0 + 2, gs0_ref)
        tiles(k0 + 1, gs1_ref)
        return carry

    lax.fori_loop(0, ks // 2, body, 0)
    tiles(ks - 1, gs0_ref)

    down_rows = mm_rows
    for r0 in range(0, rows, down_rows):
        for n0 in range(0, D_MODEL, tk):
            f = _dot(act_ref[0, r0:r0 + down_rows, :], wd_ref[0, :, n0:n0 + tk])
            for k in range(1, ks):
                f = f + _dot(act_ref[k, r0:r0 + down_rows, :], wd_ref[k, :, n0:n0 + tk])
            o_ref[0, r0:r0 + down_rows, n0:n0 + tk] = f
    for r0 in range(0, rows, sub):
        f = o_ref[0, r0:r0 + sub, :]
        o_ref[0, r0:r0 + sub, :] = x_ref[0, r0:r0 + sub, :] + gate_ref[0] * _rms(f, pg_ref[...])


def _ffn(x3, pre_g, shift, scale, w_up, conv3, w_down3, post_g, gate, *, rows, width, grid2d):
    nb, seq, d = x3.shape
    tk = FFN_TK
    ks = D_FF // tk
    n_tiles = seq // rows
    tile_rows = max(width, 128)
    halo = FFN_HALO if grid2d else 0
    assert seq % rows == 0 and rows % 512 == 0 and tile_rows % width == 0 and (not grid2d or width == FFN_HALO)
    per_block = shift.shape[0] != 1
    mod_map = (lambda i: (i // n_tiles, 0, 0)) if per_block else (lambda i: (0, 0, 0))
    vec = lambda: pl.BlockSpec((1, d), lambda i: (0, 0))
    mod = lambda: pl.BlockSpec((1, 1, d), mod_map)
    resident = lambda shape: pl.BlockSpec(shape, lambda i: (0,) * len(shape), pipeline_mode=pl.Buffered(1))
    in_specs = [pl.BlockSpec((1, rows, d), lambda i: (i // n_tiles, i % n_tiles, 0))]
    args = [x3]
    if grid2d:
        per = rows // halo
        last = seq // halo - 1
        in_specs += [pl.BlockSpec((1, halo, d), lambda i: (i // n_tiles, jnp.maximum((i % n_tiles) * per - 1, 0), 0)),
                     pl.BlockSpec((1, halo, d), lambda i: (i // n_tiles, jnp.minimum((i % n_tiles + 1) * per, last), 0))]
        args += [x3, x3]
    in_specs += [vec(), mod(), mod(),
                 resident((d, 2 * ks * tk)), resident((ks, 9, tk)), resident((ks, tk, d)),
                 vec(), mod()]
    args += [pre_g.reshape(1, d), shift, scale, w_up, conv3, w_down3, post_g.reshape(1, d), gate]
    return pl.pallas_call(
        functools.partial(_ffn_kernel, rows=rows, tile_rows=tile_rows, width=width, grid2d=grid2d, n_tiles=n_tiles),
        grid=(nb * n_tiles,),
        in_specs=in_specs,
        out_specs=pl.BlockSpec((1, rows, d), lambda i: (i // n_tiles, i % n_tiles, 0)),
        out_shape=jax.ShapeDtypeStruct((nb, seq, d), F32),
        scratch_shapes=[pltpu.VMEM((rows + 2 * halo, d), BF16),
                        pltpu.VMEM((rows + 2 * halo, tk), F32),
                        pltpu.VMEM((rows + 2 * halo, tk), F32),
                        pltpu.VMEM((ks, rows, tk), BF16)],
        compiler_params=_cparams(("arbitrary",), 56),
        name="ffn",
    )(*args)


RET_CHUNK = 256


def _ret_kernel(k_ref, v_ref, q_ref, kc_ref, vc_ref, dec_ref, r_ref, rf_ref, rb_ref, *, seq, ctx_len):
    T = RET_CHUNK
    nc = seq // T
    t_col = lax.broadcasted_iota(jnp.int32, (T, 1), 0).astype(F32)
    lg = [_log_sigmoid(dec_ref[0, direction][0:1, 0:1]) for direction in range(2)]
    scale = C_QK_DIM ** -0.5
    zeta = (jnp.exp(lg[0] * (T - 1 - t_col)) * scale, jnp.exp(lg[1] * t_col) * scale)
    xi = (jnp.exp(lg[0] * (t_col + 1.0)), jnp.exp(lg[1] * (T - t_col)))
    gch = (jnp.exp(lg[0] * T), jnp.exp(lg[1] * T))
    row = lax.broadcasted_iota(jnp.int32, (T, T), 0)
    col = lax.broadcasted_iota(jnp.int32, (T, T), 1)
    rel = (row - col).astype(F32)
    d_intra = (jnp.where(row >= col, jnp.exp(lg[0] * rel), 0.0)
               + jnp.where(col >= row, jnp.exp(-lg[1] * rel), 0.0)) * scale

    def absorb(state_ref, direction, k, v, first):
        upd = _dot_tn((k.astype(F32) * zeta[direction]).astype(BF16), v)
        state_ref[...] = upd if first else gch[direction] * state_ref[...] + upd

    ncc = ctx_len // T
    for direction, state_ref in ((0, rf_ref), (1, rb_ref)):
        order = list(range(ncc)) if direction == 0 else list(range(ncc - 1, -1, -1))
        for n, c in enumerate(order):
            absorb(state_ref, direction, kc_ref[0, c * T:(c + 1) * T, :], vc_ref[0, c * T:(c + 1) * T, :], n == 0)

    def scan(i, first_visit):
        slf = pl.ds(pl.multiple_of(i * T, T), T)
        slb = pl.ds(pl.multiple_of((nc - 1 - i) * T, T), T)
        q, k, v = q_ref[0, slf, :], k_ref[0, slf, :], v_ref[0, slf, :]
        scores = _dot_nt(q, k) * d_intra
        fwd = _dot(scores.astype(BF16), v) + xi[0] * _dot(q, rf_ref[...].astype(BF16))
        absorb(rf_ref, 0, k, v, False)
        qb, kb, vb = q_ref[0, slb, :], k_ref[0, slb, :], v_ref[0, slb, :]
        bwd = xi[1] * _dot(qb, rb_ref[...].astype(BF16))
        absorb(rb_ref, 1, kb, vb, False)
        if first_visit:
            r_ref[0, slf, :] = fwd
            r_ref[0, slb, :] = bwd
        else:
            r_ref[0, slf, :] += fwd
            r_ref[0, slb, :] += bwd

    assert nc % 2 == 0

    def first_half(i, carry):
        scan(i, True)
        return carry

    def second_half(i, carry):
        scan(i, False)
        return carry

    lax.fori_loop(0, nc // 2, first_half, 0, unroll=True)
    lax.fori_loop(nc // 2, nc, second_half, 0, unroll=True)


def _ret(p, p_ctx, dec_b):
    b, seq, _ = p.shape
    ctx_len = p_ctx.shape[1]
    assert seq % RET_CHUNK == 0 and ctx_len % RET_CHUNK == 0
    return pl.pallas_call(
        functools.partial(_ret_kernel, seq=seq, ctx_len=ctx_len),
        grid=(b, C_HEADS),
        in_specs=[pl.BlockSpec((1, seq, C_QK_DIM), lambda i, h: (i, 0, h)),
                  pl.BlockSpec((1, seq, C_V_DIM), lambda i, h: (i, 0, 2 + h)),
                  pl.BlockSpec((1, seq, C_QK_DIM), lambda i, h: (i, 0, 12 + h)),
                  pl.BlockSpec((1, ctx_len, C_QK_DIM), lambda i, h: (i, 0, h)),
                  pl.BlockSpec((1, ctx_len, C_V_DIM), lambda i, h: (i, 0, 2 + h)),
                  pl.BlockSpec((1, 2, 8, LANES), lambda i, h: (h, 0, 0, 0))],
        out_specs=pl.BlockSpec((1, seq, C_V_DIM), lambda i, h: (i, 0, h)),
        out_shape=jax.ShapeDtypeStruct((b, seq, C_HEADS * C_V_DIM), F32),
        scratch_shapes=[pltpu.VMEM((C_QK_DIM, C_V_DIM), F32), pltpu.VMEM((C_QK_DIM, C_V_DIM), F32)],
        compiler_params=_cparams(("arbitrary", "arbitrary"), 48),
        name="ret",
    )(p, p, p, p_ctx, p_ctx, dec_b)


def _merge1_kernel(r_ref, g_ref, x_ref, hg_ref, wo_ref, pg_ref, gate_ref, out_ref, z_ref):
    for h in range(C_HEADS):
        hs = slice(h * C_V_DIM, (h + 1) * C_V_DIM)
        z = _silu(g_ref[0, :, hs].astype(F32)) * (_ln(r_ref[0, :, hs]) * hg_ref[:, hs])
        z_ref[:, hs] = z.astype(BF16)
    y = _dot(z_ref[...], wo_ref[...])
    out_ref[0] = x_ref[0] + gate_ref[0] * _rms(y, pg_ref[...])


def _merge1(r, p, x, head_g, w_out, post_g, gate, tm):
    nb = x.shape[0]
    d = D_MODEL
    hv = C_HEADS * C_V_DIM
    blocks_per_gate = nb // gate.shape[0]
    return pl.pallas_call(
        _merge1_kernel,
        grid=(nb,),
        in_specs=[pl.BlockSpec((1, tm, hv), lambda i: (i, 0, 0)),
                  pl.BlockSpec((1, tm, hv), lambda i: (i, 0, 2)),
                  pl.BlockSpec((1, tm, d), lambda i: (i, 0, 0)),
                  pl.BlockSpec((1, hv), lambda i: (0, 0)),
                  pl.BlockSpec((hv, d), lambda i: (0, 0)),
                  pl.BlockSpec((1, d), lambda i: (0, 0)),
                  pl.BlockSpec((1, 1, d), lambda i: (i // blocks_per_gate, 0, 0))],
        out_specs=pl.BlockSpec((1, tm, d), lambda i: (i, 0, 0)),
        out_shape=jax.ShapeDtypeStruct((nb, tm, d), F32),
        scratch_shapes=[pltpu.VMEM((tm, hv), BF16)],
        compiler_params=_cparams(("arbitrary",), 48),
        name="merge1",
    )(r, p, x, head_g, w_out, post_g.reshape(1, d), gate)


def kernel(x, c, ctx, c_ctx, ada_w, ada_b, pre_g, post_g, ffn_up, ffn_conv, ffn_down,
           ab_w_in, ab_qk_conv, ab_gate_b, ab_sgu_w, ab_sgu_b, ab_head_g, ab_w_out,
           ret_w_in, ret_decay, ret_head_g, ret_w_out):
    b, seq, d = x.shape
    ctx_len = ctx.shape[1]
    depth = ada_w.shape[0]
    tm = 512
    ctx_rows = min(2048, b * ctx_len)

    n_cond = b + 1
    n_pad = -n_cond % 8
    cs = jnp.concatenate([c, c_ctx[None, :], jnp.zeros((n_pad, d), F32)], axis=0)
    mods = _adaln(cs, ada_w, ada_b)

    def mod_l(layer, i):
        return mods[layer, :b, i * d:(i + 1) * d].reshape(b, 1, d)

    def mod_c(layer, i):
        return mods[layer, b:b + 1, i * d:(i + 1) * d].reshape(1, 1, d)

    n_slab = D_FF // FFN_TK
    ffn_up_b = ffn_up.astype(BF16)
    ffn_conv3 = ffn_conv.reshape(depth, 9, n_slab, FFN_TK).transpose(0, 2, 1, 3)
    ffn_down3 = (0.5 * ffn_down).astype(BF16).reshape(depth, n_slab, FFN_TK, d)

    def ffn_l(layer, xl):
        return _ffn(xl, pre_g[layer, 1], mod_l(layer, 3), mod_l(layer, 4), ffn_up_b[layer], ffn_conv3[layer],
                    ffn_down3[layer], post_g[layer, 1], mod_l(layer, 5), rows=1024, width=GRID_W, grid2d=True)

    def ffn_c(layer, xc):
        out = _ffn(xc.reshape(1, b * ctx_len, d), pre_g[layer, 1], mod_c(layer, 3), mod_c(layer, 4), ffn_up_b[layer],
                   ffn_conv3[layer], ffn_down3[layer], post_g[layer, 1], mod_c(layer, 5),
                   rows=min(1024, b * ctx_len), width=ctx_len, grid2d=False)
        return out.reshape(xc.shape)

    w_in = ab_w_in[0]
    g_lo = 2 * B_DIM
    g_hi = g_lo + N_GATES * B_HEADS
    w_main = jnp.concatenate([w_in[:, :g_lo], w_in[:, g_hi:]], axis=1).astype(BF16)
    w_gates = w_in[:, g_lo:g_hi].reshape(d, N_GATES, B_HEADS)
    lane_pad = ((0, 0), (0, LANES - N_CHAINS))
    w_gate = jnp.concatenate([jnp.pad(w_gates[:, 0::2].reshape(d, N_CHAINS), lane_pad),
                              jnp.pad(w_gates[:, 1::2].reshape(d, N_CHAINS), lane_pad)], axis=1).astype(BF16)
    bias_row = jnp.concatenate([jnp.pad(ab_gate_b[0][0::2].reshape(1, N_CHAINS), lane_pad),
                                jnp.pad(ab_gate_b[0][1::2].reshape(1, N_CHAINS), lane_pad)], axis=1)
    wq = ab_qk_conv[0][:, :B_DIM]
    wk = ab_qk_conv[0][:, B_DIM:]
    head_g0 = ab_head_g[0].reshape(1, B_DIM)
    sgu_w = ab_sgu_w[0].astype(BF16)
    sgu_bb = jnp.broadcast_to(ab_sgu_b[0][:, :, None], (A_GROUPS, CHUNK, CHUNK))
    w_out0 = ab_w_out[0].astype(BF16)

    p_c, gates_c = _proj(ctx.reshape(-1, ctx_rows, d), pre_g[0, 0], mod_c(0, 0), mod_c(0, 1), w_main, w_gate)
    p_l, gates_l = _proj(x, pre_g[0, 0], mod_l(0, 0), mod_l(0, 1), w_main, w_gate)
    p_c = p_c.reshape(b, ctx_len, -1)
    gates_c = gates_c.reshape(b, ctx_len, GATE_LANES)
    mem_c, *ctx_state = _mlstm(p_c, gates_c, wq, wk, bias_row, head_g0)
    mem_l, *_ = _mlstm(p_l, gates_l, wq, wk, bias_row, head_g0, tuple(ctx_state))

    rt = lambda t: t.reshape(-1, tm, t.shape[-1])
    x = _merge0(rt(p_l), rt(mem_l), rt(x), sgu_w, sgu_bb, w_out0, post_g[0, 0], mod_l(0, 2), tm).reshape(b, seq, d)
    ctx = _merge0(rt(p_c), rt(mem_c), rt(ctx), sgu_w, sgu_bb, w_out0, post_g[0, 0], mod_c(0, 2), tm).reshape(b, ctx_len, d)
    x = ffn_l(0, x)
    ctx = ffn_c(0, ctx)

    w_ret = ret_w_in[0].astype(BF16)
    kv_cols = C_HEADS * (C_QK_DIM + C_V_DIM)
    dec_b = jnp.broadcast_to(ret_decay[0].T[:, :, None, None], (C_HEADS, 2, 8, LANES))
    p_c = _proj(ctx.reshape(-1, ctx_rows, d), pre_g[1, 0], mod_c(1, 0), mod_c(1, 1), w_ret[:, :kv_cols])
    p_l = _proj(x, pre_g[1, 0], mod_l(1, 0), mod_l(1, 1), w_ret)
    r = _ret(p_l, p_c.reshape(b, ctx_len, -1), dec_b)
    x = _merge1(rt(r), rt(p_l), rt(x), ret_head_g[0].reshape(1, -1), ret_w_out[0].astype(BF16),
                post_g[1, 0], mod_l(1, 2), tm).reshape(b, seq, d)
    x = ffn_l(1, x)
    return x
```
